```python
import math
import jax, jax.numpy as jnp
from jax import lax
import numpy as np

D_MODEL = 1024
BATCH = 8
SEQ = 2048
DEPTH = 4

CHUNK = 64
N_A_LAYERS = DEPTH // 2
N_B_LAYERS = DEPTH - N_A_LAYERS
A_HEADS = 4
A_DV = D_MODEL // A_HEADS
A_DQK = A_DV // 2
A_QK_W = A_HEADS * A_DQK
A_V_W = A_HEADS * A_DV
A_PROJ = 2 * A_QK_W + 2 * A_V_W + 2 * A_HEADS
B_HEADS = 16
B_DH = D_MODEL // B_HEADS
B_KV_PROJ = 2 * D_MODEL + B_HEADS
Q_BLOCK = 128
D_FF = 4 * D_MODEL
EPS = 1e-6

kernel_name = "yoco_mlstm_fox_adaln_encoder"


def rms_norm(x, gain=None):
    xf = x.astype(jnp.float32)
    y = xf * lax.rsqrt(jnp.mean(xf * xf, axis=-1, keepdims=True) + EPS)
    if gain is not None:
        y = y * gain.astype(jnp.float32)
    return y.astype(x.dtype)


def modulate(x, shift, scale):
    return rms_norm(x) * (1 + scale[:, None, :]) + shift[:, None, :]


def mlstm_chunkwise(q, k, v, i_pre, logf):
    B, H, S, DQK = q.shape
    DV = v.shape[-1]
    nc = S // CHUNK

    def chunks(t):
        return jnp.moveaxis(t.reshape(B, H, nc, CHUNK, *t.shape[3:]), 2, 0)

    causal = jnp.asarray(np.tril(np.ones((CHUNK, CHUNK), dtype=bool)))

    def step(carry, xs):
        C, n, m = carry
        qc, kc, vc, ic, fc = xs
        b = jnp.cumsum(fc, axis=-1)
        d = b[..., :, None] - b[..., None, :] + ic[..., None, :]
        d = jnp.where(causal, d, -jnp.inf)
        inter = b + m[..., None]
        m_t = jnp.maximum(inter, jnp.max(d, axis=-1))
        w = jnp.exp(d - m_t[..., None])
        w_inter = jnp.exp(inter - m_t)
        a = w * jnp.einsum('bhtd,bhsd->bhts', qc, kc)
        num = (w_inter[..., None] * jnp.einsum('bhtd,bhde->bhte', qc, C)
               + jnp.einsum('bhts,bhse->bhte', a, vc))
        den = w_inter * jnp.einsum('bhtd,bhd->bht', qc, n) + jnp.sum(a, axis=-1)
        h = num / jnp.maximum(jnp.abs(den), jnp.exp(-m_t))[..., None]
        bL = b[..., -1]
        dl = bL[..., None] - b + ic
        m_new = jnp.maximum(bL + m, jnp.max(dl, axis=-1))
        wl = jnp.exp(dl - m_new[..., None])
        decay = jnp.exp(bL + m - m_new)
        C_new = decay[..., None, None] * C + jnp.einsum('bhs,bhsd,bhse->bhde', wl, kc, vc)
        n_new = decay[..., None] * n + jnp.einsum('bhs,bhsd->bhd', wl, kc)
        return (C_new, n_new, m_new), h

    init = (jnp.zeros((B, H, DQK, DV), jnp.float32),
            jnp.zeros((B, H, DQK), jnp.float32),
            jnp.zeros((B, H), jnp.float32))
    _, hs = lax.scan(step, init, (chunks(q), chunks(k), chunks(v), chunks(i_pre), chunks(logf)))
    return jnp.moveaxis(hs, 0, 2).reshape(B, H, S, DV)


def mlstm_mixer(h, w_in, b_i, b_f, head_gain, w_out):
    B, S, _ = h.shape
    proj = h @ w_in
    q, k, v, o, ig, fg = jnp.split(
        proj, [A_QK_W, 2 * A_QK_W, 2 * A_QK_W + A_V_W, 2 * A_QK_W + 2 * A_V_W,
               2 * A_QK_W + 2 * A_V_W + A_HEADS], axis=-1)
    q = q.reshape(B, S, A_HEADS, A_DQK).transpose(0, 2, 1, 3).astype(jnp.float32) * (A_DQK ** -0.5)
    k = k.reshape(B, S, A_HEADS, A_DQK).transpose(0, 2, 1, 3).astype(jnp.float32)
    v = v.reshape(B, S, A_HEADS, A_DV).transpose(0, 2, 1, 3).astype(jnp.float32)
    i_pre = (ig + b_i).astype(jnp.float32).transpose(0, 2, 1)
    logf = jax.nn.log_sigmoid((fg + b_f).astype(jnp.float32)).transpose(0, 2, 1)
    ht = mlstm_chunkwise(q, k, v, i_pre, logf)
    ht = ht * lax.rsqrt(jnp.mean(ht * ht, axis=-1, keepdims=True) + EPS)
    ht = ht * head_gain.astype(jnp.float32)[None, :, None, :]
    ht = ht.transpose(0, 2, 1, 3).reshape(B, S, A_V_W).astype(h.dtype)
    return (jax.nn.sigmoid(o) * ht) @ w_out


def fox_shared_kv(x, kv_gain, w_kv, fg_bias):
    B, S, _ = x.shape
    proj = rms_norm(x, kv_gain) @ w_kv
    k, v, fg = jnp.split(proj, [D_MODEL, 2 * D_MODEL], axis=-1)
    k = k.reshape(B, S, B_HEADS, B_DH).transpose(0, 2, 1, 3)
    v = v.reshape(B, S, B_HEADS, B_DH).transpose(0, 2, 1, 3)
    logf = jax.nn.log_sigmoid((fg + fg_bias).astype(jnp.float32)).transpose(0, 2, 1)
    F = jnp.cumsum(logf, axis=-1)
    return k, v, F


def fox_attention(q, k, v, F):
    S = q.shape[2]
    scale = B_DH ** -0.5
    outs = []
    for blk in range(S // Q_BLOCK):
        lo, hi = blk * Q_BLOCK, (blk + 1) * Q_BLOCK
        kb, vb = k[:, :, :hi], v[:, :, :hi]
        s = (jnp.einsum('bhtd,bhsd->bhts', q[:, :, lo:hi], kb).astype(jnp.float32) * scale
             + F[:, :, lo:hi, None] - F[:, :, None, :hi])
        mask = (lo + np.arange(Q_BLOCK))[:, None] >= np.arange(hi)[None, :]
        p = jax.nn.softmax(jnp.where(mask, s, -jnp.inf), axis=-1)
        outs.append(jnp.einsum('bhts,bhsd->bhtd', p.astype(vb.dtype), vb))
    return jnp.concatenate(outs, axis=2)


def fox_mixer(h, w_q, w_out, k, v, F):
    B, S, _ = h.shape
    q = (h @ w_q).reshape(B, S, B_HEADS, B_DH).transpose(0, 2, 1, 3)
    o = fox_attention(q, k, v, F)
    return o.transpose(0, 2, 1, 3).reshape(B, S, D_MODEL) @ w_out


def squared_relu_mlp(h, w1, w2):
    return jnp.square(jax.nn.relu(h @ w1)) @ w2


def setup_inputs(seed: int = 0) -> dict:
    key = jax.random.key(seed)
    ks = jax.random.split(key, 17)
    f32 = jnp.float32
    nrm = lambda k, shape: jax.random.normal(k, shape, f32)
    d = D_MODEL
    return {
        "x": nrm(ks[0], (BATCH, SEQ, d)),
        "c": nrm(ks[1], (BATCH, d)),
        "ada_w": nrm(ks[2], (DEPTH, d, 6 * d)) * (0.5 * d ** -0.5),
        "ada_b": nrm(ks[3], (DEPTH, 6 * d)) * 0.02,
        "a_w_in": nrm(ks[4], (N_A_LAYERS, d, A_PROJ)) * d ** -0.5,
        "a_b_i": nrm(ks[5], (N_A_LAYERS, A_HEADS)) * 0.1,
        "a_b_f": jnp.linspace(3.0, 6.0, A_HEADS, dtype=f32)[None, :] + 0.1 * nrm(ks[6], (N_A_LAYERS, A_HEADS)),
        "a_head_gain": 1.0 + 0.02 * nrm(ks[7], (N_A_LAYERS, A_HEADS, A_DV)),
        "a_w_out": nrm(ks[8], (N_A_LAYERS, A_V_W, d)) * A_V_W ** -0.5,
        "kv_gain": 1.0 + 0.02 * nrm(ks[9], (d,)),
        "b_w_kv": nrm(ks[10], (d, B_KV_PROJ)) * d ** -0.5,
        "b_fg_bias": jnp.linspace(1.0, 5.0, B_HEADS, dtype=f32) + 0.1 * nrm(ks[11], (B_HEADS,)),
        "b_w_q": nrm(ks[12], (N_B_LAYERS, d, d)) * d ** -0.5,
        "b_w_out": nrm(ks[13], (N_B_LAYERS, d, d)) * d ** -0.5,
        "mlp_w1": nrm(ks[14], (DEPTH, d, D_FF)) * d ** -0.5,
        "mlp_w2": nrm(ks[15], (DEPTH, D_FF, d)) * D_FF ** -0.5,
        "final_gain": 1.0 + 0.02 * nrm(ks[16], (d,)),
    }


def reference(x, c, ada_w, ada_b, a_w_in, a_b_i, a_b_f, a_head_gain, a_w_out,
              kv_gain, b_w_kv, b_fg_bias, b_w_q, b_w_out, mlp_w1, mlp_w2, final_gain):
    cond = jax.nn.silu(c)
    shared = None
    for l in range(DEPTH):
        mod = cond @ ada_w[l] + ada_b[l]
        sh1, sc1, g1, sh2, sc2, g2 = jnp.split(mod, 6, axis=-1)
        h = modulate(x, sh1, sc1)
        if l < N_A_LAYERS:
            y = mlstm_mixer(h, a_w_in[l], a_b_i[l], a_b_f[l], a_head_gain[l], a_w_out[l])
        else:
            if shared is None:
                shared = fox_shared_kv(x, kv_gain, b_w_kv, b_fg_bias)
            j = l - N_A_LAYERS
            y = fox_mixer(h, b_w_q[j], b_w_out[j], *shared)
        x = x + g1[:, None, :] * y
        h = modulate(x, sh2, sc2)
        x = x + g2[:, None, :] * squared_relu_mlp(h, mlp_w1[l], mlp_w2[l])
    return rms_norm(x, final_gain)
```

```python
import functools

import jax
import jax.numpy as jnp
from jax import lax
from jax.experimental import pallas as pl
from jax.experimental.pallas import tpu as pltpu

D_MODEL = 1024
DEPTH = 4
N_A_LAYERS = DEPTH // 2
A_HEADS = 4
A_DV = D_MODEL // A_HEADS
A_DQK = A_DV // 2
A_QK_W = A_HEADS * A_DQK
A_V_W = A_HEADS * A_DV
B_HEADS = 16
B_DH = D_MODEL // B_HEADS
D_FF = 4 * D_MODEL
EPS = 1e-6

LANES = 128
MLSTM_CHUNK = 256
PROJ_TILE = 512
KV_TILE = 256
Q_TILE = 256
MLP_TILE = 512
FF_CHUNK = 512
VMEM_LIMIT = 56 * 1024 * 1024

_BF16 = jnp.bfloat16
_F32 = jnp.float32


def _dot(a, b):
    return jnp.dot(a, b, preferred_element_type=_F32)


def _dot_nt(a, b):
    return lax.dot_general(a, b, (((1,), (1,)), ((), ())), preferred_element_type=_F32)


def _dot_tn(a, b):
    return lax.dot_general(a, b, (((0,), (0,)), ((), ())), preferred_element_type=_F32)


def _rms(x):
    return x * lax.rsqrt(jnp.mean(x * x, axis=-1, keepdims=True) + EPS)


def _modulate(x, shift, scale):
    return _rms(x) * (1.0 + scale) + shift


def _log_sigmoid(z):
    return jnp.minimum(z, 0.0) - jnp.log1p(jnp.exp(-jnp.abs(z)))


def _segment_cumsum(x, axis, seg):
    pos = lax.broadcasted_iota(jnp.int32, x.shape, axis) & (seg - 1)
    k = 1
    while k < seg:
        x = x + jnp.where(pos >= k, pltpu.roll(x, k, axis), 0.0)
        k *= 2
    return x


def _params(n_grid):
    return pltpu.CompilerParams(dimension_semantics=("arbitrary",) * n_grid,
                                vmem_limit_bytes=VMEM_LIMIT)


def _const_spec(shape):
    return pl.BlockSpec(shape, lambda *_: (0,) * len(shape), pipeline_mode=pl.Buffered(1))


def _mod_spec(layer, slot):
    return pl.BlockSpec((None, None, None, 1, D_MODEL), lambda b, i: (layer, slot, b, 0, 0))


def _adaln_kernel(c_ref, w_ref, b_ref, o_ref):
    c = c_ref[...]
    cond = (c * jax.nn.sigmoid(c)).astype(_BF16)
    o_ref[...] = _dot(cond, w_ref[...].astype(_BF16)) + b_ref[...]


def _adaln_table(c, ada_w, ada_b):
    batch = c.shape[0]
    return pl.pallas_call(
        _adaln_kernel,
        grid=(DEPTH, 6),
        in_specs=[
            pl.BlockSpec((batch, D_MODEL), lambda l, j: (0, 0)),
            pl.BlockSpec((None, D_MODEL, D_MODEL), lambda l, j: (l, 0, j)),
            pl.BlockSpec((None, None, 1, D_MODEL), lambda l, j: (l, j, 0, 0)),
        ],
        out_specs=pl.BlockSpec((None, None, batch, D_MODEL), lambda l, j: (l, j, 0, 0)),
        out_shape=jax.ShapeDtypeStruct((DEPTH, 6, batch, D_MODEL), _F32),
        compiler_params=_params(2),
        name="adaln_table",
    )(c, ada_w, ada_b.reshape(DEPTH, 6, 1, D_MODEL))


def _mlstm_proj_kernel(x_ref, sh_ref, sc_ref, wq_ref, wk_ref, wv_ref, wo_ref, wg_ref, wgt_ref,
                       bcol_ref, brow_ref, q_ref, k_ref, v_ref, o_ref, gcol_ref, grow_ref):
    h = _modulate(x_ref[...], sh_ref[...], sc_ref[...]).astype(_BF16)
    q_ref[...] = (_dot(h, wq_ref[...]) * (A_DQK ** -0.5)).astype(_BF16)
    k_ref[...] = _dot(h, wk_ref[...]).astype(_BF16)
    v_ref[...] = _dot(h, wv_ref[...]).astype(_BF16)
    o_ref[...] = _dot(h, wo_ref[...])
    zc = _dot(h, wg_ref[...]) + bcol_ref[...]
    bc = _segment_cumsum(_log_sigmoid(zc), 0, MLSTM_CHUNK)
    lane = lax.broadcasted_iota(jnp.int32, zc.shape, 1)
    gcol_ref[...] = jnp.where(lane < A_HEADS, zc, bc)[:, :2 * A_HEADS]
    zr = _dot_nt(wgt_ref[...], h) + brow_ref[...]
    br = _segment_cumsum(_log_sigmoid(zr), 1, MLSTM_CHUNK)
    sub = lax.broadcasted_iota(jnp.int32, zr.shape, 0)
    grow_ref[...] = jnp.where(sub < A_HEADS, zr, br)


def _mlstm_proj(x, mod, layer, w_in, b_i, b_f):
    batch, seq, _ = x.shape
    tm = PROJ_TILE
    w = w_in.astype(_BF16)
    wq, wk = w[:, :A_QK_W], w[:, A_QK_W:2 * A_QK_W]
    wv = w[:, 2 * A_QK_W:2 * A_QK_W + A_V_W]
    wo = w[:, 2 * A_QK_W + A_V_W:2 * A_QK_W + 2 * A_V_W]
    wgate = w[:, 2 * A_QK_W + 2 * A_V_W:]
    wg = jnp.pad(wgate, ((0, 0), (0, LANES - 2 * A_HEADS)))
    wgt = wgate.T
    bias = jnp.concatenate([b_i, b_f]).astype(_F32)
    bcol = jnp.pad(bias, (0, LANES - 2 * A_HEADS)).reshape(1, LANES)
    brow = bias.reshape(2 * A_HEADS, 1)
    tok = lambda width: pl.BlockSpec((None, tm, width), lambda b, i: (b, i, 0))
    return pl.pallas_call(
        _mlstm_proj_kernel,
        grid=(batch, seq // tm),
        in_specs=[
            tok(D_MODEL), _mod_spec(layer, 0), _mod_spec(layer, 1),
            _const_spec((D_MODEL, A_QK_W)), _const_spec((D_MODEL, A_QK_W)),
            _const_spec((D_MODEL, A_V_W)), _const_spec((D_MODEL, A_V_W)),
            _const_spec((D_MODEL, LANES)), _const_spec((2 * A_HEADS, D_MODEL)),
            _const_spec((1, LANES)), _const_spec((2 * A_HEADS, 1)),
        ],
        out_specs=[
            tok(A_QK_W), tok(A_QK_W), tok(A_V_W), tok(A_V_W), tok(2 * A_HEADS),
            pl.BlockSpec((None, 2 * A_HEADS, tm), lambda b, i: (b, 0, i)),
        ],
        out_shape=[
            jax.ShapeDtypeStruct((batch, seq, A_QK_W), _BF16),
            jax.ShapeDtypeStruct((batch, seq, A_QK_W), _BF16),
            jax.ShapeDtypeStruct((batch, seq, A_V_W), _BF16),
            jax.ShapeDtypeStruct((batch, seq, A_V_W), _F32),
            jax.ShapeDtypeStruct((batch, seq, 2 * A_HEADS), _F32),
            jax.ShapeDtypeStruct((batch, 2 * A_HEADS, seq), _F32),
        ],
        compiler_params=_params(2),
        name="mlstm_proj",
    )(x, mod, mod, wq, wk, wv, wo, wg, wgt, bcol, brow)


def _mlstm_core_kernel(x_ref, g1_ref, q_ref, k_ref, v_ref, o_ref, gcol_ref, grow_ref,
                       gain_ref, wout_ref, out_ref, c_scr, n_scr, m_scr, z_scr):
    L = MLSTM_CHUNK

    @pl.when(pl.program_id(1) == 0)
    def _():
        c_scr[...] = jnp.zeros_like(c_scr)
        n_scr[...] = jnp.zeros_like(n_scr)
        m_scr[...] = jnp.zeros_like(m_scr)

    gcol = gcol_ref[...]
    grow = grow_ref[...]
    row = lax.broadcasted_iota(jnp.int32, (L, L), 0)
    col = lax.broadcasted_iota(jnp.int32, (L, L), 1)
    causal = row >= col
    for h in range(A_HEADS):
        qh = q_ref[:, h * A_DQK:(h + 1) * A_DQK]
        kh = k_ref[:, h * A_DQK:(h + 1) * A_DQK]
        vh = v_ref[:, h * A_DV:(h + 1) * A_DV]
        ic = gcol[:, h:h + 1]
        bc = gcol[:, A_HEADS + h:A_HEADS + h + 1]
        ir = grow[h:h + 1, :]
        br = grow[A_HEADS + h:A_HEADS + h + 1, :]
        m_prev = m_scr[h][:, 0:1]
        c_prev = c_scr[h]
        n_prev = n_scr[h]

        d = jnp.where(causal, bc - br + ir, -jnp.inf)
        inter = bc + m_prev
        m_t = jnp.maximum(inter, jnp.max(d, axis=-1, keepdims=True))
        w = jnp.exp(d - m_t)
        w_inter = jnp.exp(inter - m_t)
        a = w * _dot_nt(qh, kh)
        num = w_inter * _dot(qh, c_prev.astype(_BF16)) + _dot(a.astype(_BF16), vh)
        den = (w_inter * jnp.sum(qh.astype(_F32) * n_prev, axis=-1, keepdims=True)
               + jnp.sum(a, axis=-1, keepdims=True))
        ht = num / jnp.maximum(jnp.abs(den), jnp.exp(-m_t))

        b_last = br[:, L - 1:L]
        m_new = jnp.maximum(b_last + m_prev,
                            jnp.max(b_last - br + ir, axis=-1, keepdims=True))
        wl = jnp.exp(b_last - bc + ic - m_new)
        decay = jnp.exp(b_last + m_prev - m_new)
        kw = wl * kh.astype(_F32)
        c_scr[h] = decay * c_prev + _dot_tn(kw.astype(_BF16), vh)
        n_scr[h] = decay * n_prev + jnp.sum(kw, axis=0, keepdims=True)
        m_scr[h] = jnp.broadcast_to(m_new, (1, LANES))

        ht = _rms(ht) * gain_ref[:, h * A_DV:(h + 1) * A_DV]
        og = jax.nn.sigmoid(o_ref[:, h * A_DV:(h + 1) * A_DV])
        z_scr[:, h * A_DV:(h + 1) * A_DV] = (og * ht).astype(_BF16)

    out_ref[...] = x_ref[...] + g1_ref[...] * _dot(z_scr[...], wout_ref[...])


def _mlstm_core(x, mod, layer, q, k, v, o, gcol, grow, head_gain, w_out):
    batch, seq, _ = x.shape
    L = MLSTM_CHUNK
    tok = lambda width: pl.BlockSpec((None, L, width), lambda b, i: (b, i, 0))
    return pl.pallas_call(
        _mlstm_core_kernel,
        grid=(batch, seq // L),
        in_specs=[
            tok(D_MODEL), _mod_spec(layer, 2),
            tok(A_QK_W), tok(A_QK_W), tok(A_V_W), tok(A_V_W), tok(2 * A_HEADS),
            pl.BlockSpec((None, 2 * A_HEADS, L), lambda b, i: (b, 0, i)),
            _const_spec((1, A_V_W)), _const_spec((A_V_W, D_MODEL)),
        ],
        out_specs=tok(D_MODEL),
        out_shape=jax.ShapeDtypeStruct(x.shape, _F32),
        scratch_shapes=[
            pltpu.VMEM((A_HEADS, A_DQK, A_DV), _F32),
            pltpu.VMEM((A_HEADS, 1, A_DQK), _F32),
            pltpu.VMEM((A_HEADS, 1, LANES), _F32),
            pltpu.VMEM((L, A_V_W), _BF16),
        ],
        compiler_params=_params(2),
        name="mlstm_core",
    )(x, mod, q, k, v, o, gcol, grow, head_gain.reshape(1, A_V_W).astype(_F32),
      w_out.astype(_BF16))


def _mlp_kernel(x_ref, sh_ref, sc_ref, g_ref, w1_ref, w2_ref, fgain_ref, out_ref, u_scr, *,
                final_norm):
    x = x_ref[...]
    h = _modulate(x, sh_ref[...], sc_ref[...]).astype(_BF16)
    for c in range(D_FF // FF_CHUNK):
        u = jnp.maximum(_dot(h, w1_ref[:, c * FF_CHUNK:(c + 1) * FF_CHUNK]), 0.0)
        u_scr[:, c * FF_CHUNK:(c + 1) * FF_CHUNK] = (u * u).astype(_BF16)
    y = x + g_ref[...] * _dot(u_scr[...], w2_ref[...])
    if final_norm:
        y = _rms(y) * fgain_ref[...]
    out_ref[...] = y


def _mlp(x, mod, layer, w1, w2, final_gain, final_norm):
    batch, seq, _ = x.shape
    tm = MLP_TILE
    tok = pl.BlockSpec((None, tm, D_MODEL), lambda b, i: (b, i, 0))
    return pl.pallas_call(
        functools.partial(_mlp_kernel, final_norm=final_norm),
        grid=(batch, seq // tm),
        in_specs=[
            tok, _mod_spec(layer, 3), _mod_spec(layer, 4), _mod_spec(layer, 5),
            _const_spec((D_MODEL, D_FF)), _const_spec((D_FF, D_MODEL)),
            _const_spec((1, D_MODEL)),
        ],
        out_specs=tok,
        out_shape=jax.ShapeDtypeStruct(x.shape, _F32),
        scratch_shapes=[pltpu.VMEM((tm, D_FF), _BF16)],
        compiler_params=_params(2),
        name="mlp",
    )(x, mod, mod, mod, w1.astype(_BF16), w2.astype(_BF16),
      final_gain.reshape(1, D_MODEL).astype(_F32))


def _fox_kv_kernel(x_ref, gain_ref, wk_ref, wv_ref, wf_ref, wft_ref, bcol_ref, brow_ref,
                   k_ref, v_ref, fcol_ref, frow_ref, ccol_scr, crow_scr):
    tm = KV_TILE

    @pl.when(pl.program_id(1) == 0)
    def _():
        ccol_scr[...] = jnp.zeros_like(ccol_scr)
        crow_scr[...] = jnp.zeros_like(crow_scr)

    h = (_rms(x_ref[...]) * gain_ref[...]).astype(_BF16)
    k_ref[...] = _dot(h, wk_ref[...]).astype(_BF16)
    v_ref[...] = _dot(h, wv_ref[...]).astype(_BF16)
    fc = _segment_cumsum(_log_sigmoid(_dot(h, wf_ref[...]) + bcol_ref[...]), 0, tm)
    fc = fc + ccol_scr[...]
    ccol_scr[...] = fc[tm - 1:tm, :]
    fcol_ref[...] = fc[:, :B_HEADS]
    fr = _segment_cumsum(_log_sigmoid(_dot_nt(wft_ref[...], h) + brow_ref[...]), 1, tm)
    fr = fr + crow_scr[:, 0:1]
    crow_scr[...] = jnp.broadcast_to(fr[:, tm - 1:tm], crow_scr.shape)
    frow_ref[...] = fr


def _fox_kv(x, kv_gain, w_kv, fg_bias):
    batch, seq, _ = x.shape
    tm = KV_TILE
    w = w_kv.astype(_BF16)
    wk, wv, wfg = w[:, :D_MODEL], w[:, D_MODEL:2 * D_MODEL], w[:, 2 * D_MODEL:]
    wf = jnp.pad(wfg, ((0, 0), (0, LANES - B_HEADS)))
    wft = wfg.T
    bias = fg_bias.astype(_F32)
    bcol = jnp.pad(bias, (0, LANES - B_HEADS)).reshape(1, LANES)
    brow = bias.reshape(B_HEADS, 1)
    tok = lambda width: pl.BlockSpec((None, tm, width), lambda b, i: (b, i, 0))
    return pl.pallas_call(
        _fox_kv_kernel,
        grid=(batch, seq // tm),
        in_specs=[
            tok(D_MODEL), _const_spec((1, D_MODEL)),
            _const_spec((D_MODEL, D_MODEL)), _const_spec((D_MODEL, D_MODEL)),
            _const_spec((D_MODEL, LANES)), _const_spec((B_HEADS, D_MODEL)),
            _const_spec((1, LANES)), _const_spec((B_HEADS, 1)),
        ],
        out_specs=[
            tok(D_MODEL), tok(D_MODEL), tok(B_HEADS),
            pl.BlockSpec((None, None, B_HEADS, tm), lambda b, i: (b, i, 0, 0)),
        ],
        out_shape=[
            jax.ShapeDtypeStruct((batch, seq, D_MODEL), _BF16),
            jax.ShapeDtypeStruct((batch, seq, D_MODEL), _BF16),
            jax.ShapeDtypeStruct((batch, seq, B_HEADS), _F32),
            jax.ShapeDtypeStruct((batch, seq // tm, B_HEADS, tm), _F32),
        ],
        scratch_shapes=[pltpu.VMEM((1, LANES), _F32), pltpu.VMEM((B_HEADS, LANES), _F32)],
        compiler_params=_params(2),
        name="fox_kv",
    )(x, kv_gain.reshape(1, D_MODEL).astype(_F32), wk, wv, wf, wft, bcol, brow)


def _fox_attn_kernel(x_ref, sh_ref, sc_ref, g1_ref, wq_ref, k_ref, v_ref, fcol_ref, frow_ref,
                     wout_ref, out_ref, q_scr, o_scr):
    tq, tk = Q_TILE, KV_TILE
    qi = pl.program_id(1)
    x = x_ref[...]
    h = _modulate(x, sh_ref[...], sc_ref[...]).astype(_BF16)
    q_scr[...] = (_dot(h, wq_ref[...]) * (B_DH ** -0.5)).astype(_BF16)
    fcol = fcol_ref[...]
    lane = lax.broadcasted_iota(jnp.int32, (tq, LANES), 1)
    low = lane < B_DH
    row = lax.broadcasted_iota(jnp.int32, (tq, tk), 0)
    col = lax.broadcasted_iota(jnp.int32, (tq, tk), 1)
    zero = jnp.zeros((), _BF16)

    for p in range(B_HEADS // 2):
        sl = slice(p * LANES, (p + 1) * LANES)
        qp = q_scr[:, sl]
        q_heads = (jnp.where(low, qp, zero), jnp.where(low, zero, qp))
        f_heads = (fcol[:, 2 * p:2 * p + 1], fcol[:, 2 * p + 1:2 * p + 2])

        def body(j, carry, sl=sl, p=p, q_heads=q_heads, f_heads=f_heads):
            kb = k_ref[j, :, sl]
            vb = v_ref[j, :, sl]
            fr = frow_ref[j]
            keep = (qi * tq + row) >= (j * tk + col)
            new = []
            for e in range(2):
                m_old, l_old, acc_old = carry[e]
                s = _dot_nt(q_heads[e], kb) + f_heads[e] - fr[2 * p + e:2 * p + e + 1, :]
                s = jnp.where(keep, s, -jnp.inf)
                m_new = jnp.maximum(m_old, jnp.max(s, axis=-1, keepdims=True))
                alpha = jnp.exp(m_old - m_new)
                pr = jnp.exp(s - m_new)
                l_new = alpha * l_old + jnp.sum(pr, axis=-1, keepdims=True)
                acc_new = alpha * acc_old + _dot(pr.astype(_BF16), vb)
                new.append((m_new, l_new, acc_new))
            return tuple(new)

        init = tuple((jnp.full((tq, 1), -jnp.inf, _F32), jnp.zeros((tq, 1), _F32),
                      jnp.zeros((tq, LANES), _F32)) for _ in range(2))
        (_, l_a, acc_a), (_, l_b, acc_b) = lax.fori_loop(0, qi + 1, body, init)
        o_scr[:, sl] = jnp.where(low, acc_a / l_a, acc_b / l_b).astype(_BF16)

    out_ref[...] = x + g1_ref[...] * _dot(o_scr[...], wout_ref[...])


def _fox_attn(x, mod, layer, w_q, w_out, k, v, fcol, frow):
    batch, seq, _ = x.shape
    tq, tk = Q_TILE, KV_TILE
    nkv = seq // tk
    tok = lambda width: pl.BlockSpec((None, tq, width), lambda b, i: (b, i, 0))
    kv_spec = pl.BlockSpec((None, nkv, tk, D_MODEL), lambda b, i: (b, 0, 0, 0))
    return pl.pallas_call(
        _fox_attn_kernel,
        grid=(batch, seq // tq),
        in_specs=[
            tok(D_MODEL), _mod_spec(layer, 0), _mod_spec(layer, 1), _mod_spec(layer, 2),
            _const_spec((D_MODEL, D_MODEL)), kv_spec, kv_spec, tok(B_HEADS),
            pl.BlockSpec((None, nkv, B_HEADS, tk), lambda b, i: (b, 0, 0, 0)),
            _const_spec((D_MODEL, D_MODEL)),
        ],
        out_specs=tok(D_MODEL),
        out_shape=jax.ShapeDtypeStruct(x.shape, _F32),
        scratch_shapes=[pltpu.VMEM((tq, D_MODEL), _BF16), pltpu.VMEM((tq, D_MODEL), _BF16)],
        compiler_params=_params(2),
        name="fox_attn",
    )(x, mod, mod, mod, w_q.astype(_BF16),
      k.reshape(batch, nkv, tk, D_MODEL), v.reshape(batch, nkv, tk, D_MODEL),
      fcol, frow, w_out.astype(_BF16))


def kernel(x, c, ada_w, ada_b, a_w_in, a_b_i, a_b_f, a_head_gain, a_w_out, kv_gain, b_w_kv,
           b_fg_bias, b_w_q, b_w_out, mlp_w1, mlp_w2, final_gain):
    batch, seq, d = x.shape
    assert d == D_MODEL and seq % max(PROJ_TILE, MLP_TILE, MLSTM_CHUNK, Q_TILE, KV_TILE) == 0
    assert PROJ_TILE % MLSTM_CHUNK == 0 and Q_TILE == KV_TILE
    mod = _adaln_table(c, ada_w, ada_b).reshape(DEPTH, 6, batch, 1, D_MODEL)
    shared = None
    for l in range(DEPTH):
        if l < N_A_LAYERS:
            q, k, v, o, gcol, grow = _mlstm_proj(x, mod, l, a_w_in[l], a_b_i[l], a_b_f[l])
            x = _mlstm_core(x, mod, l, q, k, v, o, gcol, grow, a_head_gain[l], a_w_out[l])
        else:
            if shared is None:
                shared = _fox_kv(x, kv_gain, b_w_kv, b_fg_bias)
            j = l - N_A_LAYERS
            x = _fox_attn(x, mod, l, b_w_q[j], b_w_out[j], *shared)
        x = _mlp(x, mod, l, mlp_w1[l], mlp_w2[l], final_gain, final_norm=(l == DEPTH - 1))
    return x
```

```python
import functools

import jax
import jax.numpy as jnp
from jax import lax
from jax.experimental import pallas as pl
from jax.experimental.pallas import tpu as pltpu

D_MODEL = 1024
DEPTH = 4
N_A_LAYERS = DEPTH // 2
A_HEADS = 4
A_DV = D_MODEL // A_HEADS
A_DQK = A_DV // 2
A_QK_W = A_HEADS * A_DQK
A_V_W = A_HEADS * A_DV
B_HEADS = 16
B_DH = D_MODEL // B_HEADS
D_FF = 4 * D_MODEL
EPS = 1e-6

LANES = 128
MLSTM_CHUNK = 256
PROJ_TILE = 512
KV_TILE = 256
Q_TILE = 256
MLP_TILE = 512
FF_CHUNK = 512
VMEM_LIMIT = 56 * 1024 * 1024

_BF16 = jnp.bfloat16
_F32 = jnp.float32


def _dot(a, b):
    return jnp.dot(a, b, preferred_element_type=_F32)


def _dot_nt(a, b):
    return lax.dot_general(a, b, (((1,), (1,)), ((), ())), preferred_element_type=_F32)


def _dot_tn(a, b):
    return lax.dot_general(a, b, (((0,), (0,)), ((), ())), preferred_element_type=_F32)


def _rms(x):
    return x * lax.rsqrt(jnp.mean(x * x, axis=-1, keepdims=True) + EPS)


def _modulate(x, shift, scale):
    return _rms(x) * (1.0 + scale) + shift


def _log_sigmoid(z):
    return jnp.minimum(z, 0.0) - jnp.log1p(jnp.exp(-jnp.abs(z)))


def _segment_cumsum(x, axis, seg):
    pos = lax.broadcasted_iota(jnp.int32, x.shape, axis) & (seg - 1)
    k = 1
    while k < seg:
        x = x + jnp.where(pos >= k, pltpu.roll(x, k, axis), 0.0)
        k *= 2
    return x


def _params(n_grid):
    return pltpu.CompilerParams(dimension_semantics=("arbitrary",) * n_grid,
                                vmem_limit_bytes=VMEM_LIMIT)


def _const_spec(shape):
    return pl.BlockSpec(shape, lambda *_: (0,) * len(shape), pipeline_mode=pl.Buffered(1))


def _mod_spec(layer, slot):
    return pl.BlockSpec((None, None, None, 1, D_MODEL), lambda b, i: (layer, slot, b, 0, 0))


def _adaln_kernel(c_ref, w_ref, b_ref, o_ref):
    c = c_ref[...]
    cond = (c * jax.nn.sigmoid(c)).astype(_BF16)
    o_ref[...] = _dot(cond, w_ref[...].astype(_BF16)) + b_ref[...]


def _adaln_table(c, ada_w, ada_b):
    batch = c.shape[0]
    return pl.pallas_call(
        _adaln_kernel,
        grid=(DEPTH, 6),
        in_specs=[
            pl.BlockSpec((batch, D_MODEL), lambda l, j: (0, 0)),
            pl.BlockSpec((None, D_MODEL, D_MODEL), lambda l, j: (l, 0, j)),
            pl.BlockSpec((None, None, 1, D_MODEL), lambda l, j: (l, j, 0, 0)),
        ],
        out_specs=pl.BlockSpec((None, None, batch, D_MODEL), lambda l, j: (l, j, 0, 0)),
        out_shape=jax.ShapeDtypeStruct((DEPTH, 6, batch, D_MODEL), _F32),
        compiler_params=_params(2),
        name="adaln_table",
    )(c, ada_w, ada_b.reshape(DEPTH, 6, 1, D_MODEL))


def _mlstm_proj_kernel(x_ref, sh_ref, sc_ref, wq_ref, wk_ref, wv_ref, wo_ref, wg_ref, wgt_ref,
                       bcol_ref, brow_ref, q_ref, k_ref, v_ref, o_ref, gcol_ref, grow_ref):
    h = _modulate(x_ref[...], sh_ref[...], sc_ref[...]).astype(_BF16)
    q_ref[...] = (_dot(h, wq_ref[...]) * (A_DQK ** -0.5)).astype(_BF16)
    k_ref[...] = _dot(h, wk_ref[...]).astype(_BF16)
    v_ref[...] = _dot(h, wv_ref[...]).astype(_BF16)
    o_ref[...] = _dot(h, wo_ref[...])
    zc = _dot(h, wg_ref[...]) + bcol_ref[...]
    bc = _segment_cumsum(_log_sigmoid(zc), 0, MLSTM_CHUNK)
    lane = lax.broadcasted_iota(jnp.int32, zc.shape, 1)
    gcol_ref[...] = jnp.where(lane < A_HEADS, zc, bc)[:, :2 * A_HEADS]
    zr = _dot_nt(wgt_ref[...], h) + brow_ref[...]
    br = _segment_cumsum(_log_sigmoid(zr), 1, MLSTM_CHUNK)
    sub = lax.broadcasted_iota(jnp.int32, zr.shape, 0)
    grow_ref[...] = jnp.where(sub < A_HEADS, zr, br)


def _mlstm_proj(x, mod, layer, w_in, b_i, b_f):
    batch, seq, _ = x.shape
    tm = PROJ_TILE
    w = w_in.astype(_BF16)
    wq, wk = w[:, :A_QK_W], w[:, A_QK_W:2 * A_QK_W]
    wv = w[:, 2 * A_QK_W:2 * A_QK_W + A_V_W]
    wo = w[:, 2 * A_QK_W + A_V_W:2 * A_QK_W + 2 * A_V_W]
    wgate = w[:, 2 * A_QK_W + 2 * A_V_W:]
    wg = jnp.pad(wgate, ((0, 0), (0, LANES - 2 * A_HEADS)))
    wgt = wgate.T
    bias = jnp.concatenate([b_i, b_f]).astype(_F32)
    bcol = jnp.pad(bias, (0, LANES - 2 * A_HEADS)).reshape(1, LANES)
    brow = bias.reshape(2 * A_HEADS, 1)
    tok = lambda width: pl.BlockSpec((None, tm, width), lambda b, i: (b, i, 0))
    return pl.pallas_call(
        _mlstm_proj_kernel,
        grid=(batch, seq // tm),
        in_specs=[
            tok(D_MODEL), _mod_spec(layer, 0), _mod_spec(layer, 1),
            _const_spec((D_MODEL, A_QK_W)), _const_spec((D_MODEL, A_QK_W)),
            _const_spec((D_MODEL, A_V_W)), _const_spec((D_MODEL, A_V_W)),
            _const_spec((D_MODEL, LANES)), _const_spec((2 * A_HEADS, D_MODEL)),
            _const_spec((1, LANES)), _const_spec((2 * A_HEADS, 1)),
        ],
        out_specs=[
            tok(A_QK_W), tok(A_QK_W), tok(A_V_W), tok(A_V_W), tok(2 * A_HEADS),
            pl.BlockSpec((None, 2 * A_HEADS, tm), lambda b, i: (b, 0, i)),
        ],
        out_shape=[
            jax.ShapeDtypeStruct((batch, seq, A_QK_W), _BF16),
            jax.ShapeDtypeStruct((batch, seq, A_QK_W), _BF16),
            jax.ShapeDtypeStruct((batch, seq, A_V_W), _BF16),
            jax.ShapeDtypeStruct((batch, seq, A_V_W), _F32),
            jax.ShapeDtypeStruct((batch, seq, 2 * A_HEADS), _F32),
            jax.ShapeDtypeStruct((batch, 2 * A_HEADS, seq), _F32),
        ],
        compiler_params=_params(2),
        name="mlstm_proj",
    )(x, mod, mod, wq, wk, wv, wo, wg, wgt, bcol, brow)


def _mlstm_core_kernel(x_ref, g1_ref, q_ref, k_ref, v_ref, o_ref, gcol_ref, grow_ref,
                       gain_ref, wout_ref, out_ref, c_scr, n_scr, m_scr, z_scr):
    L = MLSTM_CHUNK

    @pl.when(pl.program_id(1) == 0)
    def _():
        c_scr[...] = jnp.zeros_like(c_scr)
        n_scr[...] = jnp.zeros_like(n_scr)
        m_scr[...] = jnp.zeros_like(m_scr)

    gcol = gcol_ref[...]
    grow = grow_ref[...]
    row = lax.broadcasted_iota(jnp.int32, (L, L), 0)
    col = lax.broadcasted_iota(jnp.int32, (L, L), 1)
    causal = row >= col
    for h in range(A_HEADS):
        qh = q_ref[:, h * A_DQK:(h + 1) * A_DQK]
        kh = k_ref[:, h * A_DQK:(h + 1) * A_DQK]
        vh = v_ref[:, h * A_DV:(h + 1) * A_DV]
        ic = gcol[:, h:h + 1]
        bc = gcol[:, A_HEADS + h:A_HEADS + h + 1]
        ir = grow[h:h + 1, :]
        br = grow[A_HEADS + h:A_HEADS + h + 1, :]
        m_prev = m_scr[h][:, 0:1]
        c_prev = c_scr[h]
        n_prev = n_scr[h]

        d = jnp.where(causal, bc - br + ir, -jnp.inf)
        inter = bc + m_prev
        m_t = jnp.maximum(inter, jnp.max(d, axis=-1, keepdims=True))
        w = jnp.exp(d - m_t)
        w_inter = jnp.exp(inter - m_t)
        a = w * _dot_nt(qh, kh)
        num = w_inter * _dot(qh, c_prev.astype(_BF16)) + _dot(a.astype(_BF16), vh)
        den = (w_inter * jnp.sum(qh.astype(_F32) * n_prev, axis=-1, keepdims=True)
               + jnp.sum(a, axis=-1, keepdims=True))
        ht = num / jnp.maximum(jnp.abs(den), jnp.exp(-m_t))

        b_last = br[:, L - 1:L]
        m_new = jnp.maximum(b_last + m_prev,
                            jnp.max(b_last - br + ir, axis=-1, keepdims=True))
        wl = jnp.exp(b_last - bc + ic - m_new)
        decay = jnp.exp(b_last + m_prev - m_new)
        kw = wl * kh.astype(_F32)
        c_scr[h] = decay * c_prev + _dot_tn(kw.astype(_BF16), vh)
        n_scr[h] = decay * n_prev + jnp.sum(kw, axis=0, keepdims=True)
        m_scr[h] = jnp.broadcast_to(m_new, (1, LANES))

        ht = _rms(ht) * gain_ref[:, h * A_DV:(h + 1) * A_DV]
        og = jax.nn.sigmoid(o_ref[:, h * A_DV:(h + 1) * A_DV])
        z_scr[:, h * A_DV:(h + 1) * A_DV] = (og * ht).astype(_BF16)

    out_ref[...] = x_ref[...] + g1_ref[...] * _dot(z_scr[...], wout_ref[...])


def _mlstm_core(x, mod, layer, q, k, v, o, gcol, grow, head_gain, w_out):
    batch, seq, _ = x.shape
    L = MLSTM_CHUNK
    tok = lambda width: pl.BlockSpec((None, L, width), lambda b, i: (b, i, 0))
    return pl.pallas_call(
        _mlstm_core_kernel,
        grid=(batch, seq // L),
        in_specs=[
            tok(D_MODEL), _mod_spec(layer, 2),
            tok(A_QK_W), tok(A_QK_W), tok(A_V_W), tok(A_V_W), tok(2 * A_HEADS),
            pl.BlockSpec((None, 2 * A_HEADS, L), lambda b, i: (b, 0, i)),
            _const_spec((1, A_V_W)), _const_spec((A_V_W, D_MODEL)),
        ],
        out_specs=tok(D_MODEL),
        out_shape=jax.ShapeDtypeStruct(x.shape, _F32),
        scratch_shapes=[
            pltpu.VMEM((A_HEADS, A_DQK, A_DV), _F32),
            pltpu.VMEM((A_HEADS, 1, A_DQK), _F32),
            pltpu.VMEM((A_HEADS, 1, LANES), _F32),
            pltpu.VMEM((L, A_V_W), _BF16),
        ],
        compiler_params=_params(2),
        name="mlstm_core",
    )(x, mod, q, k, v, o, gcol, grow, head_gain.reshape(1, A_V_W).astype(_F32),
      w_out.astype(_BF16))


def _mlp_kernel(x_ref, sh_ref, sc_ref, g_ref, w1_ref, w2_ref, fgain_ref, out_ref, u_scr, *,
                final_norm):
    x = x_ref[...]
    h = _modulate(x, sh_ref[...], sc_ref[...]).astype(_BF16)
    for c in range(D_FF // FF_CHUNK):
        u = jnp.maximum(_dot(h, w1_ref[:, c * FF_CHUNK:(c + 1) * FF_CHUNK]), 0.0)
        u_scr[:, c * FF_CHUNK:(c + 1) * FF_CHUNK] = (u * u).astype(_BF16)
    y = x + g_ref[...] * _dot(u_scr[...], w2_ref[...])
    if final_norm:
        y = _rms(y) * fgain_ref[...]
    out_ref[...] = y


def _mlp(x, mod, layer, w1, w2, final_gain, final_norm):
    batch, seq, _ = x.shape
    tm = MLP_TILE
    tok = pl.BlockSpec((None, tm, D_MODEL), lambda b, i: (b, i, 0))
    return pl.pallas_call(
        functools.partial(_mlp_kernel, final_norm=final_norm),
        grid=(batch, seq // tm),
        in_specs=[
            tok, _mod_spec(layer, 3), _mod_spec(layer, 4), _mod_spec(layer, 5),
            _const_spec((D_MODEL, D_FF)), _const_spec((D_FF, D_MODEL)),
            _const_spec((1, D_MODEL)),
        ],
        out_specs=tok,
        out_shape=jax.ShapeDtypeStruct(x.shape, _F32),
        scratch_shapes=[pltpu.VMEM((tm, D_FF), _BF16)],
        compiler_params=_params(2),
        name="mlp",
    )(x, mod, mod, mod, w1.astype(_BF16), w2.astype(_BF16),
      final_gain.reshape(1, D_MODEL).astype(_F32))


def _split_bf16(x):
    hi = x.astype(_BF16).astype(_F32)
    rest = x - hi
    mid = rest.astype(_BF16).astype(_F32)
    lo = (rest - mid).astype(_BF16).astype(_F32)
    return hi, mid, lo


LOG2E = 1.4426950408889634
N_BIAS = 3


def _fox_kv_kernel(x_ref, gain_ref, wk_ref, wvt_ref, wf_ref, wft_ref, bcol_ref, brow_ref,
                   kaug_ref, vt_ref, frow_ref, ccol_scr, crow_scr):
    tm = KV_TILE

    @pl.when(pl.program_id(1) == 0)
    def _():
        ccol_scr[...] = jnp.zeros_like(ccol_scr)
        crow_scr[...] = jnp.zeros_like(crow_scr)

    h = (_rms(x_ref[...]) * gain_ref[...]).astype(_BF16)
    vt_ref[...] = _dot_nt(wvt_ref[...], h).astype(_BF16)
    k = _dot(h, wk_ref[...])
    fc = _segment_cumsum(_log_sigmoid(_dot(h, wf_ref[...]) + bcol_ref[...]), 0, tm)
    fc = fc + ccol_scr[...]
    ccol_scr[...] = fc[tm - 1:tm, :]
    fr = _segment_cumsum(_log_sigmoid(_dot_nt(wft_ref[...], h) + brow_ref[...]), 1, tm)
    fr = fr + crow_scr[:, 0:1]
    crow_scr[...] = jnp.broadcast_to(fr[:, tm - 1:tm], crow_scr.shape)
    frow_ref[...] = fr * LOG2E

    lane = lax.broadcasted_iota(jnp.int32, (tm, LANES), 1)
    for hd in range(B_HEADS):
        pair, odd = divmod(hd, 2)
        own = (lane >= B_DH) if odd else (lane < B_DH)
        slot = lane if odd else lane - B_DH
        hi, mid, lo = _split_bf16(fc[:, hd:hd + 1] * LOG2E)
        bias = jnp.where(slot == 0, -hi, jnp.where(slot == 1, -mid, jnp.where(
            slot == 2, -lo, jnp.where(slot < 2 * N_BIAS, 1.0, 0.0))))
        kaug_ref[hd] = jnp.where(own, k[:, pair * LANES:(pair + 1) * LANES], bias).astype(_BF16)


def _fox_kv(x, kv_gain, w_kv, fg_bias):
    batch, seq, _ = x.shape
    tm = KV_TILE
    w = w_kv.astype(_BF16)
    wk, wvt, wfg = w[:, :D_MODEL], w[:, D_MODEL:2 * D_MODEL].T, w[:, 2 * D_MODEL:]
    wf = jnp.pad(wfg, ((0, 0), (0, LANES - B_HEADS)))
    wft = wfg.T
    bias = fg_bias.astype(_F32)
    bcol = jnp.pad(bias, (0, LANES - B_HEADS)).reshape(1, LANES)
    brow = bias.reshape(B_HEADS, 1)
    return pl.pallas_call(
        _fox_kv_kernel,
        grid=(batch, seq // tm),
        in_specs=[
            pl.BlockSpec((None, tm, D_MODEL), lambda b, i: (b, i, 0)), _const_spec((1, D_MODEL)),
            _const_spec((D_MODEL, D_MODEL)), _const_spec((D_MODEL, D_MODEL)),
            _const_spec((D_MODEL, LANES)), _const_spec((B_HEADS, D_MODEL)),
            _const_spec((1, LANES)), _const_spec((B_HEADS, 1)),
        ],
        out_specs=[
            pl.BlockSpec((None, None, B_HEADS, tm, LANES), lambda b, i: (b, i, 0, 0, 0)),
            pl.BlockSpec((None, None, D_MODEL, tm), lambda b, i: (b, i, 0, 0)),
            pl.BlockSpec((None, None, B_HEADS, tm), lambda b, i: (b, i, 0, 0)),
        ],
        out_shape=[
            jax.ShapeDtypeStruct((batch, seq // tm, B_HEADS, tm, LANES), _BF16),
            jax.ShapeDtypeStruct((batch, seq // tm, D_MODEL, tm), _BF16),
            jax.ShapeDtypeStruct((batch, seq // tm, B_HEADS, tm), _F32),
        ],
        scratch_shapes=[pltpu.VMEM((1, LANES), _F32), pltpu.VMEM((B_HEADS, LANES), _F32)],
        compiler_params=_params(2),
        name="fox_kv",
    )(x, kv_gain.reshape(1, D_MODEL).astype(_F32), wk, wvt, wf, wft, bcol, brow)


def _fox_attn_kernel(x_ref, sh_ref, sc_ref, g1_ref, wqt_ref, kaug_ref, vt_ref, frow_ref,
                     wout_ref, out_ref, qaug_scr, m_scr, l_scr, acc_scr, o_scr):
    tq, tk = Q_TILE, KV_TILE
    qi = pl.program_id(1)
    x = x_ref[...]
    h = _modulate(x, sh_ref[...], sc_ref[...]).astype(_BF16)
    qt = (_dot_nt(wqt_ref[...], h) * (B_DH ** -0.5 * LOG2E)).astype(_BF16)
    ft = frow_ref[qi]
    sub = lax.broadcasted_iota(jnp.int32, (B_DH, tq), 0)
    for hd in range(B_HEADS):
        hi, mid, lo = _split_bf16(ft[hd:hd + 1, :])
        bias = jnp.where(sub < N_BIAS, 1.0, jnp.where(sub == N_BIAS, hi, jnp.where(
            sub == N_BIAS + 1, mid, jnp.where(sub == N_BIAS + 2, lo, 0.0)))).astype(_BF16)
        qh = qt[hd * B_DH:(hd + 1) * B_DH, :]
        lo_half, hi_half = (bias, qh) if hd % 2 else (qh, bias)
        qaug_scr[hd, 0:B_DH, :] = lo_half
        qaug_scr[hd, B_DH:2 * B_DH, :] = hi_half
    m_scr[...] = jnp.full(m_scr.shape, -jnp.inf, _F32)
    l_scr[...] = jnp.zeros_like(l_scr)
    acc_scr[...] = jnp.zeros_like(acc_scr)

    def kv_block(j, diagonal):
        if diagonal:
            keep = (lax.broadcasted_iota(jnp.int32, (tk, tq), 1)
                    >= lax.broadcasted_iota(jnp.int32, (tk, tq), 0))
        s_next = _dot(kaug_ref[j, 0], qaug_scr[0])
        for hd in range(B_HEADS):
            rows = slice(hd * B_DH, (hd + 1) * B_DH)
            s = s_next
            if hd + 1 < B_HEADS:
                s_next = _dot(kaug_ref[j, hd + 1], qaug_scr[hd + 1])
            if diagonal:
                s = jnp.where(keep, s, -jnp.inf)
            m_old = m_scr[hd]
            m_new = jnp.maximum(m_old, jnp.max(s, axis=0, keepdims=True))
            alpha = jnp.exp2(m_old - m_new)
            p = jnp.exp2(s - m_new)
            l_scr[hd] = alpha * l_scr[hd] + jnp.sum(p, axis=0, keepdims=True)
            m_scr[hd] = m_new
            acc_scr[rows, :] = alpha * acc_scr[rows, :] + _dot(vt_ref[j, rows, :], p.astype(_BF16))

    def body(j, carry):
        kv_block(j, False)
        return carry

    lax.fori_loop(0, qi, body, 0)
    kv_block(qi, True)

    for hd in range(B_HEADS):
        rows = slice(hd * B_DH, (hd + 1) * B_DH)
        o_scr[rows, :] = (acc_scr[rows, :] / l_scr[hd]).astype(_BF16)
    out_ref[...] = x + g1_ref[...] * _dot_tn(o_scr[...], wout_ref[...])


def _fox_attn(x, mod, layer, w_q, w_out, kaug, vt, frow):
    batch, seq, _ = x.shape
    tq, tk = Q_TILE, KV_TILE
    nkv = seq // tk
    tok = pl.BlockSpec((None, tq, D_MODEL), lambda b, i: (b, i, 0))
    return pl.pallas_call(
        _fox_attn_kernel,
        grid=(batch, seq // tq),
        in_specs=[
            tok, _mod_spec(layer, 0), _mod_spec(layer, 1), _mod_spec(layer, 2),
            _const_spec((D_MODEL, D_MODEL)),
            pl.BlockSpec((None, nkv, B_HEADS, tk, LANES), lambda b, i: (b, 0, 0, 0, 0)),
            pl.BlockSpec((None, nkv, D_MODEL, tk), lambda b, i: (b, 0, 0, 0)),
            pl.BlockSpec((None, nkv, B_HEADS, tk), lambda b, i: (b, 0, 0, 0)),
            _const_spec((D_MODEL, D_MODEL)),
        ],
        out_specs=tok,
        out_shape=jax.ShapeDtypeStruct(x.shape, _F32),
        scratch_shapes=[
            pltpu.VMEM((B_HEADS, 2 * B_DH, tq), _BF16),
            pltpu.VMEM((B_HEADS, 1, tq), _F32), pltpu.VMEM((B_HEADS, 1, tq), _F32),
            pltpu.VMEM((D_MODEL, tq), _F32), pltpu.VMEM((D_MODEL, tq), _BF16),
        ],
        compiler_params=_params(2),
        name="fox_attn",
    )(x, mod, mod, mod, w_q.T.astype(_BF16), kaug, vt, frow, w_out.astype(_BF16))


def kernel(x, c, ada_w, ada_b, a_w_in, a_b_i, a_b_f, a_head_gain, a_w_out, kv_gain, b_w_kv,
           b_fg_bias, b_w_q, b_w_out, mlp_w1, mlp_w2, final_gain):
    batch, seq, d = x.shape
    assert d == D_MODEL and seq % max(PROJ_TILE, MLP_TILE, MLSTM_CHUNK, Q_TILE, KV_TILE) == 0
    assert PROJ_TILE % MLSTM_CHUNK == 0 and Q_TILE == KV_TILE
    mod = _adaln_table(c, ada_w, ada_b).reshape(DEPTH, 6, batch, 1, D_MODEL)
    shared = None
    for l in range(DEPTH):
        if l < N_A_LAYERS:
            q, k, v, o, gcol, grow = _mlstm_proj(x, mod, l, a_w_in[l], a_b_i[l], a_b_f[l])
            x = _mlstm_core(x, mod, l, q, k, v, o, gcol, grow, a_head_gain[l], a_w_out[l])
        else:
            if shared is None:
                shared = _fox_kv(x, kv_gain, b_w_kv, b_fg_bias)
            j = l - N_A_LAYERS
            x = _fox_attn(x, mod, l, b_w_q[j], b_w_out[j], *shared)
        x = _mlp(x, mod, l, mlp_w1[l], mlp_w2[l], final_gain, final_norm=(l == DEPTH - 1))
    return x
```

```python
import functools

import jax
import jax.numpy as jnp
from jax import lax
from jax.experimental import pallas as pl
from jax.experimental.pallas import tpu as pltpu

D_MODEL = 1024
DEPTH = 4
N_A_LAYERS = DEPTH // 2
A_HEADS = 4
A_DV = D_MODEL // A_HEADS
A_DQK = A_DV // 2
A_QK_W = A_HEADS * A_DQK
A_V_W = A_HEADS * A_DV
B_HEADS = 16
B_DH = D_MODEL // B_HEADS
D_FF = 4 * D_MODEL
EPS = 1e-6

LANES = 128
MLSTM_CHUNK = 256
PROJ_TILE = 512
KV_TILE = 256
Q_TILE = 256
HEAD_GROUP = 8
MLP_TILE = 512
FF_CHUNK = 512
VMEM_LIMIT = 56 * 1024 * 1024

_BF16 = jnp.bfloat16
_F32 = jnp.float32


def _dot(a, b):
    return jnp.dot(a, b, preferred_element_type=_F32)


def _dot_nt(a, b):
    return lax.dot_general(a, b, (((1,), (1,)), ((), ())), preferred_element_type=_F32)


def _dot_tn(a, b):
    return lax.dot_general(a, b, (((0,), (0,)), ((), ())), preferred_element_type=_F32)


def _rms(x):
    return x * lax.rsqrt(jnp.mean(x * x, axis=-1, keepdims=True) + EPS)


def _modulate(x, shift, scale):
    return _rms(x) * (1.0 + scale) + shift


def _log_sigmoid(z):
    return jnp.minimum(z, 0.0) - jnp.log1p(jnp.exp(-jnp.abs(z)))


def _segment_cumsum(x, axis, seg):
    pos = lax.broadcasted_iota(jnp.int32, x.shape, axis) & (seg - 1)
    k = 1
    while k < seg:
        x = x + jnp.where(pos >= k, pltpu.roll(x, k, axis), 0.0)
        k *= 2
    return x


def _params(n_grid):
    return pltpu.CompilerParams(dimension_semantics=("arbitrary",) * n_grid,
                                vmem_limit_bytes=VMEM_LIMIT)


def _const_spec(shape):
    return pl.BlockSpec(shape, lambda *_: (0,) * len(shape), pipeline_mode=pl.Buffered(1))


def _mod_spec(layer, slot):
    return pl.BlockSpec((None, None, None, 1, D_MODEL), lambda b, i: (layer, slot, b, 0, 0))


def _adaln_kernel(c_ref, w_ref, b_ref, o_ref):
    c = c_ref[...]
    cond = (c * jax.nn.sigmoid(c)).astype(_BF16)
    o_ref[...] = _dot(cond, w_ref[...].astype(_BF16)) + b_ref[...]


def _adaln_table(c, ada_w, ada_b):
    batch = c.shape[0]
    return pl.pallas_call(
        _adaln_kernel,
        grid=(DEPTH, 6),
        in_specs=[
            pl.BlockSpec((batch, D_MODEL), lambda l, j: (0, 0)),
            pl.BlockSpec((None, D_MODEL, D_MODEL), lambda l, j: (l, 0, j)),
            pl.BlockSpec((None, None, 1, D_MODEL), lambda l, j: (l, j, 0, 0)),
        ],
        out_specs=pl.BlockSpec((None, None, batch, D_MODEL), lambda l, j: (l, j, 0, 0)),
        out_shape=jax.ShapeDtypeStruct((DEPTH, 6, batch, D_MODEL), _F32),
        compiler_params=_params(2),
        name="adaln_table",
    )(c, ada_w, ada_b.reshape(DEPTH, 6, 1, D_MODEL))


def _mlstm_proj_kernel(x_ref, sh_ref, sc_ref, wq_ref, wk_ref, wv_ref, wo_ref, wg_ref, wgt_ref,
                       bcol_ref, brow_ref, q_ref, k_ref, v_ref, o_ref, gcol_ref, grow_ref):
    h = _modulate(x_ref[...], sh_ref[...], sc_ref[...]).astype(_BF16)
    q_ref[...] = (_dot(h, wq_ref[...]) * (A_DQK ** -0.5)).astype(_BF16)
    k_ref[...] = _dot(h, wk_ref[...]).astype(_BF16)
    v_ref[...] = _dot(h, wv_ref[...]).astype(_BF16)
    o_ref[...] = _dot(h, wo_ref[...])
    zc = _dot(h, wg_ref[...]) + bcol_ref[...]
    bc = _segment_cumsum(_log_sigmoid(zc), 0, MLSTM_CHUNK)
    lane = lax.broadcasted_iota(jnp.int32, zc.shape, 1)
    gcol_ref[...] = jnp.where(lane < A_HEADS, zc, bc)[:, :2 * A_HEADS]
    zr = _dot_nt(wgt_ref[...], h) + brow_ref[...]
    br = _segment_cumsum(_log_sigmoid(zr), 1, MLSTM_CHUNK)
    sub = lax.broadcasted_iota(jnp.int32, zr.shape, 0)
    grow_ref[...] = jnp.where(sub < A_HEADS, zr, br)


def _mlstm_proj(x, mod, layer, w_in, b_i, b_f):
    batch, seq, _ = x.shape
    tm = PROJ_TILE
    w = w_in.astype(_BF16)
    wq, wk = w[:, :A_QK_W], w[:, A_QK_W:2 * A_QK_W]
    wv = w[:, 2 * A_QK_W:2 * A_QK_W + A_V_W]
    wo = w[:, 2 * A_QK_W + A_V_W:2 * A_QK_W + 2 * A_V_W]
    wgate = w[:, 2 * A_QK_W + 2 * A_V_W:]
    wg = jnp.pad(wgate, ((0, 0), (0, LANES - 2 * A_HEADS)))
    wgt = wgate.T
    bias = jnp.concatenate([b_i, b_f]).astype(_F32)
    bcol = jnp.pad(bias, (0, LANES - 2 * A_HEADS)).reshape(1, LANES)
    brow = bias.reshape(2 * A_HEADS, 1)
    tok = lambda width: pl.BlockSpec((None, tm, width), lambda b, i: (b, i, 0))
    return pl.pallas_call(
        _mlstm_proj_kernel,
        grid=(batch, seq // tm),
        in_specs=[
            tok(D_MODEL), _mod_spec(layer, 0), _mod_spec(layer, 1),
            _const_spec((D_MODEL, A_QK_W)), _const_spec((D_MODEL, A_QK_W)),
            _const_spec((D_MODEL, A_V_W)), _const_spec((D_MODEL, A_V_W)),
            _const_spec((D_MODEL, LANES)), _const_spec((2 * A_HEADS, D_MODEL)),
            _const_spec((1, LANES)), _const_spec((2 * A_HEADS, 1)),
        ],
        out_specs=[
            tok(A_QK_W), tok(A_QK_W), tok(A_V_W), tok(A_V_W), tok(2 * A_HEADS),
            pl.BlockSpec((None, 2 * A_HEADS, tm), lambda b, i: (b, 0, i)),
        ],
        out_shape=[
            jax.ShapeDtypeStruct((batch, seq, A_QK_W), _BF16),
            jax.ShapeDtypeStruct((batch, seq, A_QK_W), _BF16),
            jax.ShapeDtypeStruct((batch, seq, A_V_W), _BF16),
            jax.ShapeDtypeStruct((batch, seq, A_V_W), _F32),
            jax.ShapeDtypeStruct((batch, seq, 2 * A_HEADS), _F32),
            jax.ShapeDtypeStruct((batch, 2 * A_HEADS, seq), _F32),
        ],
        compiler_params=_params(2),
        name="mlstm_proj",
    )(x, mod, mod, wq, wk, wv, wo, wg, wgt, bcol, brow)


def _mlstm_core_kernel(x_ref, g1_ref, q_ref, k_ref, v_ref, o_ref, gcol_ref, grow_ref,
                       gain_ref, wout_ref, out_ref, c_scr, n_scr, m_scr, z_scr):
    L = MLSTM_CHUNK

    @pl.when(pl.program_id(1) == 0)
    def _():
        c_scr[...] = jnp.zeros_like(c_scr)
        n_scr[...] = jnp.zeros_like(n_scr)
        m_scr[...] = jnp.zeros_like(m_scr)

    gcol = gcol_ref[...]
    grow = grow_ref[...]
    row = lax.broadcasted_iota(jnp.int32, (L, L), 0)
    col = lax.broadcasted_iota(jnp.int32, (L, L), 1)
    causal = row >= col
    for h in range(A_HEADS):
        qh = q_ref[:, h * A_DQK:(h + 1) * A_DQK]
        kh = k_ref[:, h * A_DQK:(h + 1) * A_DQK]
        vh = v_ref[:, h * A_DV:(h + 1) * A_DV]
        ic = gcol[:, h:h + 1]
        bc = gcol[:, A_HEADS + h:A_HEADS + h + 1]
        ir = grow[h:h + 1, :]
        br = grow[A_HEADS + h:A_HEADS + h + 1, :]
        m_prev = m_scr[h][:, 0:1]
        c_prev = c_scr[h]
        n_prev = n_scr[h]

        d = jnp.where(causal, bc - br + ir, -jnp.inf)
        inter = bc + m_prev
        m_t = jnp.maximum(inter, jnp.max(d, axis=-1, keepdims=True))
        w = jnp.exp(d - m_t)
        w_inter = jnp.exp(inter - m_t)
        a = w * _dot_nt(qh, kh)
        num = w_inter * _dot(qh, c_prev.astype(_BF16)) + _dot(a.astype(_BF16), vh)
        den = (w_inter * jnp.sum(qh.astype(_F32) * n_prev, axis=-1, keepdims=True)
               + jnp.sum(a, axis=-1, keepdims=True))
        ht = num / jnp.maximum(jnp.abs(den), jnp.exp(-m_t))

        b_last = br[:, L - 1:L]
        m_new = jnp.maximum(b_last + m_prev,
                            jnp.max(b_last - br + ir, axis=-1, keepdims=True))
        wl = jnp.exp(b_last - bc + ic - m_new)
        decay = jnp.exp(b_last + m_prev - m_new)
        kw = wl * kh.astype(_F32)
        c_scr[h] = decay * c_prev + _dot_tn(kw.astype(_BF16), vh)
        n_scr[h] = decay * n_prev + jnp.sum(kw, axis=0, keepdims=True)
        m_scr[h] = jnp.broadcast_to(m_new, (1, LANES))

        ht = _rms(ht) * gain_ref[:, h * A_DV:(h + 1) * A_DV]
        og = jax.nn.sigmoid(o_ref[:, h * A_DV:(h + 1) * A_DV])
        z_scr[:, h * A_DV:(h + 1) * A_DV] = (og * ht).astype(_BF16)

    out_ref[...] = x_ref[...] + g1_ref[...] * _dot(z_scr[...], wout_ref[...])


def _mlstm_core(x, mod, layer, q, k, v, o, gcol, grow, head_gain, w_out):
    batch, seq, _ = x.shape
    L = MLSTM_CHUNK
    tok = lambda width: pl.BlockSpec((None, L, width), lambda b, i: (b, i, 0))
    return pl.pallas_call(
        _mlstm_core_kernel,
        grid=(batch, seq // L),
        in_specs=[
            tok(D_MODEL), _mod_spec(layer, 2),
            tok(A_QK_W), tok(A_QK_W), tok(A_V_W), tok(A_V_W), tok(2 * A_HEADS),
            pl.BlockSpec((None, 2 * A_HEADS, L), lambda b, i: (b, 0, i)),
            _const_spec((1, A_V_W)), _const_spec((A_V_W, D_MODEL)),
        ],
        out_specs=tok(D_MODEL),
        out_shape=jax.ShapeDtypeStruct(x.shape, _F32),
        scratch_shapes=[
            pltpu.VMEM((A_HEADS, A_DQK, A_DV), _F32),
            pltpu.VMEM((A_HEADS, 1, A_DQK), _F32),
            pltpu.VMEM((A_HEADS, 1, LANES), _F32),
            pltpu.VMEM((L, A_V_W), _BF16),
        ],
        compiler_params=_params(2),
        name="mlstm_core",
    )(x, mod, q, k, v, o, gcol, grow, head_gain.reshape(1, A_V_W).astype(_F32),
      w_out.astype(_BF16))


def _mlp_kernel(x_ref, sh_ref, sc_ref, g_ref, w1_ref, w2_ref, fgain_ref, out_ref, u_scr, *,
                final_norm):
    x = x_ref[...]
    h = _modulate(x, sh_ref[...], sc_ref[...]).astype(_BF16)
    for c in range(D_FF // FF_CHUNK):
        u = jnp.maximum(_dot(h, w1_ref[:, c * FF_CHUNK:(c + 1) * FF_CHUNK]), 0.0)
        u_scr[:, c * FF_CHUNK:(c + 1) * FF_CHUNK] = (u * u).astype(_BF16)
    y = x + g_ref[...] * _dot(u_scr[...], w2_ref[...])
    if final_norm:
        y = _rms(y) * fgain_ref[...]
    out_ref[...] = y


def _layer_spec(layer, rows, cols):
    return pl.BlockSpec((None, rows, cols), lambda *_: (layer, 0, 0),
                        pipeline_mode=pl.Buffered(1))


def _mlp(x, mod, layer, w1, w2, final_gain, final_norm):
    batch, seq, _ = x.shape
    tm = MLP_TILE
    tok = pl.BlockSpec((None, tm, D_MODEL), lambda b, i: (b, i, 0))
    return pl.pallas_call(
        functools.partial(_mlp_kernel, final_norm=final_norm),
        grid=(batch, seq // tm),
        in_specs=[
            tok, _mod_spec(layer, 3), _mod_spec(layer, 4), _mod_spec(layer, 5),
            _layer_spec(layer, D_MODEL, D_FF), _layer_spec(layer, D_FF, D_MODEL),
            _const_spec((1, D_MODEL)),
        ],
        out_specs=tok,
        out_shape=jax.ShapeDtypeStruct(x.shape, _F32),
        scratch_shapes=[pltpu.VMEM((tm, D_FF), _BF16)],
        compiler_params=_params(2),
        name="mlp",
    )(x, mod, mod, mod, w1, w2, final_gain.reshape(1, D_MODEL).astype(_F32))


def _split_bf16(x):
    hi = x.astype(_BF16).astype(_F32)
    rest = x - hi
    mid = rest.astype(_BF16).astype(_F32)
    lo = (rest - mid).astype(_BF16).astype(_F32)
    return hi, mid, lo


LOG2E = 1.4426950408889634
N_BIAS = 3
V_ROWS = B_DH + 16


def _fox_kv_kernel(x_ref, gain_ref, wk_ref, wvt_ref, wf_ref, wft_ref, bcol_ref, brow_ref,
                   kaug_ref, vt_ref, frow_ref, ccol_scr, crow_scr):
    tm = KV_TILE

    @pl.when(pl.program_id(1) == 0)
    def _():
        ccol_scr[...] = jnp.zeros_like(ccol_scr)
        crow_scr[...] = jnp.zeros_like(crow_scr)

    h = (_rms(x_ref[...]) * gain_ref[...]).astype(_BF16)
    vt = _dot_nt(wvt_ref[...], h).astype(_BF16)
    ones = jnp.ones((V_ROWS - B_DH, tm), _BF16)
    for hd in range(B_HEADS):
        vt_ref[hd, 0:B_DH, :] = vt[hd * B_DH:(hd + 1) * B_DH, :]
        vt_ref[hd, B_DH:V_ROWS, :] = ones
    k = _dot(h, wk_ref[...])
    fc = _segment_cumsum(_log_sigmoid(_dot(h, wf_ref[...]) + bcol_ref[...]), 0, tm)
    fc = fc + ccol_scr[...]
    ccol_scr[...] = fc[tm - 1:tm, :]
    fr = _segment_cumsum(_log_sigmoid(_dot_nt(wft_ref[...], h) + brow_ref[...]), 1, tm)
    fr = fr + crow_scr[:, 0:1]
    crow_scr[...] = jnp.broadcast_to(fr[:, tm - 1:tm], crow_scr.shape)
    frow_ref[...] = fr * LOG2E

    lane = lax.broadcasted_iota(jnp.int32, (tm, LANES), 1)
    for hd in range(B_HEADS):
        pair, odd = divmod(hd, 2)
        own = (lane >= B_DH) if odd else (lane < B_DH)
        slot = lane if odd else lane - B_DH
        hi, mid, lo = _split_bf16(fc[:, hd:hd + 1] * LOG2E)
        bias = jnp.where(slot == 0, -hi, jnp.where(slot == 1, -mid, jnp.where(
            slot == 2, -lo, jnp.where(slot < 2 * N_BIAS, 1.0, 0.0))))
        kaug_ref[hd] = jnp.where(own, k[:, pair * LANES:(pair + 1) * LANES], bias).astype(_BF16)


def _fox_kv(x, kv_gain, w_kv, fg_bias):
    batch, seq, _ = x.shape
    tm = KV_TILE
    w = w_kv.astype(_BF16)
    wk, wvt, wfg = w[:, :D_MODEL], w[:, D_MODEL:2 * D_MODEL].T, w[:, 2 * D_MODEL:]
    wf = jnp.pad(wfg, ((0, 0), (0, LANES - B_HEADS)))
    wft = wfg.T
    bias = fg_bias.astype(_F32)
    bcol = jnp.pad(bias, (0, LANES - B_HEADS)).reshape(1, LANES)
    brow = bias.reshape(B_HEADS, 1)
    return pl.pallas_call(
        _fox_kv_kernel,
        grid=(batch, seq // tm),
        in_specs=[
            pl.BlockSpec((None, tm, D_MODEL), lambda b, i: (b, i, 0)), _const_spec((1, D_MODEL)),
            _const_spec((D_MODEL, D_MODEL)), _const_spec((D_MODEL, D_MODEL)),
            _const_spec((D_MODEL, LANES)), _const_spec((B_HEADS, D_MODEL)),
            _const_spec((1, LANES)), _const_spec((B_HEADS, 1)),
        ],
        out_specs=[
            pl.BlockSpec((None, None, B_HEADS, tm, LANES), lambda b, i: (b, i, 0, 0, 0)),
            pl.BlockSpec((None, None, B_HEADS, V_ROWS, tm), lambda b, i: (b, i, 0, 0, 0)),
            pl.BlockSpec((None, None, B_HEADS, tm), lambda b, i: (b, i, 0, 0)),
        ],
        out_shape=[
            jax.ShapeDtypeStruct((batch, seq // tm, B_HEADS, tm, LANES), _BF16),
            jax.ShapeDtypeStruct((batch, seq // tm, B_HEADS, V_ROWS, tm), _BF16),
            jax.ShapeDtypeStruct((batch, seq // tm, B_HEADS, tm), _F32),
        ],
        scratch_shapes=[pltpu.VMEM((1, LANES), _F32), pltpu.VMEM((B_HEADS, LANES), _F32)],
        compiler_params=_params(2),
        name="fox_kv",
    )(x, kv_gain.reshape(1, D_MODEL).astype(_F32), wk, wvt, wf, wft, bcol, brow)


def _fox_attn_kernel(x_ref, sh_ref, sc_ref, g1_ref, wqt_ref, kaug_ref, vt_ref, frow_ref,
                     wout_ref, out_ref, qaug_scr, m_scr, acc_scr, o_scr):
    tq, tk = Q_TILE, KV_TILE
    qi = pl.program_id(1)
    x = x_ref[...]
    h = _modulate(x, sh_ref[...], sc_ref[...]).astype(_BF16)
    qt = (_dot_nt(wqt_ref[...], h) * (B_DH ** -0.5 * LOG2E)).astype(_BF16)
    ft = frow_ref[qi]
    sub = lax.broadcasted_iota(jnp.int32, (B_DH, tq), 0)
    for hd in range(B_HEADS):
        hi, mid, lo = _split_bf16(ft[hd:hd + 1, :])
        bias = jnp.where(sub < N_BIAS, 1.0, jnp.where(sub == N_BIAS, hi, jnp.where(
            sub == N_BIAS + 1, mid, jnp.where(sub == N_BIAS + 2, lo, 0.0)))).astype(_BF16)
        qh = qt[hd * B_DH:(hd + 1) * B_DH, :]
        lo_half, hi_half = (bias, qh) if hd % 2 else (qh, bias)
        qaug_scr[hd, 0:B_DH, :] = lo_half
        qaug_scr[hd, B_DH:2 * B_DH, :] = hi_half
    m_scr[...] = jnp.full(m_scr.shape, -jnp.inf, _F32)
    acc_scr[...] = jnp.zeros_like(acc_scr)

    def kv_block(j, diagonal):
        if diagonal:
            keep = (lax.broadcasted_iota(jnp.int32, (tk, tq), 1)
                    >= lax.broadcasted_iota(jnp.int32, (tk, tq), 0))
        def scores(g):
            return [_dot(kaug_ref[j, hd], qaug_scr[hd])
                    for hd in range(g * HEAD_GROUP, (g + 1) * HEAD_GROUP)]

        n_groups = B_HEADS // HEAD_GROUP
        s_next = scores(0)
        for g in range(n_groups):
            heads = range(g * HEAD_GROUP, (g + 1) * HEAD_GROUP)
            s_cur = s_next
            if g + 1 < n_groups:
                s_next = scores(g + 1)
            if diagonal:
                s_cur = [jnp.where(keep, s, -jnp.inf) for s in s_cur]
            m_old = [m_scr[hd] for hd in heads]
            m_new = [jnp.maximum(mo, jnp.max(s, axis=0, keepdims=True))
                     for mo, s in zip(m_old, s_cur)]
            alpha = [jnp.exp2(mo - mn) for mo, mn in zip(m_old, m_new)]
            p = [jnp.exp2(s - mn) for s, mn in zip(s_cur, m_new)]
            for i, hd in enumerate(heads):
                m_scr[hd] = m_new[i]
                acc_scr[hd] = alpha[i] * acc_scr[hd] + _dot(vt_ref[j, hd], p[i].astype(_BF16))

    def body(j, carry):
        kv_block(j, False)
        return carry

    lax.fori_loop(0, qi, body, 0)
    kv_block(qi, True)

    for hd in range(B_HEADS):
        o_scr[hd * B_DH:(hd + 1) * B_DH, :] = (
            acc_scr[hd, 0:B_DH, :] / acc_scr[hd, B_DH:B_DH + 1, :]).astype(_BF16)
    out_ref[...] = x + g1_ref[...] * _dot_tn(o_scr[...], wout_ref[...])


def _fox_attn(x, mod, layer, w_q, w_out, kaug, vt, frow):
    batch, seq, _ = x.shape
    tq, tk = Q_TILE, KV_TILE
    nkv = seq // tk
    tok = pl.BlockSpec((None, tq, D_MODEL), lambda b, i: (b, i, 0))
    return pl.pallas_call(
        _fox_attn_kernel,
        grid=(batch, seq // tq),
        in_specs=[
            tok, _mod_spec(layer, 0), _mod_spec(layer, 1), _mod_spec(layer, 2),
            _const_spec((D_MODEL, D_MODEL)),
            pl.BlockSpec((None, nkv, B_HEADS, tk, LANES), lambda b, i: (b, 0, 0, 0, 0)),
            pl.BlockSpec((None, nkv, B_HEADS, V_ROWS, tk), lambda b, i: (b, 0, 0, 0, 0)),
            pl.BlockSpec((None, nkv, B_HEADS, tk), lambda b, i: (b, 0, 0, 0)),
            _const_spec((D_MODEL, D_MODEL)),
        ],
        out_specs=tok,
        out_shape=jax.ShapeDtypeStruct(x.shape, _F32),
        scratch_shapes=[
            pltpu.VMEM((B_HEADS, 2 * B_DH, tq), _BF16),
            pltpu.VMEM((B_HEADS, 1, tq), _F32),
            pltpu.VMEM((B_HEADS, V_ROWS, tq), _F32), pltpu.VMEM((D_MODEL, tq), _BF16),
        ],
        compiler_params=_params(2),
        name="fox_attn",
    )(x, mod, mod, mod, w_q.T.astype(_BF16), kaug, vt, frow, w_out.astype(_BF16))


def kernel(x, c, ada_w, ada_b, a_w_in, a_b_i, a_b_f, a_head_gain, a_w_out, kv_gain, b_w_kv,
           b_fg_bias, b_w_q, b_w_out, mlp_w1, mlp_w2, final_gain):
    batch, seq, d = x.shape
    assert d == D_MODEL and seq % max(PROJ_TILE, MLP_TILE, MLSTM_CHUNK, Q_TILE, KV_TILE) == 0
    assert PROJ_TILE % MLSTM_CHUNK == 0 and Q_TILE == KV_TILE
    mod = _adaln_table(c, ada_w, ada_b).reshape(DEPTH, 6, batch, 1, D_MODEL)
    w1_all, w2_all = mlp_w1.astype(_BF16), mlp_w2.astype(_BF16)
    shared = None
    for l in range(DEPTH):
        if l < N_A_LAYERS:
            q, k, v, o, gcol, grow = _mlstm_proj(x, mod, l, a_w_in[l], a_b_i[l], a_b_f[l])
            x = _mlstm_core(x, mod, l, q, k, v, o, gcol, grow, a_head_gain[l], a_w_out[l])
        else:
            if shared is None:
                shared = _fox_kv(x, kv_gain, b_w_kv, b_fg_bias)
            j = l - N_A_LAYERS
            x = _fox_attn(x, mod, l, b_w_q[j], b_w_out[j], *shared)
        x = _mlp(x, mod, l, w1_all, w2_all, final_gain, final_norm=(l == DEPTH - 1))
    return x
```

```python
import functools

import jax
import jax.numpy as jnp
from jax import lax
from jax.experimental import pallas as pl
from jax.experimental.pallas import tpu as pltpu

D_MODEL = 1024
DEPTH = 4
N_A_LAYERS = DEPTH // 2
A_HEADS = 4
A_DV = D_MODEL // A_HEADS
A_DQK = A_DV // 2
A_QK_W = A_HEADS * A_DQK
A_V_W = A_HEADS * A_DV
B_HEADS = 16
B_DH = D_MODEL // B_HEADS
D_FF = 4 * D_MODEL
EPS = 1e-6
LOG2E = 1.4426950408889634
AV_ROWS = A_DV + 16
V_ROWS = B_DH + 16

LANES = 128
MLSTM_CHUNK = 256
PROJ_TILE = 512
KV_TILE = 256
KV_STEP = 2
Q_TILE = 256
HEAD_GROUP = 8
MLP_TILE = 512
FF_CHUNK = 512
VMEM_LIMIT = 56 * 1024 * 1024

_BF16 = jnp.bfloat16
_F32 = jnp.float32


def _dot(a, b):
    return jnp.dot(a, b, preferred_element_type=_F32)


def _dot_nt(a, b):
    return lax.dot_general(a, b, (((1,), (1,)), ((), ())), preferred_element_type=_F32)


def _dot_tn(a, b):
    return lax.dot_general(a, b, (((0,), (0,)), ((), ())), preferred_element_type=_F32)


def _rms(x):
    return x * lax.rsqrt(jnp.mean(x * x, axis=-1, keepdims=True) + EPS)


def _modulate(x, shift, scale):
    return _rms(x) * (1.0 + scale) + shift


def _log_sigmoid(z):
    return jnp.minimum(z, 0.0) - jnp.log1p(jnp.exp(-jnp.abs(z)))


def _segment_cumsum(x, axis, seg):
    pos = lax.broadcasted_iota(jnp.int32, x.shape, axis) & (seg - 1)
    k = 1
    while k < seg:
        x = x + jnp.where(pos >= k, pltpu.roll(x, k, axis), 0.0)
        k *= 2
    return x


def _params(n_grid):
    return pltpu.CompilerParams(dimension_semantics=("arbitrary",) * n_grid,
                                vmem_limit_bytes=VMEM_LIMIT)


def _const_spec(shape):
    return pl.BlockSpec(shape, lambda *_: (0,) * len(shape), pipeline_mode=pl.Buffered(1))


def _mod_spec(layer, slot):
    return pl.BlockSpec((None, None, None, 1, D_MODEL), lambda b, i: (layer, slot, b, 0, 0))


def _adaln_kernel(c_ref, w_ref, b_ref, o_ref):
    c = c_ref[...]
    cond = (c * jax.nn.sigmoid(c)).astype(_BF16)
    o_ref[...] = _dot(cond, w_ref[...].astype(_BF16)) + b_ref[...]


def _adaln_table(c, ada_w, ada_b):
    batch = c.shape[0]
    return pl.pallas_call(
        _adaln_kernel,
        grid=(DEPTH, 6),
        in_specs=[
            pl.BlockSpec((batch, D_MODEL), lambda l, j: (0, 0)),
            pl.BlockSpec((None, D_MODEL, D_MODEL), lambda l, j: (l, 0, j)),
            pl.BlockSpec((None, None, 1, D_MODEL), lambda l, j: (l, j, 0, 0)),
        ],
        out_specs=pl.BlockSpec((None, None, batch, D_MODEL), lambda l, j: (l, j, 0, 0)),
        out_shape=jax.ShapeDtypeStruct((DEPTH, 6, batch, D_MODEL), _F32),
        compiler_params=_params(2),
        name="adaln_table",
    )(c, ada_w, ada_b.reshape(DEPTH, 6, 1, D_MODEL))


def _mlstm_proj_kernel(x_ref, sh_ref, sc_ref, wqt_ref, wk_ref, wvt_ref, wot_ref, wg_ref, wgt_ref,
                       bcol_ref, brow_ref, qt_ref, k_ref, vt_ref, ot_ref, gcol_ref, grow_ref):
    ct = MLSTM_CHUNK
    ones = jnp.ones((AV_ROWS - A_DV, ct), _BF16)
    sub = lax.broadcasted_iota(jnp.int32, (2 * A_HEADS, ct), 0)
    for c in range(PROJ_TILE // ct):
        tok = slice(c * ct, (c + 1) * ct)
        h = _modulate(x_ref[tok, :], sh_ref[...], sc_ref[...]).astype(_BF16)
        qt_ref[:, tok] = (_dot_nt(wqt_ref[...], h) * (A_DQK ** -0.5)).astype(_BF16)
        k_ref[tok, :] = _dot(h, wk_ref[...]).astype(_BF16)
        vt = _dot_nt(wvt_ref[...], h).astype(_BF16)
        for hd in range(A_HEADS):
            vt_ref[hd, 0:A_DV, tok] = vt[hd * A_DV:(hd + 1) * A_DV, :]
            vt_ref[hd, A_DV:AV_ROWS, tok] = ones
        ot_ref[:, tok] = _dot_nt(wot_ref[...], h)
        z = _dot(h, wg_ref[...]) + bcol_ref[...]
        bc = _segment_cumsum(_log_sigmoid(z), 0, ct)
        gcol_ref[tok, :] = ((z - pltpu.roll(bc, LANES - A_HEADS, 1)) * LOG2E)[:, :2 * A_HEADS]
        zr = _dot_nt(wgt_ref[...], h) + brow_ref[...]
        br = _segment_cumsum(_log_sigmoid(zr), 1, ct)
        grow_ref[:, tok] = jnp.where(sub < A_HEADS, zr, br) * LOG2E


def _mlstm_proj(x, mod, layer, w_in, b_i, b_f):
    batch, seq, _ = x.shape
    tm = PROJ_TILE
    w = w_in.astype(_BF16)
    wqt, wk = w[:, :A_QK_W].T, w[:, A_QK_W:2 * A_QK_W]
    wvt = w[:, 2 * A_QK_W:2 * A_QK_W + A_V_W].T
    wot = w[:, 2 * A_QK_W + A_V_W:2 * A_QK_W + 2 * A_V_W].T
    wgate = w[:, 2 * A_QK_W + 2 * A_V_W:]
    wg = jnp.pad(wgate, ((0, 0), (0, LANES - 2 * A_HEADS)))
    bias = jnp.concatenate([b_i, b_f]).astype(_F32)
    bcol = jnp.pad(bias, (0, LANES - 2 * A_HEADS)).reshape(1, LANES)
    brow = bias.reshape(2 * A_HEADS, 1)
    rows = lambda n: pl.BlockSpec((None, n, tm), lambda b, i: (b, 0, i))
    tok = lambda width: pl.BlockSpec((None, tm, width), lambda b, i: (b, i, 0))
    return pl.pallas_call(
        _mlstm_proj_kernel,
        grid=(batch, seq // tm),
        in_specs=[
            tok(D_MODEL), _mod_spec(layer, 0), _mod_spec(layer, 1),
            _const_spec((A_QK_W, D_MODEL)), _const_spec((D_MODEL, A_QK_W)),
            _const_spec((A_V_W, D_MODEL)), _const_spec((A_V_W, D_MODEL)),
            _const_spec((D_MODEL, LANES)), _const_spec((2 * A_HEADS, D_MODEL)),
            _const_spec((1, LANES)), _const_spec((2 * A_HEADS, 1)),
        ],
        out_specs=[
            rows(A_QK_W), tok(A_QK_W),
            pl.BlockSpec((None, A_HEADS, AV_ROWS, tm), lambda b, i: (b, 0, 0, i)),
            rows(A_V_W), tok(2 * A_HEADS), rows(2 * A_HEADS),
        ],
        out_shape=[
            jax.ShapeDtypeStruct((batch, A_QK_W, seq), _BF16),
            jax.ShapeDtypeStruct((batch, seq, A_QK_W), _BF16),
            jax.ShapeDtypeStruct((batch, A_HEADS, AV_ROWS, seq), _BF16),
            jax.ShapeDtypeStruct((batch, A_V_W, seq), _F32),
            jax.ShapeDtypeStruct((batch, seq, 2 * A_HEADS), _F32),
            jax.ShapeDtypeStruct((batch, 2 * A_HEADS, seq), _F32),
        ],
        compiler_params=_params(2),
        name="mlstm_proj",
    )(x, mod, mod, wqt, wk, wvt, wot, wg, wgate.T, bcol, brow)


def _mlstm_core_kernel(x_ref, g1_ref, qt_ref, k_ref, vt_ref, ot_ref, gcol_ref, grow_ref,
                       gain_ref, wout_ref, out_ref, cn_scr, m_scr, z_scr):
    L = MLSTM_CHUNK
    heads = range(A_HEADS)

    @pl.when(pl.program_id(1) == 0)
    def _():
        cn_scr[...] = jnp.zeros_like(cn_scr)
        m_scr[...] = jnp.zeros_like(m_scr)

    gcol = gcol_ref[...]
    grow = grow_ref[...]
    causal = (lax.broadcasted_iota(jnp.int32, (L, L), 0)
              <= lax.broadcasted_iota(jnp.int32, (L, L), 1))
    kh = [k_ref[:, h * A_DQK:(h + 1) * A_DQK] for h in heads]
    qt = [qt_ref[h * A_DQK:(h + 1) * A_DQK, :] for h in heads]
    st = [_dot(kh[h], qt[h]) for h in heads]
    inter_mm = [_dot(cn_scr[h].astype(_BF16), qt[h]) for h in heads]

    i_row = [grow[h:h + 1, :] for h in heads]
    b_row = [grow[A_HEADS + h:A_HEADS + h + 1, :] for h in heads]
    m_prev = [m_scr[h][:, 0:1] for h in heads]
    d = [jnp.where(causal, gcol[:, h:h + 1] + b_row[h], -jnp.inf) for h in heads]
    inter = [b_row[h] + m_prev[h] for h in heads]
    m_t = [jnp.maximum(inter[h], jnp.max(d[h], axis=0, keepdims=True)) for h in heads]
    a = [(jnp.exp2(d[h] - m_t[h]) * st[h]).astype(_BF16) for h in heads]
    w_inter = [jnp.exp2(inter[h] - m_t[h]) for h in heads]
    nd = [w_inter[h] * inter_mm[h] + _dot(vt_ref[h], a[h]) for h in heads]

    b_last = [b_row[h][:, L - 1:L] for h in heads]
    dl = [b_last[h] - b_row[h] + i_row[h] for h in heads]
    m_new = [jnp.maximum(b_last[h] + m_prev[h], jnp.max(dl[h], axis=1, keepdims=True))
             for h in heads]
    for h in heads:
        vw = (vt_ref[h].astype(_F32) * jnp.exp2(dl[h] - m_new[h])).astype(_BF16)
        decay = jnp.exp2(b_last[h] + m_prev[h] - m_new[h])
        cn_scr[h] = decay * cn_scr[h] + _dot(vw, kh[h])
        m_scr[h] = jnp.broadcast_to(m_new[h], (1, LANES))

    for h in heads:
        rows = slice(h * A_DV, (h + 1) * A_DV)
        den = nd[h][A_DV:A_DV + 1, :]
        ht = nd[h][0:A_DV, :] / jnp.maximum(jnp.abs(den), jnp.exp2(-m_t[h]))
        ht = ht * lax.rsqrt(jnp.mean(ht * ht, axis=0, keepdims=True) + EPS) * gain_ref[rows, :]
        z_scr[rows, :] = (jax.nn.sigmoid(ot_ref[rows, :]) * ht).astype(_BF16)

    out_ref[...] = x_ref[...] + g1_ref[...] * _dot_tn(z_scr[...], wout_ref[...])


def _mlstm_core(x, mod, layer, qt, k, vt, ot, gcol, grow, head_gain, w_out):
    batch, seq, _ = x.shape
    L = MLSTM_CHUNK
    tok = lambda width: pl.BlockSpec((None, L, width), lambda b, i: (b, i, 0))
    rows = lambda n: pl.BlockSpec((None, n, L), lambda b, i: (b, 0, i))
    gain = jnp.broadcast_to(head_gain.reshape(A_V_W, 1).astype(_F32), (A_V_W, L))
    return pl.pallas_call(
        _mlstm_core_kernel,
        grid=(batch, seq // L),
        in_specs=[
            tok(D_MODEL), _mod_spec(layer, 2),
            rows(A_QK_W), tok(A_QK_W),
            pl.BlockSpec((None, A_HEADS, AV_ROWS, L), lambda b, i: (b, 0, 0, i)),
            rows(A_V_W), tok(2 * A_HEADS), rows(2 * A_HEADS),
            _const_spec((A_V_W, L)), _const_spec((A_V_W, D_MODEL)),
        ],
        out_specs=tok(D_MODEL),
        out_shape=jax.ShapeDtypeStruct(x.shape, _F32),
        scratch_shapes=[
            pltpu.VMEM((A_HEADS, AV_ROWS, A_DQK), _F32),
            pltpu.VMEM((A_HEADS, 1, LANES), _F32),
            pltpu.VMEM((A_V_W, L), _BF16),
        ],
        compiler_params=_params(2),
        name="mlstm_core",
    )(x, mod, qt, k, vt, ot, gcol, grow, gain, w_out.astype(_BF16))


def _mlp_kernel(x_ref, sh_ref, sc_ref, g_ref, w1_ref, w2_ref, fgain_ref, out_ref, u_scr, *,
                final_norm):
    x = x_ref[...]
    h = _modulate(x, sh_ref[...], sc_ref[...]).astype(_BF16)
    for c in range(D_FF // FF_CHUNK):
        u = jnp.maximum(_dot(h, w1_ref[:, c * FF_CHUNK:(c + 1) * FF_CHUNK]), 0.0)
        u_scr[:, c * FF_CHUNK:(c + 1) * FF_CHUNK] = (u * u).astype(_BF16)
    y = x + g_ref[...] * _dot(u_scr[...], w2_ref[...])
    if final_norm:
        y = _rms(y) * fgain_ref[...]
    out_ref[...] = y


def _layer_spec(layer, rows, cols):
    return pl.BlockSpec((None, rows, cols), lambda *_: (layer, 0, 0),
                        pipeline_mode=pl.Buffered(1))


def _mlp(x, mod, layer, w1, w2, final_gain, final_norm):
    batch, seq, _ = x.shape
    tm = MLP_TILE
    tok = pl.BlockSpec((None, tm, D_MODEL), lambda b, i: (b, i, 0))
    return pl.pallas_call(
        functools.partial(_mlp_kernel, final_norm=final_norm),
        grid=(batch, seq // tm),
        in_specs=[
            tok, _mod_spec(layer, 3), _mod_spec(layer, 4), _mod_spec(layer, 5),
            _layer_spec(layer, D_MODEL, D_FF), _layer_spec(layer, D_FF, D_MODEL),
            _const_spec((1, D_MODEL)),
        ],
        out_specs=tok,
        out_shape=jax.ShapeDtypeStruct(x.shape, _F32),
        scratch_shapes=[pltpu.VMEM((tm, D_FF), _BF16)],
        compiler_params=_params(2),
        name="mlp",
    )(x, mod, mod, mod, w1, w2, final_gain.reshape(1, D_MODEL).astype(_F32))


def _split_bf16(x):
    hi = x.astype(_BF16).astype(_F32)
    rest = x - hi
    mid = rest.astype(_BF16).astype(_F32)
    lo = (rest - mid).astype(_BF16).astype(_F32)
    return hi, mid, lo


N_BIAS = 3


def _fox_kv_kernel(x_ref, gain_ref, wk_ref, wvt_ref, wf_ref, wft_ref, bcol_ref, brow_ref,
                   kaug_ref, vt_ref, frow_ref, ccol_scr, crow_scr):
    tm = KV_TILE

    @pl.when(pl.program_id(1) == 0)
    def _():
        ccol_scr[...] = jnp.zeros_like(ccol_scr)
        crow_scr[...] = jnp.zeros_like(crow_scr)

    ones = jnp.ones((V_ROWS - B_DH, tm), _BF16)
    lane = lax.broadcasted_iota(jnp.int32, (tm, LANES), 1)
    carry_col = ccol_scr[...]
    carry_row = crow_scr[:, 0:1]
    for blk in range(KV_STEP):
        h = (_rms(x_ref[blk * tm:(blk + 1) * tm, :]) * gain_ref[...]).astype(_BF16)
        vt = _dot_nt(wvt_ref[...], h).astype(_BF16)
        for hd in range(B_HEADS):
            vt_ref[blk, hd, 0:B_DH, :] = vt[hd * B_DH:(hd + 1) * B_DH, :]
            vt_ref[blk, hd, B_DH:V_ROWS, :] = ones
        k = _dot(h, wk_ref[...])
        fc = _segment_cumsum(_log_sigmoid(_dot(h, wf_ref[...]) + bcol_ref[...]), 0, tm)
        fc = fc + carry_col
        carry_col = fc[tm - 1:tm, :]
        fr = _segment_cumsum(_log_sigmoid(_dot_nt(wft_ref[...], h) + brow_ref[...]), 1, tm)
        fr = fr + carry_row
        carry_row = fr[:, tm - 1:tm]
        frow_ref[blk] = fr * LOG2E
        for hd in range(B_HEADS):
            pair, odd = divmod(hd, 2)
            own = (lane >= B_DH) if odd else (lane < B_DH)
            slot = lane if odd else lane - B_DH
            hi, mid, lo = _split_bf16(fc[:, hd:hd + 1] * LOG2E)
            bias = jnp.where(slot == 0, -hi, jnp.where(slot == 1, -mid, jnp.where(
                slot == 2, -lo, jnp.where(slot < 2 * N_BIAS, 1.0, 0.0))))
            kaug_ref[blk, hd] = jnp.where(
                own, k[:, pair * LANES:(pair + 1) * LANES], bias).astype(_BF16)
    ccol_scr[...] = carry_col
    crow_scr[...] = jnp.broadcast_to(carry_row, crow_scr.shape)


def _fox_kv(x, kv_gain, w_kv, fg_bias):
    batch, seq, _ = x.shape
    tm = KV_TILE
    w = w_kv.astype(_BF16)
    wk, wvt, wfg = w[:, :D_MODEL], w[:, D_MODEL:2 * D_MODEL].T, w[:, 2 * D_MODEL:]
    wf = jnp.pad(wfg, ((0, 0), (0, LANES - B_HEADS)))
    wft = wfg.T
    bias = fg_bias.astype(_F32)
    bcol = jnp.pad(bias, (0, LANES - B_HEADS)).reshape(1, LANES)
    brow = bias.reshape(B_HEADS, 1)
    ks = KV_STEP
    return pl.pallas_call(
        _fox_kv_kernel,
        grid=(batch, seq // (ks * tm)),
        in_specs=[
            pl.BlockSpec((None, ks * tm, D_MODEL), lambda b, i: (b, i, 0)),
            _const_spec((1, D_MODEL)),
            _const_spec((D_MODEL, D_MODEL)), _const_spec((D_MODEL, D_MODEL)),
            _const_spec((D_MODEL, LANES)), _const_spec((B_HEADS, D_MODEL)),
            _const_spec((1, LANES)), _const_spec((B_HEADS, 1)),
        ],
        out_specs=[
            pl.BlockSpec((None, ks, B_HEADS, tm, LANES), lambda b, i: (b, i, 0, 0, 0)),
            pl.BlockSpec((None, ks, B_HEADS, V_ROWS, tm), lambda b, i: (b, i, 0, 0, 0)),
            pl.BlockSpec((None, ks, B_HEADS, tm), lambda b, i: (b, i, 0, 0)),
        ],
        out_shape=[
            jax.ShapeDtypeStruct((batch, seq // tm, B_HEADS, tm, LANES), _BF16),
            jax.ShapeDtypeStruct((batch, seq // tm, B_HEADS, V_ROWS, tm), _BF16),
            jax.ShapeDtypeStruct((batch, seq // tm, B_HEADS, tm), _F32),
        ],
        scratch_shapes=[pltpu.VMEM((1, LANES), _F32), pltpu.VMEM((B_HEADS, LANES), _F32)],
        compiler_params=_params(2),
        name="fox_kv",
    )(x, kv_gain.reshape(1, D_MODEL).astype(_F32), wk, wvt, wf, wft, bcol, brow)


def _fox_attn_kernel(x_ref, sh_ref, sc_ref, g1_ref, wqt_ref, kaug_ref, vt_ref, frow_ref,
                     wout_ref, out_ref, qaug_scr, m_scr, acc_scr, o_scr):
    tq, tk = Q_TILE, KV_TILE
    qi = pl.program_id(1)
    x = x_ref[...]
    h = _modulate(x, sh_ref[...], sc_ref[...]).astype(_BF16)
    qt = (_dot_nt(wqt_ref[...], h) * (B_DH ** -0.5 * LOG2E)).astype(_BF16)
    ft = frow_ref[qi]
    sub = lax.broadcasted_iota(jnp.int32, (B_DH, tq), 0)
    for hd in range(B_HEADS):
        hi, mid, lo = _split_bf16(ft[hd:hd + 1, :])
        bias = jnp.where(sub < N_BIAS, 1.0, jnp.where(sub == N_BIAS, hi, jnp.where(
            sub == N_BIAS + 1, mid, jnp.where(sub == N_BIAS + 2, lo, 0.0)))).astype(_BF16)
        qh = qt[hd * B_DH:(hd + 1) * B_DH, :]
        lo_half, hi_half = (bias, qh) if hd % 2 else (qh, bias)
        qaug_scr[hd, 0:B_DH, :] = lo_half
        qaug_scr[hd, B_DH:2 * B_DH, :] = hi_half
    m_scr[...] = jnp.full(m_scr.shape, -jnp.inf, _F32)
    acc_scr[...] = jnp.zeros_like(acc_scr)

    def kv_block(j, diagonal):
        if diagonal:
            keep = (lax.broadcasted_iota(jnp.int32, (tk, tq), 1)
                    >= lax.broadcasted_iota(jnp.int32, (tk, tq), 0))
        def scores(g):
            return [_dot(kaug_ref[j, hd], qaug_scr[hd])
                    for hd in range(g * HEAD_GROUP, (g + 1) * HEAD_GROUP)]

        n_groups = B_HEADS // HEAD_GROUP
        s_next = scores(0)
        for g in range(n_groups):
            heads = range(g * HEAD_GROUP, (g + 1) * HEAD_GROUP)
            s_cur = s_next
            if g + 1 < n_groups:
                s_next = scores(g + 1)
            if diagonal:
                s_cur = [jnp.where(keep, s, -jnp.inf) for s in s_cur]
            m_old = [m_scr[hd] for hd in heads]
            m_new = [jnp.maximum(mo, jnp.max(s, axis=0, keepdims=True))
                     for mo, s in zip(m_old, s_cur)]
            alpha = [jnp.exp2(mo - mn) for mo, mn in zip(m_old, m_new)]
            p = [jnp.exp2(s - mn) for s, mn in zip(s_cur, m_new)]
            for i, hd in enumerate(heads):
                m_scr[hd] = m_new[i]
                acc_scr[hd] = alpha[i] * acc_scr[hd] + _dot(vt_ref[j, hd], p[i].astype(_BF16))

    def body(j, carry):
        kv_block(j, False)
        return carry

    lax.fori_loop(0, qi, body, 0)
    kv_block(qi, True)

    for hd in range(B_HEADS):
        o_scr[hd * B_DH:(hd + 1) * B_DH, :] = (
            acc_scr[hd, 0:B_DH, :] / acc_scr[hd, B_DH:B_DH + 1, :]).astype(_BF16)
    out_ref[...] = x + g1_ref[...] * _dot_tn(o_scr[...], wout_ref[...])


def _fox_attn(x, mod, layer, w_q, w_out, kaug, vt, frow):
    batch, seq, _ = x.shape
    tq, tk = Q_TILE, KV_TILE
    nkv = seq // tk
    tok = pl.BlockSpec((None, tq, D_MODEL), lambda b, i: (b, i, 0))
    return pl.pallas_call(
        _fox_attn_kernel,
        grid=(batch, seq // tq),
        in_specs=[
            tok, _mod_spec(layer, 0), _mod_spec(layer, 1), _mod_spec(layer, 2),
            _const_spec((D_MODEL, D_MODEL)),
            pl.BlockSpec((None, nkv, B_HEADS, tk, LANES), lambda b, i: (b, 0, 0, 0, 0)),
            pl.BlockSpec((None, nkv, B_HEADS, V_ROWS, tk), lambda b, i: (b, 0, 0, 0, 0)),
            pl.BlockSpec((None, nkv, B_HEADS, tk), lambda b, i: (b, 0, 0, 0)),
            _const_spec((D_MODEL, D_MODEL)),
        ],
        out_specs=tok,
        out_shape=jax.ShapeDtypeStruct(x.shape, _F32),
        scratch_shapes=[
            pltpu.VMEM((B_HEADS, 2 * B_DH, tq), _BF16),
            pltpu.VMEM((B_HEADS, 1, tq), _F32),
            pltpu.VMEM((B_HEADS, V_ROWS, tq), _F32), pltpu.VMEM((D_MODEL, tq), _BF16),
        ],
        compiler_params=_params(2),
        name="fox_attn",
    )(x, mod, mod, mod, w_q.T.astype(_BF16), kaug, vt, frow, w_out.astype(_BF16))


def kernel(x, c, ada_w, ada_b, a_w_in, a_b_i, a_b_f, a_head_gain, a_w_out, kv_gain, b_w_kv,
           b_fg_bias, b_w_q, b_w_out, mlp_w1, mlp_w2, final_gain):
    batch, seq, d = x.shape
    assert d == D_MODEL and seq % max(PROJ_TILE, MLP_TILE, MLSTM_CHUNK, Q_TILE, KV_TILE) == 0
    assert PROJ_TILE % MLSTM_CHUNK == 0 and Q_TILE == KV_TILE
    mod = _adaln_table(c, ada_w, ada_b).reshape(DEPTH, 6, batch, 1, D_MODEL)
    w1_all, w2_all = mlp_w1.astype(_BF16), mlp_w2.astype(_BF16)
    shared = None
    for l in range(DEPTH):
        if l < N_A_LAYERS:
            qt, k, vt, ot, gcol, grow = _mlstm_proj(x, mod, l, a_w_in[l], a_b_i[l], a_b_f[l])
            x = _mlstm_core(x, mod, l, qt, k, vt, ot, gcol, grow, a_head_gain[l], a_w_out[l])
        else:
            if shared is None:
                shared = _fox_kv(x, kv_gain, b_w_kv, b_fg_bias)
            j = l - N_A_LAYERS
            x = _fox_attn(x, mod, l, b_w_q[j], b_w_out[j], *shared)
        x = _mlp(x, mod, l, w1_all, w2_all, final_gain, final_norm=(l == DEPTH - 1))
    return x
```

```python
import functools

import jax
import jax.numpy as jnp
from jax import lax
from jax.experimental import pallas as pl
from jax.experimental.pallas import tpu as pltpu

D_MODEL = 1024
DEPTH = 4
N_A_LAYERS = DEPTH // 2
A_HEADS = 4
A_DV = D_MODEL // A_HEADS
A_DQK = A_DV // 2
A_QK_W = A_HEADS * A_DQK
A_V_W = A_HEADS * A_DV
B_HEADS = 16
B_DH = D_MODEL // B_HEADS
D_FF = 4 * D_MODEL
EPS = 1e-6
LOG2E = 1.4426950408889634
AV_ROWS = A_DV + 16
V_ROWS = B_DH + 16

LANES = 128
MLSTM_CHUNK = 256
KV_TILE = 256
KV_STEP = 4
Q_TILE = 256
HEAD_GROUP = 8
MLP_TILE = 1024
FF_CHUNK = 512
VMEM_LIMIT = 56 * 1024 * 1024

_BF16 = jnp.bfloat16
_F32 = jnp.float32


def _dot(a, b):
    return jnp.dot(a, b, preferred_element_type=_F32)


def _dot_nt(a, b):
    return lax.dot_general(a, b, (((1,), (1,)), ((), ())), preferred_element_type=_F32)


def _dot_tn(a, b):
    return lax.dot_general(a, b, (((0,), (0,)), ((), ())), preferred_element_type=_F32)


def _rms(x):
    return x * lax.rsqrt(jnp.mean(x * x, axis=-1, keepdims=True) + EPS)


def _modulate(x, shift, scale):
    return _rms(x) * (1.0 + scale) + shift


def _log_sigmoid(z):
    return jnp.minimum(z, 0.0) - jnp.log1p(jnp.exp(-jnp.abs(z)))


def _segment_cumsum(x, axis, seg):
    pos = lax.broadcasted_iota(jnp.int32, x.shape, axis) & (seg - 1)
    k = 1
    while k < seg:
        x = x + jnp.where(pos >= k, pltpu.roll(x, k, axis), 0.0)
        k *= 2
    return x


def _params(n_grid):
    return pltpu.CompilerParams(dimension_semantics=("arbitrary",) * n_grid,
                                vmem_limit_bytes=VMEM_LIMIT)


def _const_spec(shape):
    return pl.BlockSpec(shape, lambda *_: (0,) * len(shape), pipeline_mode=pl.Buffered(1))


def _mod_spec(layer, slot):
    return pl.BlockSpec((None, None, None, 1, D_MODEL), lambda b, i: (layer, slot, b, 0, 0))


def _adaln_kernel(c_ref, w_ref, b_ref, o_ref):
    c = c_ref[...]
    cond = (c * jax.nn.sigmoid(c)).astype(_BF16)
    o_ref[...] = _dot(cond, w_ref[...].astype(_BF16)) + b_ref[...]


def _adaln_table(c, ada_w, ada_b):
    batch = c.shape[0]
    return pl.pallas_call(
        _adaln_kernel,
        grid=(DEPTH, 6),
        in_specs=[
            pl.BlockSpec((batch, D_MODEL), lambda l, j: (0, 0)),
            pl.BlockSpec((None, D_MODEL, D_MODEL), lambda l, j: (l, 0, j)),
            pl.BlockSpec((None, None, 1, D_MODEL), lambda l, j: (l, j, 0, 0)),
        ],
        out_specs=pl.BlockSpec((None, None, batch, D_MODEL), lambda l, j: (l, j, 0, 0)),
        out_shape=jax.ShapeDtypeStruct((DEPTH, 6, batch, D_MODEL), _F32),
        compiler_params=_params(2),
        name="adaln_table",
    )(c, ada_w, ada_b.reshape(DEPTH, 6, 1, D_MODEL))


def _mlstm_project_qkg(h, w, dst):
    wqt_ref, wk_ref, _, _, wg_ref, wgt_ref, bcol_ref, brow_ref = w
    qt_s, k_s, _, _, gcol_s, grow_s = dst
    ct = MLSTM_CHUNK
    qt_s[...] = (_dot_nt(wqt_ref[...], h) * (A_DQK ** -0.5)).astype(_BF16)
    k_s[...] = _dot(h, wk_ref[...]).astype(_BF16)
    z = _dot(h, wg_ref[...]) + bcol_ref[...]
    bc = _segment_cumsum(_log_sigmoid(z), 0, ct)
    gcol_s[...] = (z - pltpu.roll(bc, LANES - A_HEADS, 1)) * LOG2E
    zr = _dot_nt(wgt_ref[...], h) + brow_ref[...]
    br = _segment_cumsum(_log_sigmoid(zr), 1, ct)
    sub = lax.broadcasted_iota(jnp.int32, zr.shape, 0)
    grow_s[...] = jnp.where(sub < A_HEADS, zr, br) * LOG2E


def _mlstm_project_vo(h, w, dst):
    _, _, wvt_ref, wot_ref, _, _, _, _ = w
    _, _, vt_s, ot_s, _, _ = dst
    vt = _dot_nt(wvt_ref[...], h).astype(_BF16)
    for hd in range(A_HEADS):
        vt_s[hd, 0:A_DV, :] = vt[hd * A_DV:(hd + 1) * A_DV, :]
    ot_s[...] = _dot_nt(wot_ref[...], h)


def _mlstm_layer_kernel(xp_ref, xc_ref, sh_ref, sc_ref, g1_ref, wqt_ref, wk_ref, wvt_ref, wot_ref,
                        wg_ref, wgt_ref, bcol_ref, brow_ref, gain_ref, wout_ref, out_ref,
                        qt_a, k_a, vt_a, ot_a, gcol_a, grow_a, qt_b, k_b, vt_b, ot_b, gcol_b, grow_b,
                        cn_scr, m_scr, z_scr, *, n_chunks):
    L = MLSTM_CHUNK
    heads = range(A_HEADS)
    g = pl.program_id(0)
    weights = (wqt_ref, wk_ref, wvt_ref, wot_ref, wg_ref, wgt_ref, bcol_ref, brow_ref)
    set_a = (qt_a, k_a, vt_a, ot_a, gcol_a, grow_a)
    set_b = (qt_b, k_b, vt_b, ot_b, gcol_b, grow_b)

    @pl.when(g == 0)
    def _():
        for ref in set_b:
            ref[...] = jnp.zeros_like(ref)
        for vt_s in (vt_a, vt_b):
            vt_s[:, A_DV:AV_ROWS, :] = jnp.ones((A_HEADS, AV_ROWS - A_DV, L), _BF16)
        cn_scr[...] = jnp.zeros_like(cn_scr)
        m_scr[...] = jnp.zeros_like(m_scr)

    fresh = lax.rem(jnp.maximum(g - 1, 0), n_chunks) == 0
    causal = (lax.broadcasted_iota(jnp.int32, (L, L), 0)
              <= lax.broadcasted_iota(jnp.int32, (L, L), 1))

    def step(dst, src):
        qt_s, k_s, vt_s, ot_s, gcol_s, grow_s = src
        gcol = gcol_s[...]
        grow = grow_s[...]
        kh = [k_s[:, h * A_DQK:(h + 1) * A_DQK] for h in heads]
        qt = [qt_s[h * A_DQK:(h + 1) * A_DQK, :] for h in heads]
        cn_prev = [jnp.where(fresh, 0.0, cn_scr[h]) for h in heads]
        m_prev = [jnp.where(fresh, 0.0, m_scr[h][:, 0:1]) for h in heads]
        st = [_dot(kh[h], qt[h]) for h in heads]
        inter_mm = [_dot(cn_prev[h].astype(_BF16), qt[h]) for h in heads]

        i_row = [grow[h:h + 1, :] for h in heads]
        b_row = [grow[A_HEADS + h:A_HEADS + h + 1, :] for h in heads]
        d = [jnp.where(causal, gcol[:, h:h + 1] + b_row[h], -jnp.inf) for h in heads]
        inter = [b_row[h] + m_prev[h] for h in heads]
        m_t = [jnp.maximum(inter[h], jnp.max(d[h], axis=0, keepdims=True)) for h in heads]
        a = [(jnp.exp2(d[h] - m_t[h]) * st[h]).astype(_BF16) for h in heads]
        w_inter = [jnp.exp2(inter[h] - m_t[h]) for h in heads]

        hp = _modulate(xp_ref[...], sh_ref[...], sc_ref[...]).astype(_BF16)
        _mlstm_project_qkg(hp, weights, dst)
        nd = [w_inter[h] * inter_mm[h] + _dot(vt_s[h], a[h]) for h in heads]
        _mlstm_project_vo(hp, weights, dst)

        b_last = [b_row[h][:, L - 1:L] for h in heads]
        dl = [b_last[h] - b_row[h] + i_row[h] for h in heads]
        m_new = [jnp.maximum(b_last[h] + m_prev[h], jnp.max(dl[h], axis=1, keepdims=True))
                 for h in heads]
        for h in heads:
            vw = (vt_s[h].astype(_F32) * jnp.exp2(dl[h] - m_new[h])).astype(_BF16)
            decay = jnp.exp2(b_last[h] + m_prev[h] - m_new[h])
            cn_scr[h] = decay * cn_prev[h] + _dot(vw, kh[h])
            m_scr[h] = jnp.broadcast_to(m_new[h], (1, LANES))
        for h in heads:
            rows = slice(h * A_DV, (h + 1) * A_DV)
            den = nd[h][A_DV:A_DV + 1, :]
            ht = nd[h][0:A_DV, :] / jnp.maximum(jnp.abs(den), jnp.exp2(-m_t[h]))
            ht = (ht * lax.rsqrt(jnp.mean(ht * ht, axis=0, keepdims=True) + EPS)
                  * gain_ref[rows, :])
            z_scr[rows, :] = (jax.nn.sigmoid(ot_s[rows, :]) * ht).astype(_BF16)
        out_ref[...] = xc_ref[...] + g1_ref[...] * _dot_tn(z_scr[...], wout_ref[...])

    @pl.when(lax.rem(g, 2) == 0)
    def _():
        step(set_a, set_b)

    @pl.when(lax.rem(g, 2) == 1)
    def _():
        step(set_b, set_a)


def _mlstm_layer(x, mod, layer, w_in, b_i, b_f, head_gain, w_out):
    batch, seq, _ = x.shape
    L = MLSTM_CHUNK
    nc = seq // L
    total = batch * nc
    w = w_in.astype(_BF16)
    wqt, wk = w[:, :A_QK_W].T, w[:, A_QK_W:2 * A_QK_W]
    wvt = w[:, 2 * A_QK_W:2 * A_QK_W + A_V_W].T
    wot = w[:, 2 * A_QK_W + A_V_W:2 * A_QK_W + 2 * A_V_W].T
    wgate = w[:, 2 * A_QK_W + 2 * A_V_W:]
    wg = jnp.pad(wgate, ((0, 0), (0, LANES - 2 * A_HEADS)))
    bias = jnp.concatenate([b_i, b_f]).astype(_F32)
    bcol = jnp.pad(bias, (0, LANES - 2 * A_HEADS)).reshape(1, LANES)
    brow = bias.reshape(2 * A_HEADS, 1)
    gain = jnp.broadcast_to(head_gain.reshape(A_V_W, 1).astype(_F32), (A_V_W, L))

    proj_chunk = lambda g: jnp.minimum(g, total - 1)
    rec_chunk = lambda g: jnp.maximum(g - 1, 0)
    tok = lambda chunk: pl.BlockSpec(
        (None, L, D_MODEL), lambda g: (chunk(g) // nc, lax.rem(chunk(g), nc), 0))
    modrow = lambda slot, chunk: pl.BlockSpec(
        (None, None, None, 1, D_MODEL), lambda g: (layer, slot, chunk(g) // nc, 0, 0))
    scratch_set = [
        pltpu.VMEM((A_QK_W, L), _BF16), pltpu.VMEM((L, A_QK_W), _BF16),
        pltpu.VMEM((A_HEADS, AV_ROWS, L), _BF16), pltpu.VMEM((A_V_W, L), _F32),
        pltpu.VMEM((L, LANES), _F32), pltpu.VMEM((2 * A_HEADS, L), _F32),
    ]
    return pl.pallas_call(
        functools.partial(_mlstm_layer_kernel, n_chunks=nc),
        grid=(total + 1,),
        in_specs=[
            tok(proj_chunk), tok(rec_chunk),
            modrow(0, proj_chunk), modrow(1, proj_chunk), modrow(2, rec_chunk),
            _const_spec((A_QK_W, D_MODEL)), _const_spec((D_MODEL, A_QK_W)),
            _const_spec((A_V_W, D_MODEL)), _const_spec((A_V_W, D_MODEL)),
            _const_spec((D_MODEL, LANES)), _const_spec((2 * A_HEADS, D_MODEL)),
            _const_spec((1, LANES)), _const_spec((2 * A_HEADS, 1)),
            _const_spec((A_V_W, L)), _const_spec((A_V_W, D_MODEL)),
        ],
        out_specs=tok(rec_chunk),
        out_shape=jax.ShapeDtypeStruct(x.shape, _F32),
        scratch_shapes=scratch_set + scratch_set + [
            pltpu.VMEM((A_HEADS, AV_ROWS, A_DQK), _F32),
            pltpu.VMEM((A_HEADS, 1, LANES), _F32),
            pltpu.VMEM((A_V_W, L), _BF16),
        ],
        compiler_params=_params(1),
        name="mlstm_layer",
    )(x, x, mod, mod, mod, wqt, wk, wvt, wot, wg, wgate.T, bcol, brow, gain,
      w_out.astype(_BF16))


def _mlp_kernel(x_ref, sh_ref, sc_ref, g_ref, w1_ref, w2_ref, fgain_ref, out_ref, u_scr, *,
                final_norm):
    x = x_ref[...]
    h = _modulate(x, sh_ref[...], sc_ref[...]).astype(_BF16)
    for c in range(D_FF // FF_CHUNK):
        u = jnp.maximum(_dot(h, w1_ref[:, c * FF_CHUNK:(c + 1) * FF_CHUNK]), 0.0)
        u_scr[:, c * FF_CHUNK:(c + 1) * FF_CHUNK] = (u * u).astype(_BF16)
    y = x + g_ref[...] * _dot(u_scr[...], w2_ref[...])
    if final_norm:
        y = _rms(y) * fgain_ref[...]
    out_ref[...] = y


def _layer_spec(layer, rows, cols):
    return pl.BlockSpec((None, rows, cols), lambda *_: (layer, 0, 0),
                        pipeline_mode=pl.Buffered(1))


def _mlp(x, mod, layer, w1, w2, final_gain, final_norm):
    batch, seq, _ = x.shape
    tm = MLP_TILE
    tok = pl.BlockSpec((None, tm, D_MODEL), lambda b, i: (b, i, 0))
    return pl.pallas_call(
        functools.partial(_mlp_kernel, final_norm=final_norm),
        grid=(batch, seq // tm),
        in_specs=[
            tok, _mod_spec(layer, 3), _mod_spec(layer, 4), _mod_spec(layer, 5),
            _layer_spec(layer, D_MODEL, D_FF), _layer_spec(layer, D_FF, D_MODEL),
            _const_spec((1, D_MODEL)),
        ],
        out_specs=tok,
        out_shape=jax.ShapeDtypeStruct(x.shape, _F32),
        scratch_shapes=[pltpu.VMEM((tm, D_FF), _BF16)],
        compiler_params=_params(2),
        name="mlp",
    )(x, mod, mod, mod, w1, w2, final_gain.reshape(1, D_MODEL).astype(_F32))


def _split_bf16(x):
    hi = x.astype(_BF16).astype(_F32)
    rest = x - hi
    mid = rest.astype(_BF16).astype(_F32)
    lo = (rest - mid).astype(_BF16).astype(_F32)
    return hi, mid, lo


N_BIAS = 3


def _fox_kv_kernel(x_ref, gain_ref, wk_ref, wvt_ref, wf_ref, wft_ref, bcol_ref, brow_ref,
                   kaug_ref, vt_ref, frow_ref, ccol_scr, crow_scr):
    tm = KV_TILE

    @pl.when(pl.program_id(1) == 0)
    def _():
        ccol_scr[...] = jnp.zeros_like(ccol_scr)
        crow_scr[...] = jnp.zeros_like(crow_scr)

    ones = jnp.ones((V_ROWS - B_DH, tm), _BF16)
    lane = lax.broadcasted_iota(jnp.int32, (tm, LANES), 1)
    carry_col = ccol_scr[...]
    carry_row = crow_scr[:, 0:1]
    for blk in range(KV_STEP):
        h = (_rms(x_ref[blk * tm:(blk + 1) * tm, :]) * gain_ref[...]).astype(_BF16)
        vt = _dot_nt(wvt_ref[...], h).astype(_BF16)
        for hd in range(B_HEADS):
            vt_ref[blk, hd, 0:B_DH, :] = vt[hd * B_DH:(hd + 1) * B_DH, :]
            vt_ref[blk, hd, B_DH:V_ROWS, :] = ones
        k = _dot(h, wk_ref[...])
        fc = _segment_cumsum(_log_sigmoid(_dot(h, wf_ref[...]) + bcol_ref[...]), 0, tm)
        fc = fc + carry_col
        carry_col = fc[tm - 1:tm, :]
        fr = _segment_cumsum(_log_sigmoid(_dot_nt(wft_ref[...], h) + brow_ref[...]), 1, tm)
        fr = fr + carry_row
        carry_row = fr[:, tm - 1:tm]
        frow_ref[blk] = fr * LOG2E
        for hd in range(B_HEADS):
            pair, odd = divmod(hd, 2)
            own = (lane >= B_DH) if odd else (lane < B_DH)
            slot = lane if odd else lane - B_DH
            hi, mid, lo = _split_bf16(fc[:, hd:hd + 1] * LOG2E)
            bias = jnp.where(slot == 0, -hi, jnp.where(slot == 1, -mid, jnp.where(
                slot == 2, -lo, jnp.where(slot < 2 * N_BIAS, 1.0, 0.0))))
            kaug_ref[blk, hd] = jnp.where(
                own, k[:, pair * LANES:(pair + 1) * LANES], bias).astype(_BF16)
    ccol_scr[...] = carry_col
    crow_scr[...] = jnp.broadcast_to(carry_row, crow_scr.shape)


def _fox_kv(x, kv_gain, w_kv, fg_bias):
    batch, seq, _ = x.shape
    tm = KV_TILE
    w = w_kv.astype(_BF16)
    wk, wvt, wfg = w[:, :D_MODEL], w[:, D_MODEL:2 * D_MODEL].T, w[:, 2 * D_MODEL:]
    wf = jnp.pad(wfg, ((0, 0), (0, LANES - B_HEADS)))
    wft = wfg.T
    bias = fg_bias.astype(_F32)
    bcol = jnp.pad(bias, (0, LANES - B_HEADS)).reshape(1, LANES)
    brow = bias.reshape(B_HEADS, 1)
    ks = KV_STEP
    return pl.pallas_call(
        _fox_kv_kernel,
        grid=(batch, seq // (ks * tm)),
        in_specs=[
            pl.BlockSpec((None, ks * tm, D_MODEL), lambda b, i: (b, i, 0)),
            _const_spec((1, D_MODEL)),
            _const_spec((D_MODEL, D_MODEL)), _const_spec((D_MODEL, D_MODEL)),
            _const_spec((D_MODEL, LANES)), _const_spec((B_HEADS, D_MODEL)),
            _const_spec((1, LANES)), _const_spec((B_HEADS, 1)),
        ],
        out_specs=[
            pl.BlockSpec((None, ks, B_HEADS, tm, LANES), lambda b, i: (b, i, 0, 0, 0)),
            pl.BlockSpec((None, ks, B_HEADS, V_ROWS, tm), lambda b, i: (b, i, 0, 0, 0)),
            pl.BlockSpec((None, ks, B_HEADS, tm), lambda b, i: (b, i, 0, 0)),
        ],
        out_shape=[
            jax.ShapeDtypeStruct((batch, seq // tm, B_HEADS, tm, LANES), _BF16),
            jax.ShapeDtypeStruct((batch, seq // tm, B_HEADS, V_ROWS, tm), _BF16),
            jax.ShapeDtypeStruct((batch, seq // tm, B_HEADS, tm), _F32),
        ],
        scratch_shapes=[pltpu.VMEM((1, LANES), _F32), pltpu.VMEM((B_HEADS, LANES), _F32)],
        compiler_params=_params(2),
        name="fox_kv",
    )(x, kv_gain.reshape(1, D_MODEL).astype(_F32), wk, wvt, wf, wft, bcol, brow)


def _fox_attn_kernel(x_ref, sh_ref, sc_ref, g1_ref, wqt_ref, kaug_ref, vt_ref, frow_ref,
                     wout_ref, out_ref, qaug_scr, m_scr, acc_scr, o_scr):
    tq, tk = Q_TILE, KV_TILE
    qi = pl.program_id(1)
    x = x_ref[...]
    h = _modulate(x, sh_ref[...], sc_ref[...]).astype(_BF16)
    qt = (_dot_nt(wqt_ref[...], h) * (B_DH ** -0.5 * LOG2E)).astype(_BF16)
    ft = frow_ref[qi]
    sub = lax.broadcasted_iota(jnp.int32, (B_DH, tq), 0)
    for hd in range(B_HEADS):
        hi, mid, lo = _split_bf16(ft[hd:hd + 1, :])
        bias = jnp.where(sub < N_BIAS, 1.0, jnp.where(sub == N_BIAS, hi, jnp.where(
            sub == N_BIAS + 1, mid, jnp.where(sub == N_BIAS + 2, lo, 0.0)))).astype(_BF16)
        qh = qt[hd * B_DH:(hd + 1) * B_DH, :]
        lo_half, hi_half = (bias, qh) if hd % 2 else (qh, bias)
        qaug_scr[hd, 0:B_DH, :] = lo_half
        qaug_scr[hd, B_DH:2 * B_DH, :] = hi_half
    m_scr[...] = jnp.full(m_scr.shape, -jnp.inf, _F32)
    acc_scr[...] = jnp.zeros_like(acc_scr)

    def kv_block(j, diagonal):
        if diagonal:
            keep = (lax.broadcasted_iota(jnp.int32, (tk, tq), 1)
                    >= lax.broadcasted_iota(jnp.int32, (tk, tq), 0))
        def scores(g):
            return [_dot(kaug_ref[j, hd], qaug_scr[hd])
                    for hd in range(g * HEAD_GROUP, (g + 1) * HEAD_GROUP)]

        n_groups = B_HEADS // HEAD_GROUP
        s_next = scores(0)
        for g in range(n_groups):
            heads = range(g * HEAD_GROUP, (g + 1) * HEAD_GROUP)
            s_cur = s_next
            if g + 1 < n_groups:
                s_next = scores(g + 1)
            if diagonal:
                s_cur = [jnp.where(keep, s, -jnp.inf) for s in s_cur]
            m_old = [m_scr[hd] for hd in heads]
            m_new = [jnp.maximum(mo, jnp.max(s, axis=0, keepdims=True))
                     for mo, s in zip(m_old, s_cur)]
            alpha = [jnp.exp2(mo - mn) for mo, mn in zip(m_old, m_new)]
            p = [jnp.exp2(s - mn) for s, mn in zip(s_cur, m_new)]
            for i, hd in enumerate(heads):
                m_scr[hd] = m_new[i]
                acc_scr[hd] = alpha[i] * acc_scr[hd] + _dot(vt_ref[j, hd], p[i].astype(_BF16))

    def body(j, carry):
        kv_block(j, False)
        return carry

    lax.fori_loop(0, qi, body, 0)
    kv_block(qi, True)

    for hd in range(B_HEADS):
        o_scr[hd * B_DH:(hd + 1) * B_DH, :] = (
            acc_scr[hd, 0:B_DH, :] / acc_scr[hd, B_DH:B_DH + 1, :]).astype(_BF16)
    out_ref[...] = x + g1_ref[...] * _dot_tn(o_scr[...], wout_ref[...])


def _fox_attn(x, mod, layer, w_q, w_out, kaug, vt, frow):
    batch, seq, _ = x.shape
    tq, tk = Q_TILE, KV_TILE
    nkv = seq // tk
    tok = pl.BlockSpec((None, tq, D_MODEL), lambda b, i: (b, i, 0))
    return pl.pallas_call(
        _fox_attn_kernel,
        grid=(batch, seq // tq),
        in_specs=[
            tok, _mod_spec(layer, 0), _mod_spec(layer, 1), _mod_spec(layer, 2),
            _const_spec((D_MODEL, D_MODEL)),
            pl.BlockSpec((None, nkv, B_HEADS, tk, LANES), lambda b, i: (b, 0, 0, 0, 0)),
            pl.BlockSpec((None, nkv, B_HEADS, V_ROWS, tk), lambda b, i: (b, 0, 0, 0, 0)),
            pl.BlockSpec((None, nkv, B_HEADS, tk), lambda b, i: (b, 0, 0, 0)),
            _const_spec((D_MODEL, D_MODEL)),
        ],
        out_specs=tok,
        out_shape=jax.ShapeDtypeStruct(x.shape, _F32),
        scratch_shapes=[
            pltpu.VMEM((B_HEADS, 2 * B_DH, tq), _BF16),
            pltpu.VMEM((B_HEADS, 1, tq), _F32),
            pltpu.VMEM((B_HEADS, V_ROWS, tq), _F32), pltpu.VMEM((D_MODEL, tq), _BF16),
        ],
        compiler_params=_params(2),
        name="fox_attn",
    )(x, mod, mod, mod, w_q.T.astype(_BF16), kaug, vt, frow, w_out.astype(_BF16))


def kernel(x, c, ada_w, ada_b, a_w_in, a_b_i, a_b_f, a_head_gain, a_w_out, kv_gain, b_w_kv,
           b_fg_bias, b_w_q, b_w_out, mlp_w1, mlp_w2, final_gain):
    batch, seq, d = x.shape
    assert d == D_MODEL and Q_TILE == KV_TILE
    assert all(seq % t == 0 for t in (MLP_TILE, MLSTM_CHUNK, Q_TILE, KV_STEP * KV_TILE))
    mod = _adaln_table(c, ada_w, ada_b).reshape(DEPTH, 6, batch, 1, D_MODEL)
    w1_all, w2_all = mlp_w1.astype(_BF16), mlp_w2.astype(_BF16)
    shared = None
    for l in range(DEPTH):
        if l < N_A_LAYERS:
            x = _mlstm_layer(x, mod, l, a_w_in[l], a_b_i[l], a_b_f[l], a_head_gain[l], a_w_out[l])
        else:
            if shared is None:
                shared = _fox_kv(x, kv_gain, b_w_kv, b_fg_bias)
            j = l - N_A_LAYERS
            x = _fox_attn(x, mod, l, b_w_q[j], b_w_out[j], *shared)
        x = _mlp(x, mod, l, w1_all, w2_all, final_gain, final_norm=(l == DEPTH - 1))
    return x
```

```python
import functools

import jax
import jax.numpy as jnp
import numpy as np
from jax import lax
from jax.experimental import pallas as pl
from jax.experimental.pallas import tpu as pltpu

D_MODEL = 1024
DEPTH = 4
N_A_LAYERS = DEPTH // 2
A_HEADS = 4
A_DV = D_MODEL // A_HEADS
A_DQK = A_DV // 2
A_QK_W = A_HEADS * A_DQK
A_V_W = A_HEADS * A_DV
B_HEADS = 16
B_DH = D_MODEL // B_HEADS
D_FF = 4 * D_MODEL
EPS = 1e-6
LOG2E = 1.4426950408889634
AV_ROWS = A_DV + 16
V_ROWS = B_DH + 16

LANES = 128
MLSTM_CHUNK = 256
KV_TILE = 256
KV_STEP = 4
Q_TILE = 256
HEAD_GROUP = 8
Q_PROJ_HEADS = 4
MLP_TILE = 1024
FF_CHUNK = 512
VMEM_LIMIT = 56 * 1024 * 1024

_BF16 = jnp.bfloat16
_F32 = jnp.float32


def _dot(a, b):
    return jnp.dot(a, b, preferred_element_type=_F32)


def _dot_nt(a, b):
    return lax.dot_general(a, b, (((1,), (1,)), ((), ())), preferred_element_type=_F32)


def _dot_tn(a, b):
    return lax.dot_general(a, b, (((0,), (0,)), ((), ())), preferred_element_type=_F32)


def _rms(x):
    return x * lax.rsqrt(jnp.mean(x * x, axis=-1, keepdims=True) + EPS)


def _modulate(x, shift, scale):
    return _rms(x) * (1.0 + scale) + shift


def _log_sigmoid(z):
    return jnp.minimum(z, 0.0) - jnp.log1p(jnp.exp(-jnp.abs(z)))


def _segment_cumsum(x, axis, seg):
    pos = lax.broadcasted_iota(jnp.int32, x.shape, axis) & (seg - 1)
    k = 1
    while k < seg:
        x = x + jnp.where(pos >= k, pltpu.roll(x, k, axis), 0.0)
        k *= 2
    return x


def _params(n_grid):
    return pltpu.CompilerParams(dimension_semantics=("arbitrary",) * n_grid,
                                vmem_limit_bytes=VMEM_LIMIT)


def _const_spec(shape):
    return pl.BlockSpec(shape, lambda *_: (0,) * len(shape), pipeline_mode=pl.Buffered(1))


def _mod_spec(layer, slot):
    return pl.BlockSpec((None, None, None, 1, D_MODEL), lambda b, i: (layer, slot, b, 0, 0))


def _adaln_kernel(c_ref, w_ref, b_ref, o_ref):
    c = c_ref[...]
    cond = (c * jax.nn.sigmoid(c)).astype(_BF16)
    o_ref[...] = _dot(cond, w_ref[...].astype(_BF16)) + b_ref[...]


def _adaln_table(c, ada_w, ada_b):
    batch = c.shape[0]
    return pl.pallas_call(
        _adaln_kernel,
        grid=(DEPTH, 6),
        in_specs=[
            pl.BlockSpec((batch, D_MODEL), lambda l, j: (0, 0)),
            pl.BlockSpec((None, D_MODEL, D_MODEL), lambda l, j: (l, 0, j)),
            pl.BlockSpec((None, None, 1, D_MODEL), lambda l, j: (l, j, 0, 0)),
        ],
        out_specs=pl.BlockSpec((None, None, batch, D_MODEL), lambda l, j: (l, j, 0, 0)),
        out_shape=jax.ShapeDtypeStruct((DEPTH, 6, batch, D_MODEL), _F32),
        compiler_params=_params(2),
        name="adaln_table",
    )(c, ada_w, ada_b.reshape(DEPTH, 6, 1, D_MODEL))


def _mlstm_project_qkg(h, w, dst):
    wqt_ref, wk_ref, _, _, wg_ref, wgt_ref, bcol_ref, brow_ref = w
    qt_s, k_s, _, _, gcol_s, grow_s = dst
    ct = MLSTM_CHUNK
    qt_s[...] = (_dot_nt(wqt_ref[...], h) * (A_DQK ** -0.5)).astype(_BF16)
    k_s[...] = _dot(h, wk_ref[...]).astype(_BF16)
    z = _dot(h, wg_ref[...]) + bcol_ref[...]
    bc = _segment_cumsum(_log_sigmoid(z), 0, ct)
    gcol_s[...] = (z - pltpu.roll(bc, LANES - A_HEADS, 1)) * LOG2E
    zr = _dot_nt(wgt_ref[...], h) + brow_ref[...]
    br = _segment_cumsum(_log_sigmoid(zr), 1, ct)
    sub = lax.broadcasted_iota(jnp.int32, zr.shape, 0)
    grow_s[...] = jnp.where(sub < A_HEADS, zr, br) * LOG2E


def _mlstm_project_v(h, w, dst):
    vt = _dot_nt(w[2][...], h).astype(_BF16)
    for hd in range(A_HEADS):
        dst[2][hd, 0:A_DV, :] = vt[hd * A_DV:(hd + 1) * A_DV, :]


def _mlstm_project_o(h, w, dst):
    dst[3][...] = _dot_nt(w[3][...], h)


def _mlstm_layer_kernel(xp_ref, xc_ref, sh_ref, sc_ref, g1_ref, wqt_ref, wk_ref, wvt_ref, wot_ref,
                        wg_ref, wgt_ref, bcol_ref, brow_ref, gain_ref, wout_ref, out_ref,
                        qt_a, k_a, vt_a, ot_a, gcol_a, grow_a, qt_b, k_b, vt_b, ot_b, gcol_b, grow_b,
                        cn_scr, m_scr, z_scr, *, n_chunks):
    L = MLSTM_CHUNK
    heads = range(A_HEADS)
    g = pl.program_id(0)
    weights = (wqt_ref, wk_ref, wvt_ref, wot_ref, wg_ref, wgt_ref, bcol_ref, brow_ref)
    set_a = (qt_a, k_a, vt_a, ot_a, gcol_a, grow_a)
    set_b = (qt_b, k_b, vt_b, ot_b, gcol_b, grow_b)

    @pl.when(g == 0)
    def _():
        for ref in set_b:
            ref[...] = jnp.zeros_like(ref)
        for vt_s in (vt_a, vt_b):
            vt_s[:, A_DV:AV_ROWS, :] = jnp.ones((A_HEADS, AV_ROWS - A_DV, L), _BF16)
        cn_scr[...] = jnp.zeros_like(cn_scr)
        m_scr[...] = jnp.zeros_like(m_scr)

    fresh = lax.rem(jnp.maximum(g - 1, 0), n_chunks) == 0
    causal = (lax.broadcasted_iota(jnp.int32, (L, L), 0)
              <= lax.broadcasted_iota(jnp.int32, (L, L), 1))

    def step(dst, src):
        qt_s, k_s, vt_s, ot_s, gcol_s, grow_s = src
        gcol = gcol_s[...]
        grow = grow_s[...]
        kh = [k_s[:, h * A_DQK:(h + 1) * A_DQK] for h in heads]
        qt = [qt_s[h * A_DQK:(h + 1) * A_DQK, :] for h in heads]
        cn_prev = [jnp.where(fresh, 0.0, cn_scr[h]) for h in heads]
        m_prev = [jnp.where(fresh, 0.0, m_scr[h][:, 0:1]) for h in heads]
        st = [_dot(kh[h], qt[h]) for h in heads]
        inter_mm = [_dot(cn_prev[h].astype(_BF16), qt[h]) for h in heads]

        i_row = [grow[h:h + 1, :] for h in heads]
        b_row = [grow[A_HEADS + h:A_HEADS + h + 1, :] for h in heads]
        d = [jnp.where(causal, gcol[:, h:h + 1] + b_row[h], -jnp.inf) for h in heads]
        inter = [b_row[h] + m_prev[h] for h in heads]
        m_t = [jnp.maximum(inter[h], jnp.max(d[h], axis=0, keepdims=True)) for h in heads]
        a = [(jnp.exp2(d[h] - m_t[h]) * st[h]).astype(_BF16) for h in heads]
        w_inter = [jnp.exp2(inter[h] - m_t[h]) for h in heads]

        hp = _modulate(xp_ref[...], sh_ref[...], sc_ref[...]).astype(_BF16)
        _mlstm_project_qkg(hp, weights, dst)
        _mlstm_project_v(hp, weights, dst)
        nd = [w_inter[h] * inter_mm[h] + _dot(vt_s[h], a[h]) for h in heads]
        _mlstm_project_o(hp, weights, dst)

        b_last = [b_row[h][:, L - 1:L] for h in heads]
        dl = [b_last[h] - b_row[h] + i_row[h] for h in heads]
        m_new = [jnp.maximum(b_last[h] + m_prev[h], jnp.max(dl[h], axis=1, keepdims=True))
                 for h in heads]
        for h in heads:
            vw = (vt_s[h].astype(_F32) * jnp.exp2(dl[h] - m_new[h])).astype(_BF16)
            decay = jnp.exp2(b_last[h] + m_prev[h] - m_new[h])
            cn_scr[h] = decay * cn_prev[h] + _dot(vw, kh[h])
            m_scr[h] = jnp.broadcast_to(m_new[h], (1, LANES))
        for h in heads:
            rows = slice(h * A_DV, (h + 1) * A_DV)
            den = nd[h][A_DV:A_DV + 1, :]
            ht = nd[h][0:A_DV, :] / jnp.maximum(jnp.abs(den), jnp.exp2(-m_t[h]))
            ht = (ht * lax.rsqrt(jnp.mean(ht * ht, axis=0, keepdims=True) + EPS)
                  * gain_ref[rows, :])
            z_scr[rows, :] = (jax.nn.sigmoid(ot_s[rows, :]) * ht).astype(_BF16)
        out_ref[...] = xc_ref[...] + g1_ref[...] * _dot_tn(z_scr[...], wout_ref[...])

    @pl.when(lax.rem(g, 2) == 0)
    def _():
        step(set_a, set_b)

    @pl.when(lax.rem(g, 2) == 1)
    def _():
        step(set_b, set_a)


def _mlstm_layer(x, mod, layer, w_in, b_i, b_f, head_gain, w_out):
    batch, seq, _ = x.shape
    L = MLSTM_CHUNK
    nc = seq // L
    total = batch * nc
    w = w_in.astype(_BF16)
    wqt, wk = w[:, :A_QK_W].T, w[:, A_QK_W:2 * A_QK_W]
    wvt = w[:, 2 * A_QK_W:2 * A_QK_W + A_V_W].T
    wot = w[:, 2 * A_QK_W + A_V_W:2 * A_QK_W + 2 * A_V_W].T
    wgate = w[:, 2 * A_QK_W + 2 * A_V_W:]
    wg = jnp.pad(wgate, ((0, 0), (0, LANES - 2 * A_HEADS)))
    bias = jnp.concatenate([b_i, b_f]).astype(_F32)
    bcol = jnp.pad(bias, (0, LANES - 2 * A_HEADS)).reshape(1, LANES)
    brow = bias.reshape(2 * A_HEADS, 1)
    gain = jnp.broadcast_to(head_gain.reshape(A_V_W, 1).astype(_F32), (A_V_W, L))

    proj_chunk = lambda g: jnp.minimum(g, total - 1)
    rec_chunk = lambda g: jnp.maximum(g - 1, 0)
    tok = lambda chunk: pl.BlockSpec(
        (None, L, D_MODEL), lambda g: (chunk(g) // nc, lax.rem(chunk(g), nc), 0))
    modrow = lambda slot, chunk: pl.BlockSpec(
        (None, None, None, 1, D_MODEL), lambda g: (layer, slot, chunk(g) // nc, 0, 0))
    scratch_set = [
        pltpu.VMEM((A_QK_W, L), _BF16), pltpu.VMEM((L, A_QK_W), _BF16),
        pltpu.VMEM((A_HEADS, AV_ROWS, L), _BF16), pltpu.VMEM((A_V_W, L), _F32),
        pltpu.VMEM((L, LANES), _F32), pltpu.VMEM((2 * A_HEADS, L), _F32),
    ]
    return pl.pallas_call(
        functools.partial(_mlstm_layer_kernel, n_chunks=nc),
        grid=(total + 1,),
        in_specs=[
            tok(proj_chunk), tok(rec_chunk),
            modrow(0, proj_chunk), modrow(1, proj_chunk), modrow(2, rec_chunk),
            _const_spec((A_QK_W, D_MODEL)), _const_spec((D_MODEL, A_QK_W)),
            _const_spec((A_V_W, D_MODEL)), _const_spec((A_V_W, D_MODEL)),
            _const_spec((D_MODEL, LANES)), _const_spec((2 * A_HEADS, D_MODEL)),
            _const_spec((1, LANES)), _const_spec((2 * A_HEADS, 1)),
            _const_spec((A_V_W, L)), _const_spec((A_V_W, D_MODEL)),
        ],
        out_specs=tok(rec_chunk),
        out_shape=jax.ShapeDtypeStruct(x.shape, _F32),
        scratch_shapes=scratch_set + scratch_set + [
            pltpu.VMEM((A_HEADS, AV_ROWS, A_DQK), _F32),
            pltpu.VMEM((A_HEADS, 1, LANES), _F32),
            pltpu.VMEM((A_V_W, L), _BF16),
        ],
        compiler_params=_params(1),
        name="mlstm_layer",
    )(x, x, mod, mod, mod, wqt, wk, wvt, wot, wg, wgate.T, bcol, brow, gain,
      w_out.astype(_BF16))


def _mlp_kernel(x_ref, sh_ref, sc_ref, g_ref, w1_ref, w2_ref, fgain_ref, out_ref, u_scr, *,
                final_norm):
    x = x_ref[...]
    h = _modulate(x, sh_ref[...], sc_ref[...]).astype(_BF16)
    for c in range(D_FF // FF_CHUNK):
        u = jnp.maximum(_dot(h, w1_ref[:, c * FF_CHUNK:(c + 1) * FF_CHUNK]), 0.0)
        u_scr[:, c * FF_CHUNK:(c + 1) * FF_CHUNK] = (u * u).astype(_BF16)
    y = x + g_ref[...] * _dot(u_scr[...], w2_ref[...])
    if final_norm:
        y = _rms(y) * fgain_ref[...]
    out_ref[...] = y


def _layer_spec(layer, rows, cols):
    return pl.BlockSpec((None, rows, cols), lambda *_: (layer, 0, 0),
                        pipeline_mode=pl.Buffered(1))


def _mlp(x, mod, layer, w1, w2, final_gain, final_norm):
    batch, seq, _ = x.shape
    tm = MLP_TILE
    tok = pl.BlockSpec((None, tm, D_MODEL), lambda b, i: (b, i, 0))
    return pl.pallas_call(
        functools.partial(_mlp_kernel, final_norm=final_norm),
        grid=(batch, seq // tm),
        in_specs=[
            tok, _mod_spec(layer, 3), _mod_spec(layer, 4), _mod_spec(layer, 5),
            _layer_spec(layer, D_MODEL, D_FF), _layer_spec(layer, D_FF, D_MODEL),
            _const_spec((1, D_MODEL)),
        ],
        out_specs=tok,
        out_shape=jax.ShapeDtypeStruct(x.shape, _F32),
        scratch_shapes=[pltpu.VMEM((tm, D_FF), _BF16)],
        compiler_params=_params(2),
        name="mlp",
    )(x, mod, mod, mod, w1, w2, final_gain.reshape(1, D_MODEL).astype(_F32))


def _split_bf16(x):
    hi = x.astype(_BF16).astype(_F32)
    rest = x - hi
    mid = rest.astype(_BF16).astype(_F32)
    lo = (rest - mid).astype(_BF16).astype(_F32)
    return hi, mid, lo


N_BIAS = 3


def _bias_placement():
    place = np.zeros((LANES, B_HEADS * LANES), np.float32)
    for hd in range(B_HEADS):
        spare = hd * LANES + (0 if hd % 2 else B_DH)
        for piece in range(N_BIAS):
            place[piece * B_HEADS + hd, spare + piece] = -1.0
            place[N_BIAS * B_HEADS, spare + N_BIAS + piece] = 1.0
    return jnp.asarray(place, _BF16)


def _fox_kv_kernel(x_ref, gain_ref, wk_ref, wvt_ref, wf_ref, wft_ref, bcol_ref, brow_ref,
                   place_ref, kaug_ref, vt_ref, frow_ref, ccol_scr, crow_scr):
    tm = KV_TILE

    @pl.when(pl.program_id(1) == 0)
    def _():
        ccol_scr[...] = jnp.zeros_like(ccol_scr)
        crow_scr[...] = jnp.zeros_like(crow_scr)

    ones = jnp.ones((V_ROWS - B_DH, tm), _BF16)
    lane = lax.broadcasted_iota(jnp.int32, (tm, LANES), 1)
    carry_col = ccol_scr[...]
    carry_row = crow_scr[:, 0:1]
    for blk in range(KV_STEP):
        h = (_rms(x_ref[blk * tm:(blk + 1) * tm, :]) * gain_ref[...]).astype(_BF16)
        vt = _dot_nt(wvt_ref[...], h).astype(_BF16)
        for hd in range(B_HEADS):
            vt_ref[blk, hd, 0:B_DH, :] = vt[hd * B_DH:(hd + 1) * B_DH, :]
            vt_ref[blk, hd, B_DH:V_ROWS, :] = ones
        k = _dot(h, wk_ref[...])
        fc = _segment_cumsum(_log_sigmoid(_dot(h, wf_ref[...]) + bcol_ref[...]), 0, tm)
        fc = fc + carry_col
        carry_col = fc[tm - 1:tm, :]
        fr = _segment_cumsum(_log_sigmoid(_dot_nt(wft_ref[...], h) + brow_ref[...]), 1, tm)
        fr = fr + carry_row
        carry_row = fr[:, tm - 1:tm]
        frow_ref[blk] = fr * LOG2E
        hi, mid, lo = _split_bf16(fc * LOG2E)
        pieces = jnp.where(lane < B_HEADS, hi, jnp.where(
            lane < 2 * B_HEADS, pltpu.roll(mid, B_HEADS, 1), jnp.where(
                lane < 3 * B_HEADS, pltpu.roll(lo, 2 * B_HEADS, 1), jnp.where(
                    lane == 3 * B_HEADS, 1.0, 0.0)))).astype(_BF16)
        bias = _dot(pieces, place_ref[...])
        for hd in range(B_HEADS):
            pair, odd = divmod(hd, 2)
            own = (lane >= B_DH) if odd else (lane < B_DH)
            kaug_ref[blk, hd] = jnp.where(
                own, k[:, pair * LANES:(pair + 1) * LANES],
                bias[:, hd * LANES:(hd + 1) * LANES]).astype(_BF16)
    ccol_scr[...] = carry_col
    crow_scr[...] = jnp.broadcast_to(carry_row, crow_scr.shape)


def _fox_kv(x, kv_gain, w_kv, fg_bias):
    batch, seq, _ = x.shape
    tm = KV_TILE
    w = w_kv.astype(_BF16)
    wk, wvt, wfg = w[:, :D_MODEL], w[:, D_MODEL:2 * D_MODEL].T, w[:, 2 * D_MODEL:]
    wf = jnp.pad(wfg, ((0, 0), (0, LANES - B_HEADS)))
    wft = wfg.T
    bias = fg_bias.astype(_F32)
    bcol = jnp.pad(bias, (0, LANES - B_HEADS)).reshape(1, LANES)
    brow = bias.reshape(B_HEADS, 1)
    ks = KV_STEP
    return pl.pallas_call(
        _fox_kv_kernel,
        grid=(batch, seq // (ks * tm)),
        in_specs=[
            pl.BlockSpec((None, ks * tm, D_MODEL), lambda b, i: (b, i, 0)),
            _const_spec((1, D_MODEL)),
            _const_spec((D_MODEL, D_MODEL)), _const_spec((D_MODEL, D_MODEL)),
            _const_spec((D_MODEL, LANES)), _const_spec((B_HEADS, D_MODEL)),
            _const_spec((1, LANES)), _const_spec((B_HEADS, 1)),
            _const_spec((LANES, B_HEADS * LANES)),
        ],
        out_specs=[
            pl.BlockSpec((None, ks, B_HEADS, tm, LANES), lambda b, i: (b, i, 0, 0, 0)),
            pl.BlockSpec((None, ks, B_HEADS, V_ROWS, tm), lambda b, i: (b, i, 0, 0, 0)),
            pl.BlockSpec((None, ks, B_HEADS, tm), lambda b, i: (b, i, 0, 0)),
        ],
        out_shape=[
            jax.ShapeDtypeStruct((batch, seq // tm, B_HEADS, tm, LANES), _BF16),
            jax.ShapeDtypeStruct((batch, seq // tm, B_HEADS, V_ROWS, tm), _BF16),
            jax.ShapeDtypeStruct((batch, seq // tm, B_HEADS, tm), _F32),
        ],
        scratch_shapes=[pltpu.VMEM((1, LANES), _F32), pltpu.VMEM((B_HEADS, LANES), _F32)],
        compiler_params=_params(2),
        name="fox_kv",
    )(x, kv_gain.reshape(1, D_MODEL).astype(_F32), wk, wvt, wf, wft, bcol, brow,
      _bias_placement())


def _fox_attn_kernel(x_ref, sh_ref, sc_ref, g1_ref, wqt_ref, kaug_ref, vt_ref, frow_ref,
                     wout_ref, out_ref, qaug_scr, m_scr, acc_scr, o_scr):
    tq, tk = Q_TILE, KV_TILE
    qi = pl.program_id(1)
    n_groups = B_HEADS // HEAD_GROUP
    h = _modulate(x_ref[...], sh_ref[...], sc_ref[...]).astype(_BF16)
    ft = frow_ref[qi]
    sub = lax.broadcasted_iota(jnp.int32, (B_DH, tq), 0)
    slab = Q_PROJ_HEADS * B_DH
    qt = [(_dot_nt(wqt_ref[r * slab:(r + 1) * slab, :], h)
           * (B_DH ** -0.5 * LOG2E)).astype(_BF16)
          for r in range(B_HEADS // Q_PROJ_HEADS)]
    for hd in range(B_HEADS):
        hi, mid, lo = _split_bf16(ft[hd:hd + 1, :])
        bias = jnp.where(sub < N_BIAS, 1.0, jnp.where(sub == N_BIAS, hi, jnp.where(
            sub == N_BIAS + 1, mid, jnp.where(sub == N_BIAS + 2, lo, 0.0)))).astype(_BF16)
        r, off = divmod(hd, Q_PROJ_HEADS)
        qh = qt[r][off * B_DH:(off + 1) * B_DH, :]
        lo_half, hi_half = (bias, qh) if hd % 2 else (qh, bias)
        qaug_scr[hd, 0:B_DH, :] = lo_half
        qaug_scr[hd, B_DH:2 * B_DH, :] = hi_half
    m_scr[...] = jnp.full(m_scr.shape, -jnp.inf, _F32)
    acc_scr[...] = jnp.zeros_like(acc_scr)

    def kv_block(j, diagonal):
        if diagonal:
            keep = (lax.broadcasted_iota(jnp.int32, (tk, tq), 1)
                    >= lax.broadcasted_iota(jnp.int32, (tk, tq), 0))
        def scores(grp):
            return [_dot(kaug_ref[j, hd], qaug_scr[hd])
                    for hd in range(grp * HEAD_GROUP, (grp + 1) * HEAD_GROUP)]

        s_next = scores(0)
        for grp in range(n_groups):
            heads = range(grp * HEAD_GROUP, (grp + 1) * HEAD_GROUP)
            s_cur = s_next
            if grp + 1 < n_groups:
                s_next = scores(grp + 1)
            if diagonal:
                s_cur = [jnp.where(keep, s, -jnp.inf) for s in s_cur]
            m_old = [m_scr[hd] for hd in heads]
            m_new = [jnp.maximum(mo, jnp.max(s, axis=0, keepdims=True))
                     for mo, s in zip(m_old, s_cur)]
            alpha = [jnp.exp2(mo - mn) for mo, mn in zip(m_old, m_new)]
            p = [jnp.exp2(s - mn) for s, mn in zip(s_cur, m_new)]
            for i, hd in enumerate(heads):
                m_scr[hd] = m_new[i]
                acc_scr[hd] = alpha[i] * acc_scr[hd] + _dot(vt_ref[j, hd], p[i].astype(_BF16))

    def body(j, carry):
        kv_block(j, False)
        return carry

    lax.fori_loop(0, qi, body, 0)
    kv_block(qi, True)

    for hd in range(B_HEADS):
        o_scr[hd * B_DH:(hd + 1) * B_DH, :] = (
            acc_scr[hd, 0:B_DH, :] / acc_scr[hd, B_DH:B_DH + 1, :]).astype(_BF16)
    out_ref[...] = x_ref[...] + g1_ref[...] * _dot_tn(o_scr[...], wout_ref[...])


def _fox_attn(x, mod, layer, w_q, w_out, kaug, vt, frow):
    batch, seq, _ = x.shape
    tq, tk = Q_TILE, KV_TILE
    nkv = seq // tk
    tok = pl.BlockSpec((None, tq, D_MODEL), lambda b, i: (b, i, 0))
    return pl.pallas_call(
        _fox_attn_kernel,
        grid=(batch, seq // tq),
        in_specs=[
            tok, _mod_spec(layer, 0), _mod_spec(layer, 1), _mod_spec(layer, 2),
            _const_spec((D_MODEL, D_MODEL)),
            pl.BlockSpec((None, nkv, B_HEADS, tk, LANES), lambda b, i: (b, 0, 0, 0, 0)),
            pl.BlockSpec((None, nkv, B_HEADS, V_ROWS, tk), lambda b, i: (b, 0, 0, 0, 0)),
            pl.BlockSpec((None, nkv, B_HEADS, tk), lambda b, i: (b, 0, 0, 0)),
            _const_spec((D_MODEL, D_MODEL)),
        ],
        out_specs=tok,
        out_shape=jax.ShapeDtypeStruct(x.shape, _F32),
        scratch_shapes=[
            pltpu.VMEM((B_HEADS, 2 * B_DH, tq), _BF16),
            pltpu.VMEM((B_HEADS, 1, tq), _F32),
            pltpu.VMEM((B_HEADS, V_ROWS, tq), _F32), pltpu.VMEM((D_MODEL, tq), _BF16),
        ],
        compiler_params=_params(2),
        name="fox_attn",
    )(x, mod, mod, mod, w_q.T.astype(_BF16), kaug, vt, frow, w_out.astype(_BF16))


def kernel(x, c, ada_w, ada_b, a_w_in, a_b_i, a_b_f, a_head_gain, a_w_out, kv_gain, b_w_kv,
           b_fg_bias, b_w_q, b_w_out, mlp_w1, mlp_w2, final_gain):
    batch, seq, d = x.shape
    assert d == D_MODEL and Q_TILE == KV_TILE
    assert all(seq % t == 0 for t in (MLP_TILE, MLSTM_CHUNK, Q_TILE, KV_STEP * KV_TILE))
    mod = _adaln_table(c, ada_w, ada_b).reshape(DEPTH, 6, batch, 1, D_MODEL)
    w1_all, w2_all = mlp_w1.astype(_BF16), mlp_w2.astype(_BF16)
    shared = None
    for l in range(DEPTH):
        if l < N_A_LAYERS:
            x = _mlstm_layer(x, mod, l, a_w_in[l], a_b_i[l], a_b_f[l], a_head_gain[l], a_w_out[l])
        else:
            if shared is None:
                shared = _fox_kv(x, kv_gain, b_w_kv, b_fg_bias)
            j = l - N_A_LAYERS
            x = _fox_attn(x, mod, l, b_w_q[j], b_w_out[j], *shared)
        x = _mlp(x, mod, l, w1_all, w2_all, final_gain, final_norm=(l == DEPTH - 1))
    return x
```

```python
import functools

import jax
import jax.numpy as jnp
import numpy as np
from jax import lax
from jax.experimental import pallas as pl
from jax.experimental.pallas import tpu as pltpu

D_MODEL = 1024
DEPTH = 4
N_A_LAYERS = DEPTH // 2
A_HEADS = 4
A_DV = D_MODEL // A_HEADS
A_DQK = A_DV // 2
A_QK_W = A_HEADS * A_DQK
A_V_W = A_HEADS * A_DV
B_HEADS = 16
B_DH = D_MODEL // B_HEADS
D_FF = 4 * D_MODEL
EPS = 1e-6
LOG2E = 1.4426950408889634
AV_ROWS = A_DV + 16
V_ROWS = B_DH + 16

LANES = 128
MLSTM_CHUNK = 256
KV_TILE = 256
KV_STEP = 4
Q_TILE = 256
HEAD_GROUP = 8
Q_PROJ_HEADS = 4
MLP_TILE = 1024
FF_CHUNK = 512
VMEM_LIMIT = 56 * 1024 * 1024

_BF16 = jnp.bfloat16
_F32 = jnp.float32


def _dot(a, b):
    return jnp.dot(a, b, preferred_element_type=_F32)


def _dot_nt(a, b):
    return lax.dot_general(a, b, (((1,), (1,)), ((), ())), preferred_element_type=_F32)


def _dot_tn(a, b):
    return lax.dot_general(a, b, (((0,), (0,)), ((), ())), preferred_element_type=_F32)


def _rms(x):
    return x * lax.rsqrt(jnp.mean(x * x, axis=-1, keepdims=True) + EPS)


def _modulate(x, shift, scale):
    return _rms(x) * (1.0 + scale) + shift


def _log_sigmoid(z):
    return jnp.minimum(z, 0.0) - jnp.log1p(jnp.exp(-jnp.abs(z)))


def _segment_cumsum(x, axis, seg):
    pos = lax.broadcasted_iota(jnp.int32, x.shape, axis) & (seg - 1)
    k = 1
    while k < seg:
        x = x + jnp.where(pos >= k, pltpu.roll(x, k, axis), 0.0)
        k *= 2
    return x


def _params(n_grid):
    return pltpu.CompilerParams(dimension_semantics=("arbitrary",) * n_grid,
                                vmem_limit_bytes=VMEM_LIMIT)


def _const_spec(shape):
    return pl.BlockSpec(shape, lambda *_: (0,) * len(shape), pipeline_mode=pl.Buffered(1))


def _mod_spec(layer, slot):
    return pl.BlockSpec((None, None, None, 1, D_MODEL), lambda b, i: (layer, slot, b, 0, 0))


def _adaln_kernel(c_ref, w_ref, b_ref, o_ref):
    c = c_ref[...]
    cond = (c * jax.nn.sigmoid(c)).astype(_BF16)
    o_ref[...] = _dot(cond, w_ref[...].astype(_BF16)) + b_ref[...]


def _adaln_table(c, ada_w, ada_b):
    batch = c.shape[0]
    return pl.pallas_call(
        _adaln_kernel,
        grid=(DEPTH, 6),
        in_specs=[
            pl.BlockSpec((batch, D_MODEL), lambda l, j: (0, 0)),
            pl.BlockSpec((None, D_MODEL, D_MODEL), lambda l, j: (l, 0, j)),
            pl.BlockSpec((None, None, 1, D_MODEL), lambda l, j: (l, j, 0, 0)),
        ],
        out_specs=pl.BlockSpec((None, None, batch, D_MODEL), lambda l, j: (l, j, 0, 0)),
        out_shape=jax.ShapeDtypeStruct((DEPTH, 6, batch, D_MODEL), _F32),
        compiler_params=_params(2),
        name="adaln_table",
    )(c, ada_w, ada_b.reshape(DEPTH, 6, 1, D_MODEL))


def _mlstm_project_qkg(h, w, dst):
    wqt_ref, wk_ref, _, _, wg_ref, wgt_ref, bcol_ref, brow_ref = w
    qt_s, k_s, _, _, gcol_s, grow_s = dst
    ct = MLSTM_CHUNK
    qt_s[...] = (_dot_nt(wqt_ref[...], h) * (A_DQK ** -0.5)).astype(_BF16)
    k_s[...] = _dot(h, wk_ref[...]).astype(_BF16)
    z = _dot(h, wg_ref[...]) + bcol_ref[...]
    bc = _segment_cumsum(_log_sigmoid(z), 0, ct)
    gcol_s[...] = (z - pltpu.roll(bc, LANES - A_HEADS, 1)) * LOG2E
    zr = _dot_nt(wgt_ref[...], h) + brow_ref[...]
    br = _segment_cumsum(_log_sigmoid(zr), 1, ct)
    sub = lax.broadcasted_iota(jnp.int32, zr.shape, 0)
    grow_s[...] = jnp.where(sub < A_HEADS, zr, br) * LOG2E


def _mlstm_project_v(h, w, dst):
    vt = _dot_nt(w[2][...], h).astype(_BF16)
    for hd in range(A_HEADS):
        dst[2][hd, 0:A_DV, :] = vt[hd * A_DV:(hd + 1) * A_DV, :]


def _mlstm_project_o(h, w, dst):
    dst[3][...] = _dot_nt(w[3][...], h)


def _mlstm_layer_kernel(xp_ref, xc_ref, sh_ref, sc_ref, g1_ref, wqt_ref, wk_ref, wvt_ref, wot_ref,
                        wg_ref, wgt_ref, bcol_ref, brow_ref, gain_ref, wout_ref, out_ref,
                        qt_a, k_a, vt_a, ot_a, gcol_a, grow_a, qt_b, k_b, vt_b, ot_b, gcol_b, grow_b,
                        cn_scr, m_scr, z_scr, *, n_chunks):
    L = MLSTM_CHUNK
    heads = range(A_HEADS)
    g = pl.program_id(0)
    weights = (wqt_ref, wk_ref, wvt_ref, wot_ref, wg_ref, wgt_ref, bcol_ref, brow_ref)
    set_a = (qt_a, k_a, vt_a, ot_a, gcol_a, grow_a)
    set_b = (qt_b, k_b, vt_b, ot_b, gcol_b, grow_b)

    @pl.when(g == 0)
    def _():
        for ref in set_b:
            ref[...] = jnp.zeros_like(ref)
        for vt_s in (vt_a, vt_b):
            vt_s[:, A_DV:AV_ROWS, :] = jnp.ones((A_HEADS, AV_ROWS - A_DV, L), _BF16)
        cn_scr[...] = jnp.zeros_like(cn_scr)
        m_scr[...] = jnp.zeros_like(m_scr)

    fresh = lax.rem(jnp.maximum(g - 1, 0), n_chunks) == 0
    causal = (lax.broadcasted_iota(jnp.int32, (L, L), 0)
              <= lax.broadcasted_iota(jnp.int32, (L, L), 1))

    def step(dst, src):
        qt_s, k_s, vt_s, ot_s, gcol_s, grow_s = src
        gcol = gcol_s[...]
        grow = grow_s[...]
        kh = [k_s[:, h * A_DQK:(h + 1) * A_DQK] for h in heads]
        qt = [qt_s[h * A_DQK:(h + 1) * A_DQK, :] for h in heads]
        cn_prev = [jnp.where(fresh, 0.0, cn_scr[h]) for h in heads]
        m_prev = [jnp.where(fresh, 0.0, m_scr[h][:, 0:1]) for h in heads]
        st = [_dot(kh[h], qt[h]) for h in heads]
        inter_mm = [_dot(cn_prev[h].astype(_BF16), qt[h]) for h in heads]

        i_row = [grow[h:h + 1, :] for h in heads]
        b_row = [grow[A_HEADS + h:A_HEADS + h + 1, :] for h in heads]
        d = [jnp.where(causal, gcol[:, h:h + 1] + b_row[h], -jnp.inf) for h in heads]
        inter = [b_row[h] + m_prev[h] for h in heads]
        m_t = [jnp.maximum(inter[h], jnp.max(d[h], axis=0, keepdims=True)) for h in heads]
        a = [(jnp.exp2(d[h] - m_t[h]) * st[h]).astype(_BF16) for h in heads]
        w_inter = [jnp.exp2(inter[h] - m_t[h]) for h in heads]

        hp = _modulate(xp_ref[...], sh_ref[...], sc_ref[...]).astype(_BF16)
        _mlstm_project_qkg(hp, weights, dst)
        _mlstm_project_v(hp, weights, dst)
        nd = [w_inter[h] * inter_mm[h] + _dot(vt_s[h], a[h]) for h in heads]
        _mlstm_project_o(hp, weights, dst)

        b_last = [b_row[h][:, L - 1:L] for h in heads]
        dl = [b_last[h] - b_row[h] + i_row[h] for h in heads]
        m_new = [jnp.maximum(b_last[h] + m_prev[h], jnp.max(dl[h], axis=1, keepdims=True))
                 for h in heads]
        for h in heads:
            rows = slice(h * A_DV, (h + 1) * A_DV)
            den = nd[h][A_DV:A_DV + 1, :]
            ht = nd[h][0:A_DV, :] / jnp.maximum(jnp.abs(den), jnp.exp2(-m_t[h]))
            ht = (ht * lax.rsqrt(jnp.mean(ht * ht, axis=0, keepdims=True) + EPS)
                  * gain_ref[rows, :])
            z_scr[rows, :] = (jax.nn.sigmoid(ot_s[rows, :]) * ht).astype(_BF16)
        out_ref[...] = xc_ref[...] + g1_ref[...] * _dot_tn(z_scr[...], wout_ref[...])
        for h in heads:
            vw = (vt_s[h].astype(_F32) * jnp.exp2(dl[h] - m_new[h])).astype(_BF16)
            decay = jnp.exp2(b_last[h] + m_prev[h] - m_new[h])
            cn_scr[h] = decay * cn_prev[h] + _dot(vw, kh[h])
            m_scr[h] = jnp.broadcast_to(m_new[h], (1, LANES))

    @pl.when(lax.rem(g, 2) == 0)
    def _():
        step(set_a, set_b)

    @pl.when(lax.rem(g, 2) == 1)
    def _():
        step(set_b, set_a)


def _mlstm_layer(x, mod, layer, w_in, b_i, b_f, head_gain, w_out):
    batch, seq, _ = x.shape
    L = MLSTM_CHUNK
    nc = seq // L
    total = batch * nc
    w = w_in.astype(_BF16)
    wqt, wk = w[:, :A_QK_W].T, w[:, A_QK_W:2 * A_QK_W]
    wvt = w[:, 2 * A_QK_W:2 * A_QK_W + A_V_W].T
    wot = w[:, 2 * A_QK_W + A_V_W:2 * A_QK_W + 2 * A_V_W].T
    wgate = w[:, 2 * A_QK_W + 2 * A_V_W:]
    wg = jnp.pad(wgate, ((0, 0), (0, LANES - 2 * A_HEADS)))
    bias = jnp.concatenate([b_i, b_f]).astype(_F32)
    bcol = jnp.pad(bias, (0, LANES - 2 * A_HEADS)).reshape(1, LANES)
    brow = bias.reshape(2 * A_HEADS, 1)
    gain = jnp.broadcast_to(head_gain.reshape(A_V_W, 1).astype(_F32), (A_V_W, L))

    proj_chunk = lambda g: jnp.minimum(g, total - 1)
    rec_chunk = lambda g: jnp.maximum(g - 1, 0)
    tok = lambda chunk: pl.BlockSpec(
        (None, L, D_MODEL), lambda g: (chunk(g) // nc, lax.rem(chunk(g), nc), 0))
    modrow = lambda slot, chunk: pl.BlockSpec(
        (None, None, None, 1, D_MODEL), lambda g: (layer, slot, chunk(g) // nc, 0, 0))
    scratch_set = [
        pltpu.VMEM((A_QK_W, L), _BF16), pltpu.VMEM((L, A_QK_W), _BF16),
        pltpu.VMEM((A_HEADS, AV_ROWS, L), _BF16), pltpu.VMEM((A_V_W, L), _F32),
        pltpu.VMEM((L, LANES), _F32), pltpu.VMEM((2 * A_HEADS, L), _F32),
    ]
    return pl.pallas_call(
        functools.partial(_mlstm_layer_kernel, n_chunks=nc),
        grid=(total + 1,),
        in_specs=[
            tok(proj_chunk), tok(rec_chunk),
            modrow(0, proj_chunk), modrow(1, proj_chunk), modrow(2, rec_chunk),
            _const_spec((A_QK_W, D_MODEL)), _const_spec((D_MODEL, A_QK_W)),
            _const_spec((A_V_W, D_MODEL)), _const_spec((A_V_W, D_MODEL)),
            _const_spec((D_MODEL, LANES)), _const_spec((2 * A_HEADS, D_MODEL)),
            _const_spec((1, LANES)), _const_spec((2 * A_HEADS, 1)),
            _const_spec((A_V_W, L)), _const_spec((A_V_W, D_MODEL)),
        ],
        out_specs=tok(rec_chunk),
        out_shape=jax.ShapeDtypeStruct(x.shape, _F32),
        scratch_shapes=scratch_set + scratch_set + [
            pltpu.VMEM((A_HEADS, AV_ROWS, A_DQK), _F32),
            pltpu.VMEM((A_HEADS, 1, LANES), _F32),
            pltpu.VMEM((A_V_W, L), _BF16),
        ],
        compiler_params=_params(1),
        name="mlstm_layer",
    )(x, x, mod, mod, mod, wqt, wk, wvt, wot, wg, wgate.T, bcol, brow, gain,
      w_out.astype(_BF16))


def _mlp_kernel(x_ref, sh_ref, sc_ref, g_ref, w1_ref, w2_ref, fgain_ref, out_ref, u_scr, *,
                final_norm):
    x = x_ref[...]
    h = _modulate(x, sh_ref[...], sc_ref[...]).astype(_BF16)
    for c in range(D_FF // FF_CHUNK):
        u = jnp.maximum(_dot(h, w1_ref[:, c * FF_CHUNK:(c + 1) * FF_CHUNK]), 0.0)
        u_scr[:, c * FF_CHUNK:(c + 1) * FF_CHUNK] = (u * u).astype(_BF16)
    y = x + g_ref[...] * _dot(u_scr[...], w2_ref[...])
    if final_norm:
        y = _rms(y) * fgain_ref[...]
    out_ref[...] = y


def _layer_spec(layer, rows, cols):
    return pl.BlockSpec((None, rows, cols), lambda *_: (layer, 0, 0),
                        pipeline_mode=pl.Buffered(1))


def _mlp(x, mod, layer, w1, w2, final_gain, final_norm):
    batch, seq, _ = x.shape
    tm = MLP_TILE
    tok = pl.BlockSpec((None, tm, D_MODEL), lambda b, i: (b, i, 0))
    return pl.pallas_call(
        functools.partial(_mlp_kernel, final_norm=final_norm),
        grid=(batch, seq // tm),
        in_specs=[
            tok, _mod_spec(layer, 3), _mod_spec(layer, 4), _mod_spec(layer, 5),
            _layer_spec(layer, D_MODEL, D_FF), _layer_spec(layer, D_FF, D_MODEL),
            _const_spec((1, D_MODEL)),
        ],
        out_specs=tok,
        out_shape=jax.ShapeDtypeStruct(x.shape, _F32),
        scratch_shapes=[pltpu.VMEM((tm, D_FF), _BF16)],
        compiler_params=_params(2),
        name="mlp",
    )(x, mod, mod, mod, w1, w2, final_gain.reshape(1, D_MODEL).astype(_F32))


def _split_bf16(x):
    hi = x.astype(_BF16).astype(_F32)
    rest = x - hi
    mid = rest.astype(_BF16).astype(_F32)
    lo = (rest - mid).astype(_BF16).astype(_F32)
    return hi, mid, lo


N_BIAS = 3
MAX_LOG2_WEIGHT = 60.0
BOUND_SLACK = 1.0 + 2.0 ** -6


def _bias_placement():
    place = np.zeros((LANES, B_HEADS * LANES), np.float32)
    for hd in range(B_HEADS):
        spare = hd * LANES + (0 if hd % 2 else B_DH)
        for piece in range(N_BIAS):
            place[piece * B_HEADS + hd, spare + piece] = -1.0
            place[N_BIAS * B_HEADS, spare + N_BIAS + piece] = 1.0
    return jnp.asarray(place, _BF16)


def _fox_kv_kernel(x_ref, gain_ref, wk_ref, wvt_ref, wf_ref, wft_ref, bcol_ref, brow_ref,
                   place_ref, kaug_ref, vt_ref, frow_ref, kmax2_ref, ccol_scr, crow_scr):
    tm = KV_TILE

    @pl.when(pl.program_id(1) == 0)
    def _():
        ccol_scr[...] = jnp.zeros_like(ccol_scr)
        crow_scr[...] = jnp.zeros_like(crow_scr)
        kmax2_ref[...] = jnp.zeros_like(kmax2_ref)

    ones = jnp.ones((V_ROWS - B_DH, tm), _BF16)
    lane = lax.broadcasted_iota(jnp.int32, (tm, LANES), 1)
    pair_lane = lax.broadcasted_iota(jnp.int32, (1, LANES), 1)
    carry_col = ccol_scr[...]
    carry_row = crow_scr[:, 0:1]
    kmax2 = kmax2_ref[...]
    for blk in range(KV_STEP):
        h = (_rms(x_ref[blk * tm:(blk + 1) * tm, :]) * gain_ref[...]).astype(_BF16)
        vt = _dot_nt(wvt_ref[...], h).astype(_BF16)
        for hd in range(B_HEADS):
            vt_ref[blk, hd, 0:B_DH, :] = vt[hd * B_DH:(hd + 1) * B_DH, :]
            vt_ref[blk, hd, B_DH:V_ROWS, :] = ones
        k = _dot(h, wk_ref[...])
        fc = _segment_cumsum(_log_sigmoid(_dot(h, wf_ref[...]) + bcol_ref[...]), 0, tm)
        fc = fc + carry_col
        carry_col = fc[tm - 1:tm, :]
        fr = _segment_cumsum(_log_sigmoid(_dot_nt(wft_ref[...], h) + brow_ref[...]), 1, tm)
        fr = fr + carry_row
        carry_row = fr[:, tm - 1:tm]
        frow_ref[blk] = fr * LOG2E
        hi, mid, lo = _split_bf16(fc * LOG2E)
        pieces = jnp.where(lane < B_HEADS, hi, jnp.where(
            lane < 2 * B_HEADS, pltpu.roll(mid, B_HEADS, 1), jnp.where(
                lane < 3 * B_HEADS, pltpu.roll(lo, 2 * B_HEADS, 1), jnp.where(
                    lane == 3 * B_HEADS, 1.0, 0.0)))).astype(_BF16)
        bias = _dot(pieces, place_ref[...])
        for hd in range(B_HEADS):
            pair, odd = divmod(hd, 2)
            own = (lane >= B_DH) if odd else (lane < B_DH)
            kaug_ref[blk, hd] = jnp.where(
                own, k[:, pair * LANES:(pair + 1) * LANES],
                bias[:, hd * LANES:(hd + 1) * LANES]).astype(_BF16)
        for pair in range(B_HEADS // 2):
            kb = k[:, pair * LANES:(pair + 1) * LANES].astype(_BF16).astype(_F32)
            n2 = jnp.max(jnp.sum(kb * kb, axis=1, keepdims=True), axis=0, keepdims=True)
            kmax2 = jnp.where(pair_lane == pair, jnp.maximum(kmax2, n2), kmax2)
    ccol_scr[...] = carry_col
    crow_scr[...] = jnp.broadcast_to(carry_row, crow_scr.shape)
    kmax2_ref[...] = kmax2


def _fox_kv(x, kv_gain, w_kv, fg_bias):
    batch, seq, _ = x.shape
    tm = KV_TILE
    w = w_kv.astype(_BF16)
    wk, wvt, wfg = w[:, :D_MODEL], w[:, D_MODEL:2 * D_MODEL].T, w[:, 2 * D_MODEL:]
    wf = jnp.pad(wfg, ((0, 0), (0, LANES - B_HEADS)))
    wft = wfg.T
    bias = fg_bias.astype(_F32)
    bcol = jnp.pad(bias, (0, LANES - B_HEADS)).reshape(1, LANES)
    brow = bias.reshape(B_HEADS, 1)
    ks = KV_STEP
    return pl.pallas_call(
        _fox_kv_kernel,
        grid=(batch, seq // (ks * tm)),
        in_specs=[
            pl.BlockSpec((None, ks * tm, D_MODEL), lambda b, i: (b, i, 0)),
            _const_spec((1, D_MODEL)),
            _const_spec((D_MODEL, D_MODEL)), _const_spec((D_MODEL, D_MODEL)),
            _const_spec((D_MODEL, LANES)), _const_spec((B_HEADS, D_MODEL)),
            _const_spec((1, LANES)), _const_spec((B_HEADS, 1)),
            _const_spec((LANES, B_HEADS * LANES)),
        ],
        out_specs=[
            pl.BlockSpec((None, ks, B_HEADS, tm, LANES), lambda b, i: (b, i, 0, 0, 0)),
            pl.BlockSpec((None, ks, B_HEADS, V_ROWS, tm), lambda b, i: (b, i, 0, 0, 0)),
            pl.BlockSpec((None, ks, B_HEADS, tm), lambda b, i: (b, i, 0, 0)),
            pl.BlockSpec((None, 1, LANES), lambda b, i: (b, 0, 0)),
        ],
        out_shape=[
            jax.ShapeDtypeStruct((batch, seq // tm, B_HEADS, tm, LANES), _BF16),
            jax.ShapeDtypeStruct((batch, seq // tm, B_HEADS, V_ROWS, tm), _BF16),
            jax.ShapeDtypeStruct((batch, seq // tm, B_HEADS, tm), _F32),
            jax.ShapeDtypeStruct((batch, 1, LANES), _F32),
        ],
        scratch_shapes=[pltpu.VMEM((1, LANES), _F32), pltpu.VMEM((B_HEADS, LANES), _F32)],
        compiler_params=_params(2),
        name="fox_kv",
    )(x, kv_gain.reshape(1, D_MODEL).astype(_F32), wk, wvt, wf, wft, bcol, brow,
      _bias_placement())


def _fox_attn_kernel(x_ref, sh_ref, sc_ref, g1_ref, wqt_ref, kaug_ref, vt_ref, frow_ref,
                     kmax2_ref, wout_ref, out_ref, qaug_scr, m_scr, acc_scr, o_scr):
    tq, tk = Q_TILE, KV_TILE
    qi = pl.program_id(1)
    n_groups = B_HEADS // HEAD_GROUP
    h = _modulate(x_ref[...], sh_ref[...], sc_ref[...]).astype(_BF16)
    ft = frow_ref[qi]
    sub = lax.broadcasted_iota(jnp.int32, (B_DH, tq), 0)
    slab = Q_PROJ_HEADS * B_DH
    qt = [(_dot_nt(wqt_ref[r * slab:(r + 1) * slab, :], h)
           * (B_DH ** -0.5 * LOG2E)).astype(_BF16)
          for r in range(B_HEADS // Q_PROJ_HEADS)]
    kmax2 = kmax2_ref[...]
    score_bound = []
    for hd in range(B_HEADS):
        hi, mid, lo = _split_bf16(ft[hd:hd + 1, :])
        bias = jnp.where(sub < N_BIAS, 1.0, jnp.where(sub == N_BIAS, hi, jnp.where(
            sub == N_BIAS + 1, mid, jnp.where(sub == N_BIAS + 2, lo, 0.0)))).astype(_BF16)
        r, off = divmod(hd, Q_PROJ_HEADS)
        qh = qt[r][off * B_DH:(off + 1) * B_DH, :]
        lo_half, hi_half = (bias, qh) if hd % 2 else (qh, bias)
        qaug_scr[hd, 0:B_DH, :] = lo_half
        qaug_scr[hd, B_DH:2 * B_DH, :] = hi_half
        qf = qh.astype(_F32)
        qn2 = jnp.sum(qf * qf, axis=0, keepdims=True)
        score_bound.append(jnp.sqrt(qn2 * kmax2[:, hd // 2:hd // 2 + 1]) * BOUND_SLACK + 1.0)
    m_scr[...] = jnp.full(m_scr.shape, -jnp.inf, _F32)
    acc_scr[...] = jnp.zeros_like(acc_scr)

    def kv_block(j, mode):
        if mode == "diagonal":
            keep = (lax.broadcasted_iota(jnp.int32, (tk, tq), 1)
                    >= lax.broadcasted_iota(jnp.int32, (tk, tq), 0))
        def scores(grp):
            return [_dot(kaug_ref[j, hd], qaug_scr[hd])
                    for hd in range(grp * HEAD_GROUP, (grp + 1) * HEAD_GROUP)]

        s_next = scores(0)
        for grp in range(n_groups):
            heads = range(grp * HEAD_GROUP, (grp + 1) * HEAD_GROUP)
            s_cur = s_next
            if grp + 1 < n_groups:
                s_next = scores(grp + 1)
            if mode == "fixed":
                for i, hd in enumerate(heads):
                    p_i = jnp.exp2(s_cur[i] - m_scr[hd]).astype(_BF16)
                    acc_scr[hd] = acc_scr[hd] + _dot(vt_ref[j, hd], p_i)
                continue
            if mode == "diagonal":
                s_cur = [jnp.where(keep, s, -jnp.inf) for s in s_cur]
            m_old = [m_scr[hd] for hd in heads]
            m_new = [jnp.maximum(mo, jnp.max(s, axis=0, keepdims=True))
                     for mo, s in zip(m_old, s_cur)]
            alpha = [jnp.exp2(mo - mn) for mo, mn in zip(m_old, m_new)]
            p = [jnp.exp2(s - mn) for s, mn in zip(s_cur, m_new)]
            for i, hd in enumerate(heads):
                m_scr[hd] = m_new[i]
                acc_scr[hd] = alpha[i] * acc_scr[hd] + _dot(vt_ref[j, hd], p[i].astype(_BF16))

    kv_block(qi, "diagonal")
    margin = m_scr[0] + MAX_LOG2_WEIGHT - score_bound[0]
    for hd in range(1, B_HEADS):
        margin = jnp.minimum(margin, m_scr[hd] + MAX_LOG2_WEIGHT - score_bound[hd])
    bounded = jnp.min(margin) >= 0.0

    def below_diagonal(mode):
        def body(j, carry):
            kv_block(j, mode)
            return carry
        lax.fori_loop(0, qi, body, 0)

    @pl.when(bounded)
    def _():
        below_diagonal("fixed")

    @pl.when(jnp.logical_not(bounded))
    def _():
        below_diagonal("rescale")

    for hd in range(B_HEADS):
        o_scr[hd * B_DH:(hd + 1) * B_DH, :] = (
            acc_scr[hd, 0:B_DH, :] / acc_scr[hd, B_DH:B_DH + 1, :]).astype(_BF16)
    out_ref[...] = x_ref[...] + g1_ref[...] * _dot_tn(o_scr[...], wout_ref[...])


def _fox_attn(x, mod, layer, w_q, w_out, kaug, vt, frow, kmax2):
    batch, seq, _ = x.shape
    tq, tk = Q_TILE, KV_TILE
    nkv = seq // tk
    tok = pl.BlockSpec((None, tq, D_MODEL), lambda b, i: (b, i, 0))
    return pl.pallas_call(
        _fox_attn_kernel,
        grid=(batch, seq // tq),
        in_specs=[
            tok, _mod_spec(layer, 0), _mod_spec(layer, 1), _mod_spec(layer, 2),
            _const_spec((D_MODEL, D_MODEL)),
            pl.BlockSpec((None, nkv, B_HEADS, tk, LANES), lambda b, i: (b, 0, 0, 0, 0)),
            pl.BlockSpec((None, nkv, B_HEADS, V_ROWS, tk), lambda b, i: (b, 0, 0, 0, 0)),
            pl.BlockSpec((None, nkv, B_HEADS, tk), lambda b, i: (b, 0, 0, 0)),
            pl.BlockSpec((None, 1, LANES), lambda b, i: (b, 0, 0)),
            _const_spec((D_MODEL, D_MODEL)),
        ],
        out_specs=tok,
        out_shape=jax.ShapeDtypeStruct(x.shape, _F32),
        scratch_shapes=[
            pltpu.VMEM((B_HEADS, 2 * B_DH, tq), _BF16),
            pltpu.VMEM((B_HEADS, 1, tq), _F32),
            pltpu.VMEM((B_HEADS, V_ROWS, tq), _F32), pltpu.VMEM((D_MODEL, tq), _BF16),
        ],
        compiler_params=_params(2),
        name="fox_attn",
    )(x, mod, mod, mod, w_q.T.astype(_BF16), kaug, vt, frow, kmax2, w_out.astype(_BF16))


def kernel(x, c, ada_w, ada_b, a_w_in, a_b_i, a_b_f, a_head_gain, a_w_out, kv_gain, b_w_kv,
           b_fg_bias, b_w_q, b_w_out, mlp_w1, mlp_w2, final_gain):
    batch, seq, d = x.shape
    assert d == D_MODEL and Q_TILE == KV_TILE
    assert all(seq % t == 0 for t in (MLP_TILE, MLSTM_CHUNK, Q_TILE, KV_STEP * KV_TILE))
    mod = _adaln_table(c, ada_w, ada_b).reshape(DEPTH, 6, batch, 1, D_MODEL)
    w1_all, w2_all = mlp_w1.astype(_BF16), mlp_w2.astype(_BF16)
    shared = None
    for l in range(DEPTH):
        if l < N_A_LAYERS:
            x = _mlstm_layer(x, mod, l, a_w_in[l], a_b_i[l], a_b_f[l], a_head_gain[l], a_w_out[l])
        else:
            if shared is None:
                shared = _fox_kv(x, kv_gain, b_w_kv, b_fg_bias)
            j = l - N_A_LAYERS
            x = _fox_attn(x, mod, l, b_w_q[j], b_w_out[j], *shared)
        x = _mlp(x, mod, l, w1_all, w2_all, final_gain, final_norm=(l == DEPTH - 1))
    return x
```

```python
import functools

import jax
import jax.numpy as jnp
import numpy as np
from jax import lax
from jax.experimental import pallas as pl
from jax.experimental.pallas import tpu as pltpu

D_MODEL = 1024
DEPTH = 4
N_A_LAYERS = DEPTH // 2
A_HEADS = 4
A_DV = D_MODEL // A_HEADS
A_DQK = A_DV // 2
A_QK_W = A_HEADS * A_DQK
A_V_W = A_HEADS * A_DV
B_HEADS = 16
B_DH = D_MODEL // B_HEADS
D_FF = 4 * D_MODEL
EPS = 1e-6
LOG2E = 1.4426950408889634
AV_ROWS = A_DV + 16
V_ROWS = B_DH + 16

LANES = 128
MLSTM_CHUNK = 256
KV_TILE = 256
KV_STEP = 4
Q_TILE = 256
Q_STEP = 2
HEAD_GROUP = 8
Q_PROJ_HEADS = 4
MLP_TILE = 1024
FF_CHUNK = 512
VMEM_LIMIT = 56 * 1024 * 1024

_BF16 = jnp.bfloat16
_F32 = jnp.float32


def _dot(a, b):
    return jnp.dot(a, b, preferred_element_type=_F32)


def _dot_nt(a, b):
    return lax.dot_general(a, b, (((1,), (1,)), ((), ())), preferred_element_type=_F32)


def _dot_tn(a, b):
    return lax.dot_general(a, b, (((0,), (0,)), ((), ())), preferred_element_type=_F32)


def _rms(x):
    return x * lax.rsqrt(jnp.mean(x * x, axis=-1, keepdims=True) + EPS)


def _modulate(x, shift, scale):
    return _rms(x) * (1.0 + scale) + shift


def _log_sigmoid(z):
    return jnp.minimum(z, 0.0) - jnp.log1p(jnp.exp(-jnp.abs(z)))


def _segment_cumsum(x, axis, seg):
    pos = lax.broadcasted_iota(jnp.int32, x.shape, axis) & (seg - 1)
    k = 1
    while k < seg:
        x = x + jnp.where(pos >= k, pltpu.roll(x, k, axis), 0.0)
        k *= 2
    return x


def _params(n_grid):
    return pltpu.CompilerParams(dimension_semantics=("arbitrary",) * n_grid,
                                vmem_limit_bytes=VMEM_LIMIT)


def _const_spec(shape):
    return pl.BlockSpec(shape, lambda *_: (0,) * len(shape), pipeline_mode=pl.Buffered(1))


def _mod_spec(layer, slot):
    return pl.BlockSpec((None, None, None, 1, D_MODEL), lambda b, i: (layer, slot, b, 0, 0))


def _adaln_kernel(c_ref, w_ref, b_ref, o_ref):
    c = c_ref[...]
    cond = (c * jax.nn.sigmoid(c)).astype(_BF16)
    o_ref[...] = _dot(cond, w_ref[...].astype(_BF16)) + b_ref[...]


def _adaln_table(c, ada_w, ada_b):
    batch = c.shape[0]
    return pl.pallas_call(
        _adaln_kernel,
        grid=(DEPTH, 6),
        in_specs=[
            pl.BlockSpec((batch, D_MODEL), lambda l, j: (0, 0)),
            pl.BlockSpec((None, D_MODEL, D_MODEL), lambda l, j: (l, 0, j)),
            pl.BlockSpec((None, None, 1, D_MODEL), lambda l, j: (l, j, 0, 0)),
        ],
        out_specs=pl.BlockSpec((None, None, batch, D_MODEL), lambda l, j: (l, j, 0, 0)),
        out_shape=jax.ShapeDtypeStruct((DEPTH, 6, batch, D_MODEL), _F32),
        compiler_params=_params(2),
        name="adaln_table",
    )(c, ada_w, ada_b.reshape(DEPTH, 6, 1, D_MODEL))


def _mlstm_project_qkg(h, w, dst):
    wqt_ref, wk_ref, _, _, wg_ref, wgt_ref, bcol_ref, brow_ref = w
    qt_s, k_s, _, _, gcol_s, grow_s = dst
    ct = MLSTM_CHUNK
    qt_s[...] = (_dot_nt(wqt_ref[...], h) * (A_DQK ** -0.5)).astype(_BF16)
    k_s[...] = _dot(h, wk_ref[...]).astype(_BF16)
    z = _dot(h, wg_ref[...]) + bcol_ref[...]
    bc = _segment_cumsum(_log_sigmoid(z), 0, ct)
    gcol_s[...] = (z - pltpu.roll(bc, LANES - A_HEADS, 1)) * LOG2E
    zr = _dot_nt(wgt_ref[...], h) + brow_ref[...]
    br = _segment_cumsum(_log_sigmoid(zr), 1, ct)
    sub = lax.broadcasted_iota(jnp.int32, zr.shape, 0)
    grow_s[...] = jnp.where(sub < A_HEADS, zr, br) * LOG2E


def _mlstm_project_v(h, w, dst):
    vt = _dot_nt(w[2][...], h).astype(_BF16)
    for hd in range(A_HEADS):
        dst[2][hd, 0:A_DV, :] = vt[hd * A_DV:(hd + 1) * A_DV, :]


def _mlstm_project_o(h, w, dst):
    dst[3][...] = _dot_nt(w[3][...], h)


def _mlstm_layer_kernel(xp_ref, xc_ref, sh_ref, sc_ref, g1_ref, wqt_ref, wk_ref, wvt_ref, wot_ref,
                        wg_ref, wgt_ref, bcol_ref, brow_ref, gain_ref, wout_ref, out_ref,
                        qt_a, k_a, vt_a, ot_a, gcol_a, grow_a, qt_b, k_b, vt_b, ot_b, gcol_b, grow_b,
                        cn_scr, m_scr, z_scr, *, n_chunks):
    L = MLSTM_CHUNK
    heads = range(A_HEADS)
    g = pl.program_id(0)
    weights = (wqt_ref, wk_ref, wvt_ref, wot_ref, wg_ref, wgt_ref, bcol_ref, brow_ref)
    set_a = (qt_a, k_a, vt_a, ot_a, gcol_a, grow_a)
    set_b = (qt_b, k_b, vt_b, ot_b, gcol_b, grow_b)

    @pl.when(g == 0)
    def _():
        for ref in set_b:
            ref[...] = jnp.zeros_like(ref)
        for vt_s in (vt_a, vt_b):
            vt_s[:, A_DV:AV_ROWS, :] = jnp.ones((A_HEADS, AV_ROWS - A_DV, L), _BF16)
        cn_scr[...] = jnp.zeros_like(cn_scr)
        m_scr[...] = jnp.zeros_like(m_scr)

    fresh = lax.rem(jnp.maximum(g - 1, 0), n_chunks) == 0
    causal = (lax.broadcasted_iota(jnp.int32, (L, L), 0)
              <= lax.broadcasted_iota(jnp.int32, (L, L), 1))

    def step(dst, src):
        qt_s, k_s, vt_s, ot_s, gcol_s, grow_s = src
        gcol = gcol_s[...]
        grow = grow_s[...]
        kh = [k_s[:, h * A_DQK:(h + 1) * A_DQK] for h in heads]
        qt = [qt_s[h * A_DQK:(h + 1) * A_DQK, :] for h in heads]
        cn_prev = [jnp.where(fresh, 0.0, cn_scr[h]) for h in heads]
        m_prev = [jnp.where(fresh, 0.0, m_scr[h][:, 0:1]) for h in heads]
        st = [_dot(kh[h], qt[h]) for h in heads]
        inter_mm = [_dot(cn_prev[h].astype(_BF16), qt[h]) for h in heads]

        i_row = [grow[h:h + 1, :] for h in heads]
        b_row = [grow[A_HEADS + h:A_HEADS + h + 1, :] for h in heads]
        d = [jnp.where(causal, gcol[:, h:h + 1] + b_row[h], -jnp.inf) for h in heads]
        inter = [b_row[h] + m_prev[h] for h in heads]
        m_t = [jnp.maximum(inter[h], jnp.max(d[h], axis=0, keepdims=True)) for h in heads]
        a = [(jnp.exp2(d[h] - m_t[h]) * st[h]).astype(_BF16) for h in heads]
        w_inter = [jnp.exp2(inter[h] - m_t[h]) for h in heads]

        hp = _modulate(xp_ref[...], sh_ref[...], sc_ref[...]).astype(_BF16)
        _mlstm_project_qkg(hp, weights, dst)
        _mlstm_project_v(hp, weights, dst)
        nd = [w_inter[h] * inter_mm[h] + _dot(vt_s[h], a[h]) for h in heads]
        _mlstm_project_o(hp, weights, dst)

        b_last = [b_row[h][:, L - 1:L] for h in heads]
        dl = [b_last[h] - b_row[h] + i_row[h] for h in heads]
        m_new = [jnp.maximum(b_last[h] + m_prev[h], jnp.max(dl[h], axis=1, keepdims=True))
                 for h in heads]
        for h in heads:
            rows = slice(h * A_DV, (h + 1) * A_DV)
            den = nd[h][A_DV:A_DV + 1, :]
            ht = nd[h][0:A_DV, :] / jnp.maximum(jnp.abs(den), jnp.exp2(-m_t[h]))
            ht = (ht * lax.rsqrt(jnp.mean(ht * ht, axis=0, keepdims=True) + EPS)
                  * gain_ref[rows, :])
            z_scr[rows, :] = (jax.nn.sigmoid(ot_s[rows, :]) * ht).astype(_BF16)
        out_ref[...] = xc_ref[...] + g1_ref[...] * _dot_tn(z_scr[...], wout_ref[...])
        for h in heads:
            vw = (vt_s[h].astype(_F32) * jnp.exp2(dl[h] - m_new[h])).astype(_BF16)
            decay = jnp.exp2(b_last[h] + m_prev[h] - m_new[h])
            cn_scr[h] = decay * cn_prev[h] + _dot(vw, kh[h])
            m_scr[h] = jnp.broadcast_to(m_new[h], (1, LANES))

    @pl.when(lax.rem(g, 2) == 0)
    def _():
        step(set_a, set_b)

    @pl.when(lax.rem(g, 2) == 1)
    def _():
        step(set_b, set_a)


def _mlstm_layer(x, mod, layer, w_in, b_i, b_f, head_gain, w_out):
    batch, seq, _ = x.shape
    L = MLSTM_CHUNK
    nc = seq // L
    total = batch * nc
    w = w_in.astype(_BF16)
    wqt, wk = w[:, :A_QK_W].T, w[:, A_QK_W:2 * A_QK_W]
    wvt = w[:, 2 * A_QK_W:2 * A_QK_W + A_V_W].T
    wot = w[:, 2 * A_QK_W + A_V_W:2 * A_QK_W + 2 * A_V_W].T
    wgate = w[:, 2 * A_QK_W + 2 * A_V_W:]
    wg = jnp.pad(wgate, ((0, 0), (0, LANES - 2 * A_HEADS)))
    bias = jnp.concatenate([b_i, b_f]).astype(_F32)
    bcol = jnp.pad(bias, (0, LANES - 2 * A_HEADS)).reshape(1, LANES)
    brow = bias.reshape(2 * A_HEADS, 1)
    gain = jnp.broadcast_to(head_gain.reshape(A_V_W, 1).astype(_F32), (A_V_W, L))

    proj_chunk = lambda g: jnp.minimum(g, total - 1)
    rec_chunk = lambda g: jnp.maximum(g - 1, 0)
    tok = lambda chunk: pl.BlockSpec(
        (None, L, D_MODEL), lambda g: (chunk(g) // nc, lax.rem(chunk(g), nc), 0))
    modrow = lambda slot, chunk: pl.BlockSpec(
        (None, None, None, 1, D_MODEL), lambda g: (layer, slot, chunk(g) // nc, 0, 0))
    scratch_set = [
        pltpu.VMEM((A_QK_W, L), _BF16), pltpu.VMEM((L, A_QK_W), _BF16),
        pltpu.VMEM((A_HEADS, AV_ROWS, L), _BF16), pltpu.VMEM((A_V_W, L), _F32),
        pltpu.VMEM((L, LANES), _F32), pltpu.VMEM((2 * A_HEADS, L), _F32),
    ]
    return pl.pallas_call(
        functools.partial(_mlstm_layer_kernel, n_chunks=nc),
        grid=(total + 1,),
        in_specs=[
            tok(proj_chunk), tok(rec_chunk),
            modrow(0, proj_chunk), modrow(1, proj_chunk), modrow(2, rec_chunk),
            _const_spec((A_QK_W, D_MODEL)), _const_spec((D_MODEL, A_QK_W)),
            _const_spec((A_V_W, D_MODEL)), _const_spec((A_V_W, D_MODEL)),
            _const_spec((D_MODEL, LANES)), _const_spec((2 * A_HEADS, D_MODEL)),
            _const_spec((1, LANES)), _const_spec((2 * A_HEADS, 1)),
            _const_spec((A_V_W, L)), _const_spec((A_V_W, D_MODEL)),
        ],
        out_specs=tok(rec_chunk),
        out_shape=jax.ShapeDtypeStruct(x.shape, _F32),
        scratch_shapes=scratch_set + scratch_set + [
            pltpu.VMEM((A_HEADS, AV_ROWS, A_DQK), _F32),
            pltpu.VMEM((A_HEADS, 1, LANES), _F32),
            pltpu.VMEM((A_V_W, L), _BF16),
        ],
        compiler_params=_params(1),
        name="mlstm_layer",
    )(x, x, mod, mod, mod, wqt, wk, wvt, wot, wg, wgate.T, bcol, brow, gain,
      w_out.astype(_BF16))


def _mlp_kernel(x_ref, sh_ref, sc_ref, g_ref, w1_ref, w2_ref, fgain_ref, out_ref, u_scr, *,
                final_norm):
    x = x_ref[...]
    h = _modulate(x, sh_ref[...], sc_ref[...]).astype(_BF16)
    for c in range(D_FF // FF_CHUNK):
        u = jnp.maximum(_dot(h, w1_ref[:, c * FF_CHUNK:(c + 1) * FF_CHUNK]), 0.0)
        u_scr[:, c * FF_CHUNK:(c + 1) * FF_CHUNK] = (u * u).astype(_BF16)
    y = x + g_ref[...] * _dot(u_scr[...], w2_ref[...])
    if final_norm:
        y = _rms(y) * fgain_ref[...]
    out_ref[...] = y


def _layer_spec(layer, rows, cols):
    return pl.BlockSpec((None, rows, cols), lambda *_: (layer, 0, 0),
                        pipeline_mode=pl.Buffered(1))


def _mlp(x, mod, layer, w1, w2, final_gain, final_norm):
    batch, seq, _ = x.shape
    tm = MLP_TILE
    tok = pl.BlockSpec((None, tm, D_MODEL), lambda b, i: (b, i, 0))
    return pl.pallas_call(
        functools.partial(_mlp_kernel, final_norm=final_norm),
        grid=(batch, seq // tm),
        in_specs=[
            tok, _mod_spec(layer, 3), _mod_spec(layer, 4), _mod_spec(layer, 5),
            _layer_spec(layer, D_MODEL, D_FF), _layer_spec(layer, D_FF, D_MODEL),
            _const_spec((1, D_MODEL)),
        ],
        out_specs=tok,
        out_shape=jax.ShapeDtypeStruct(x.shape, _F32),
        scratch_shapes=[pltpu.VMEM((tm, D_FF), _BF16)],
        compiler_params=_params(2),
        name="mlp",
    )(x, mod, mod, mod, w1, w2, final_gain.reshape(1, D_MODEL).astype(_F32))


def _split_bf16(x):
    hi = x.astype(_BF16).astype(_F32)
    rest = x - hi
    mid = rest.astype(_BF16).astype(_F32)
    lo = (rest - mid).astype(_BF16).astype(_F32)
    return hi, mid, lo


N_BIAS = 3
MAX_LOG2_WEIGHT = 60.0
BOUND_SLACK = 1.0 + 2.0 ** -6


def _bias_placement():
    place = np.zeros((LANES, B_HEADS * LANES), np.float32)
    for hd in range(B_HEADS):
        spare = hd * LANES + (0 if hd % 2 else B_DH)
        for piece in range(N_BIAS):
            place[piece * B_HEADS + hd, spare + piece] = -1.0
            place[N_BIAS * B_HEADS, spare + N_BIAS + piece] = 1.0
    return jnp.asarray(place, _BF16)


def _fox_kv_kernel(x_ref, gain_ref, wk_ref, wvt_ref, wf_ref, wft_ref, bcol_ref, brow_ref,
                   place_ref, kaug_ref, vt_ref, frow_ref, kmax2_ref, ccol_scr, crow_scr):
    tm = KV_TILE

    @pl.when(pl.program_id(1) == 0)
    def _():
        ccol_scr[...] = jnp.zeros_like(ccol_scr)
        crow_scr[...] = jnp.zeros_like(crow_scr)
        kmax2_ref[...] = jnp.zeros_like(kmax2_ref)

    ones = jnp.ones((V_ROWS - B_DH, tm), _BF16)
    lane = lax.broadcasted_iota(jnp.int32, (tm, LANES), 1)
    pair_lane = lax.broadcasted_iota(jnp.int32, (1, LANES), 1)
    carry_col = ccol_scr[...]
    carry_row = crow_scr[:, 0:1]
    kmax2 = kmax2_ref[...]
    for blk in range(KV_STEP):
        h = (_rms(x_ref[blk * tm:(blk + 1) * tm, :]) * gain_ref[...]).astype(_BF16)
        vt = _dot_nt(wvt_ref[...], h).astype(_BF16)
        for hd in range(B_HEADS):
            vt_ref[blk, hd, 0:B_DH, :] = vt[hd * B_DH:(hd + 1) * B_DH, :]
            vt_ref[blk, hd, B_DH:V_ROWS, :] = ones
        k = _dot(h, wk_ref[...])
        fc = _segment_cumsum(_log_sigmoid(_dot(h, wf_ref[...]) + bcol_ref[...]), 0, tm)
        fc = fc + carry_col
        carry_col = fc[tm - 1:tm, :]
        fr = _segment_cumsum(_log_sigmoid(_dot_nt(wft_ref[...], h) + brow_ref[...]), 1, tm)
        fr = fr + carry_row
        carry_row = fr[:, tm - 1:tm]
        frow_ref[blk] = fr * LOG2E
        hi, mid, lo = _split_bf16(fc * LOG2E)
        pieces = jnp.where(lane < B_HEADS, hi, jnp.where(
            lane < 2 * B_HEADS, pltpu.roll(mid, B_HEADS, 1), jnp.where(
                lane < 3 * B_HEADS, pltpu.roll(lo, 2 * B_HEADS, 1), jnp.where(
                    lane == 3 * B_HEADS, 1.0, 0.0)))).astype(_BF16)
        bias = _dot(pieces, place_ref[...])
        for hd in range(B_HEADS):
            pair, odd = divmod(hd, 2)
            own = (lane >= B_DH) if odd else (lane < B_DH)
            kaug_ref[blk, hd] = jnp.where(
                own, k[:, pair * LANES:(pair + 1) * LANES],
                bias[:, hd * LANES:(hd + 1) * LANES]).astype(_BF16)
        for pair in range(B_HEADS // 2):
            kb = k[:, pair * LANES:(pair + 1) * LANES]
            n2 = jnp.max(jnp.sum(kb * kb, axis=1, keepdims=True), axis=0, keepdims=True)
            kmax2 = jnp.where(pair_lane == pair, jnp.maximum(kmax2, n2), kmax2)
    ccol_scr[...] = carry_col
    crow_scr[...] = jnp.broadcast_to(carry_row, crow_scr.shape)
    kmax2_ref[...] = kmax2


def _fox_kv(x, kv_gain, w_kv, fg_bias):
    batch, seq, _ = x.shape
    tm = KV_TILE
    w = w_kv.astype(_BF16)
    wk, wvt, wfg = w[:, :D_MODEL], w[:, D_MODEL:2 * D_MODEL].T, w[:, 2 * D_MODEL:]
    wf = jnp.pad(wfg, ((0, 0), (0, LANES - B_HEADS)))
    wft = wfg.T
    bias = fg_bias.astype(_F32)
    bcol = jnp.pad(bias, (0, LANES - B_HEADS)).reshape(1, LANES)
    brow = bias.reshape(B_HEADS, 1)
    ks = KV_STEP
    return pl.pallas_call(
        _fox_kv_kernel,
        grid=(batch, seq // (ks * tm)),
        in_specs=[
            pl.BlockSpec((None, ks * tm, D_MODEL), lambda b, i: (b, i, 0)),
            _const_spec((1, D_MODEL)),
            _const_spec((D_MODEL, D_MODEL)), _const_spec((D_MODEL, D_MODEL)),
            _const_spec((D_MODEL, LANES)), _const_spec((B_HEADS, D_MODEL)),
            _const_spec((1, LANES)), _const_spec((B_HEADS, 1)),
            _const_spec((LANES, B_HEADS * LANES)),
        ],
        out_specs=[
            pl.BlockSpec((None, ks, B_HEADS, tm, LANES), lambda b, i: (b, i, 0, 0, 0)),
            pl.BlockSpec((None, ks, B_HEADS, V_ROWS, tm), lambda b, i: (b, i, 0, 0, 0)),
            pl.BlockSpec((None, ks, B_HEADS, tm), lambda b, i: (b, i, 0, 0)),
            pl.BlockSpec((None, 1, LANES), lambda b, i: (b, 0, 0)),
        ],
        out_shape=[
            jax.ShapeDtypeStruct((batch, seq // tm, B_HEADS, tm, LANES), _BF16),
            jax.ShapeDtypeStruct((batch, seq // tm, B_HEADS, V_ROWS, tm), _BF16),
            jax.ShapeDtypeStruct((batch, seq // tm, B_HEADS, tm), _F32),
            jax.ShapeDtypeStruct((batch, 1, LANES), _F32),
        ],
        scratch_shapes=[pltpu.VMEM((1, LANES), _F32), pltpu.VMEM((B_HEADS, LANES), _F32)],
        compiler_params=_params(2),
        name="fox_kv",
    )(x, kv_gain.reshape(1, D_MODEL).astype(_F32), wk, wvt, wf, wft, bcol, brow,
      _bias_placement())


def _fox_attn_kernel(x_ref, sh_ref, sc_ref, g1_ref, wqt_ref, kaug_ref, vt_ref, frow_ref,
                     kmax2_ref, wout_ref, out_ref, qaug_scr, m_scr, acc_scr, o_scr):
    for s in range(Q_STEP):
        rows = pl.ds(s * Q_TILE, Q_TILE)
        _fox_attn_block(pl.program_id(1) * Q_STEP + s, x_ref.at[rows], sh_ref, sc_ref, g1_ref,
                        wqt_ref, kaug_ref, vt_ref, frow_ref, kmax2_ref, wout_ref,
                        out_ref.at[rows], qaug_scr, m_scr, acc_scr, o_scr)


def _fox_attn_block(qi, x_ref, sh_ref, sc_ref, g1_ref, wqt_ref, kaug_ref, vt_ref, frow_ref,
                    kmax2_ref, wout_ref, out_ref, qaug_scr, m_scr, acc_scr, o_scr):
    tq, tk = Q_TILE, KV_TILE
    n_groups = B_HEADS // HEAD_GROUP
    h = _modulate(x_ref[...], sh_ref[...], sc_ref[...]).astype(_BF16)
    ft = frow_ref[qi]
    sub = lax.broadcasted_iota(jnp.int32, (B_DH, tq), 0)
    slab = Q_PROJ_HEADS * B_DH
    qt = [(_dot_nt(wqt_ref[r * slab:(r + 1) * slab, :], h)
           * (B_DH ** -0.5 * LOG2E)).astype(_BF16)
          for r in range(B_HEADS // Q_PROJ_HEADS)]
    kmax2 = kmax2_ref[...]
    score_bound = []
    for hd in range(B_HEADS):
        hi, mid, lo = _split_bf16(ft[hd:hd + 1, :])
        bias = jnp.where(sub < N_BIAS, 1.0, jnp.where(sub == N_BIAS, hi, jnp.where(
            sub == N_BIAS + 1, mid, jnp.where(sub == N_BIAS + 2, lo, 0.0)))).astype(_BF16)
        r, off = divmod(hd, Q_PROJ_HEADS)
        qh = qt[r][off * B_DH:(off + 1) * B_DH, :]
        lo_half, hi_half = (bias, qh) if hd % 2 else (qh, bias)
        qaug_scr[hd, 0:B_DH, :] = lo_half
        qaug_scr[hd, B_DH:2 * B_DH, :] = hi_half
        qf = qh.astype(_F32)
        qn2 = jnp.sum(qf * qf, axis=0, keepdims=True)
        score_bound.append(jnp.sqrt(qn2 * kmax2[:, hd // 2:hd // 2 + 1]) * BOUND_SLACK + 1.0)
    m_scr[...] = jnp.full(m_scr.shape, -jnp.inf, _F32)
    acc_scr[...] = jnp.zeros_like(acc_scr)

    def kv_block(j, mode):
        if mode == "diagonal":
            keep = (lax.broadcasted_iota(jnp.int32, (tk, tq), 1)
                    >= lax.broadcasted_iota(jnp.int32, (tk, tq), 0))
        def scores(grp):
            return [_dot(kaug_ref[j, hd], qaug_scr[hd])
                    for hd in range(grp * HEAD_GROUP, (grp + 1) * HEAD_GROUP)]

        s_next = scores(0)
        for grp in range(n_groups):
            heads = range(grp * HEAD_GROUP, (grp + 1) * HEAD_GROUP)
            s_cur = s_next
            if grp + 1 < n_groups:
                s_next = scores(grp + 1)
            if mode == "fixed":
                for i, hd in enumerate(heads):
                    p_i = jnp.exp2(s_cur[i] - m_scr[hd]).astype(_BF16)
                    acc_scr[hd] = acc_scr[hd] + _dot(vt_ref[j, hd], p_i)
                continue
            if mode == "diagonal":
                s_cur = [jnp.where(keep, s, -jnp.inf) for s in s_cur]
            m_old = [m_scr[hd] for hd in heads]
            m_new = [jnp.maximum(mo, jnp.max(s, axis=0, keepdims=True))
                     for mo, s in zip(m_old, s_cur)]
            alpha = [jnp.exp2(mo - mn) for mo, mn in zip(m_old, m_new)]
            p = [jnp.exp2(s - mn) for s, mn in zip(s_cur, m_new)]
            for i, hd in enumerate(heads):
                m_scr[hd] = m_new[i]
                acc_scr[hd] = alpha[i] * acc_scr[hd] + _dot(vt_ref[j, hd], p[i].astype(_BF16))

    kv_block(qi, "diagonal")
    margin = m_scr[0] + MAX_LOG2_WEIGHT - score_bound[0]
    for hd in range(1, B_HEADS):
        margin = jnp.minimum(margin, m_scr[hd] + MAX_LOG2_WEIGHT - score_bound[hd])
    bounded = jnp.min(margin) >= 0.0

    def below_diagonal(mode):
        def body(j, carry):
            kv_block(j, mode)
            return carry
        lax.fori_loop(0, qi, body, 0)

    @pl.when(bounded)
    def _():
        below_diagonal("fixed")

    @pl.when(jnp.logical_not(bounded))
    def _():
        below_diagonal("rescale")

    for hd in range(B_HEADS):
        o_scr[hd * B_DH:(hd + 1) * B_DH, :] = (
            acc_scr[hd, 0:B_DH, :] / acc_scr[hd, B_DH:B_DH + 1, :]).astype(_BF16)
    out_ref[...] = x_ref[...] + g1_ref[...] * _dot_tn(o_scr[...], wout_ref[...])


def _fox_attn(x, mod, layer, w_q, w_out, kaug, vt, frow, kmax2):
    batch, seq, _ = x.shape
    tq, tk = Q_TILE, KV_TILE
    nkv = seq // tk
    tok = pl.BlockSpec((None, Q_STEP * tq, D_MODEL), lambda b, i: (b, i, 0))
    return pl.pallas_call(
        _fox_attn_kernel,
        grid=(batch, seq // (Q_STEP * tq)),
        in_specs=[
            tok, _mod_spec(layer, 0), _mod_spec(layer, 1), _mod_spec(layer, 2),
            _const_spec((D_MODEL, D_MODEL)),
            pl.BlockSpec((None, nkv, B_HEADS, tk, LANES), lambda b, i: (b, 0, 0, 0, 0)),
            pl.BlockSpec((None, nkv, B_HEADS, V_ROWS, tk), lambda b, i: (b, 0, 0, 0, 0)),
            pl.BlockSpec((None, nkv, B_HEADS, tk), lambda b, i: (b, 0, 0, 0)),
            pl.BlockSpec((None, 1, LANES), lambda b, i: (b, 0, 0)),
            _const_spec((D_MODEL, D_MODEL)),
        ],
        out_specs=tok,
        out_shape=jax.ShapeDtypeStruct(x.shape, _F32),
        scratch_shapes=[
            pltpu.VMEM((B_HEADS, 2 * B_DH, tq), _BF16),
            pltpu.VMEM((B_HEADS, 1, tq), _F32),
            pltpu.VMEM((B_HEADS, V_ROWS, tq), _F32), pltpu.VMEM((D_MODEL, tq), _BF16),
        ],
        compiler_params=_params(2),
        name="fox_attn",
    )(x, mod, mod, mod, w_q.T.astype(_BF16), kaug, vt, frow, kmax2, w_out.astype(_BF16))


def kernel(x, c, ada_w, ada_b, a_w_in, a_b_i, a_b_f, a_head_gain, a_w_out, kv_gain, b_w_kv,
           b_fg_bias, b_w_q, b_w_out, mlp_w1, mlp_w2, final_gain):
    batch, seq, d = x.shape
    assert d == D_MODEL and Q_TILE == KV_TILE
    assert all(seq % t == 0
               for t in (MLP_TILE, MLSTM_CHUNK, Q_STEP * Q_TILE, KV_STEP * KV_TILE))
    mod = _adaln_table(c, ada_w, ada_b).reshape(DEPTH, 6, batch, 1, D_MODEL)
    w1_all, w2_all = mlp_w1.astype(_BF16), mlp_w2.astype(_BF16)
    shared = None
    for l in range(DEPTH):
        if l < N_A_LAYERS:
            x = _mlstm_layer(x, mod, l, a_w_in[l], a_b_i[l], a_b_f[l], a_head_gain[l], a_w_out[l])
        else:
            if shared is None:
                shared = _fox_kv(x, kv_gain, b_w_kv, b_fg_bias)
            j = l - N_A_LAYERS
            x = _fox_attn(x, mod, l, b_w_q[j], b_w_out[j], *shared)
        x = _mlp(x, mod, l, w1_all, w2_all, final_gain, final_norm=(l == DEPTH - 1))
    return x
```

```python
import functools

import jax
import jax.numpy as jnp
import numpy as np
from jax import lax
from jax.experimental import pallas as pl
from jax.experimental.pallas import tpu as pltpu

D_MODEL = 1024
DEPTH = 4
N_A_LAYERS = DEPTH // 2
A_HEADS = 4
A_DV = D_MODEL // A_HEADS
A_DQK = A_DV // 2
A_QK_W = A_HEADS * A_DQK
A_V_W = A_HEADS * A_DV
B_HEADS = 16
B_DH = D_MODEL // B_HEADS
D_FF = 4 * D_MODEL
EPS = 1e-6
LOG2E = 1.4426950408889634
LANES = 128
BF16_ROWS = 16
AV_ROWS = A_DV + BF16_ROWS
V_ROWS = B_DH + BF16_ROWS

MLSTM_CHUNK = 256
KV_TILE = 256
KV_STEP = 4
Q_TILE = 256
Q_STEP = 2
HEAD_GROUP = 8
Q_PROJ_HEADS = 4
MLP_TILE = 1024
FF_CHUNK = 512
VMEM_LIMIT = 56 * 1024 * 1024

_BF16 = jnp.bfloat16
_F32 = jnp.float32


def _dot(a, b):
    return jnp.dot(a, b, preferred_element_type=_F32)


def _dot_nt(a, b):
    return lax.dot_general(a, b, (((1,), (1,)), ((), ())), preferred_element_type=_F32)


def _dot_tn(a, b):
    return lax.dot_general(a, b, (((0,), (0,)), ((), ())), preferred_element_type=_F32)


def _rms(x):
    return x * lax.rsqrt(jnp.mean(x * x, axis=-1, keepdims=True) + EPS)


def _modulate(x, shift, scale):
    return _rms(x) * (1.0 + scale) + shift


def _log_sigmoid(z):
    return jnp.minimum(z, 0.0) - jnp.log1p(jnp.exp(-jnp.abs(z)))


def _segment_cumsum(x, axis, seg):
    pos = lax.broadcasted_iota(jnp.int32, x.shape, axis) & (seg - 1)
    k = 1
    while k < seg:
        x = x + jnp.where(pos >= k, pltpu.roll(x, k, axis), 0.0)
        k *= 2
    return x


def _params(n_grid):
    return pltpu.CompilerParams(dimension_semantics=("arbitrary",) * n_grid,
                                vmem_limit_bytes=VMEM_LIMIT)


def _const_spec(shape):
    return pl.BlockSpec(shape, lambda *_: (0,) * len(shape), pipeline_mode=pl.Buffered(1))


def _mod_spec(layer, slot):
    return pl.BlockSpec((None, None, None, 1, D_MODEL), lambda b, i: (layer, slot, b, 0, 0))


def _adaln_kernel(c_ref, w_ref, b_ref, o_ref):
    c = c_ref[...]
    cond = (c * jax.nn.sigmoid(c)).astype(_BF16)
    o_ref[...] = _dot(cond, w_ref[...].astype(_BF16)) + b_ref[...]


def _adaln_table(c, ada_w, ada_b):
    batch = c.shape[0]
    return pl.pallas_call(
        _adaln_kernel,
        grid=(DEPTH, 6),
        in_specs=[
            pl.BlockSpec((batch, D_MODEL), lambda l, j: (0, 0)),
            pl.BlockSpec((None, D_MODEL, D_MODEL), lambda l, j: (l, 0, j)),
            pl.BlockSpec((None, None, 1, D_MODEL), lambda l, j: (l, j, 0, 0)),
        ],
        out_specs=pl.BlockSpec((None, None, batch, D_MODEL), lambda l, j: (l, j, 0, 0)),
        out_shape=jax.ShapeDtypeStruct((DEPTH, 6, batch, D_MODEL), _F32),
        compiler_params=_params(2),
        name="adaln_table",
    )(c, ada_w, ada_b.reshape(DEPTH, 6, 1, D_MODEL))


def _mlstm_project_qkg(h, w, dst):
    wqt_ref, wk_ref, _, _, wg_ref, wgt_ref, bcol_ref, brow_ref = w
    qt_s, k_s, _, _, gcol_s, grow_s = dst
    ct = MLSTM_CHUNK
    qt_s[...] = (_dot_nt(wqt_ref[...], h) * (A_DQK ** -0.5)).astype(_BF16)
    k_s[...] = _dot(h, wk_ref[...]).astype(_BF16)
    z = _dot(h, wg_ref[...]) + bcol_ref[...]
    bc = _segment_cumsum(_log_sigmoid(z), 0, ct)
    gcol_s[...] = (z - pltpu.roll(bc, LANES - A_HEADS, 1)) * LOG2E
    zr = _dot_nt(wgt_ref[...], h) + brow_ref[...]
    br = _segment_cumsum(_log_sigmoid(zr), 1, ct)
    sub = lax.broadcasted_iota(jnp.int32, zr.shape, 0)
    grow_s[...] = jnp.where(sub < A_HEADS, zr, br) * LOG2E


def _mlstm_project_v(h, w, dst):
    vt = _dot_nt(w[2][...], h).astype(_BF16)
    for hd in range(A_HEADS):
        dst[2][hd, 0:A_DV, :] = vt[hd * A_DV:(hd + 1) * A_DV, :]


def _mlstm_project_o(h, w, dst):
    dst[3][...] = _dot_nt(w[3][...], h)


def _mlstm_layer_kernel(xp_ref, xc_ref, sh_ref, sc_ref, g1_ref, wqt_ref, wk_ref, wvt_ref, wot_ref,
                        wg_ref, wgt_ref, bcol_ref, brow_ref, gain_ref, wout_ref, out_ref,
                        qt_a, k_a, vt_a, ot_a, gcol_a, grow_a, qt_b, k_b, vt_b, ot_b, gcol_b, grow_b,
                        cn_scr, m_scr, z_scr, *, n_chunks):
    L = MLSTM_CHUNK
    heads = range(A_HEADS)
    g = pl.program_id(0)
    weights = (wqt_ref, wk_ref, wvt_ref, wot_ref, wg_ref, wgt_ref, bcol_ref, brow_ref)
    set_a = (qt_a, k_a, vt_a, ot_a, gcol_a, grow_a)
    set_b = (qt_b, k_b, vt_b, ot_b, gcol_b, grow_b)

    @pl.when(g == 0)
    def _():
        for ref in set_b:
            ref[...] = jnp.zeros_like(ref)
        for vt_s in (vt_a, vt_b):
            vt_s[:, A_DV:AV_ROWS, :] = jnp.ones((A_HEADS, AV_ROWS - A_DV, L), _BF16)
        cn_scr[...] = jnp.zeros_like(cn_scr)
        m_scr[...] = jnp.zeros_like(m_scr)

    fresh = lax.rem(jnp.maximum(g - 1, 0), n_chunks) == 0
    causal = (lax.broadcasted_iota(jnp.int32, (L, L), 0)
              <= lax.broadcasted_iota(jnp.int32, (L, L), 1))

    def step(dst, src):
        qt_s, k_s, vt_s, ot_s, gcol_s, grow_s = src
        gcol = gcol_s[...]
        grow = grow_s[...]
        kh = [k_s[:, h * A_DQK:(h + 1) * A_DQK] for h in heads]
        qt = [qt_s[h * A_DQK:(h + 1) * A_DQK, :] for h in heads]
        cn_prev = [jnp.where(fresh, 0.0, cn_scr[h]) for h in heads]
        m_prev = [jnp.where(fresh, 0.0, m_scr[h][:, 0:1]) for h in heads]
        st = [_dot(kh[h], qt[h]) for h in heads]
        inter_mm = [_dot(cn_prev[h].astype(_BF16), qt[h]) for h in heads]

        i_row = [grow[h:h + 1, :] for h in heads]
        b_row = [grow[A_HEADS + h:A_HEADS + h + 1, :] for h in heads]
        d = [jnp.where(causal, gcol[:, h:h + 1] + b_row[h], -jnp.inf) for h in heads]
        inter = [b_row[h] + m_prev[h] for h in heads]
        m_t = [jnp.maximum(inter[h], jnp.max(d[h], axis=0, keepdims=True)) for h in heads]
        a = [(jnp.exp2(d[h] - m_t[h]) * st[h]).astype(_BF16) for h in heads]
        w_inter = [jnp.exp2(inter[h] - m_t[h]) for h in heads]

        hp = _modulate(xp_ref[...], sh_ref[...], sc_ref[...]).astype(_BF16)
        _mlstm_project_qkg(hp, weights, dst)
        _mlstm_project_v(hp, weights, dst)
        nd = [w_inter[h] * inter_mm[h] + _dot(vt_s[h], a[h]) for h in heads]
        _mlstm_project_o(hp, weights, dst)

        b_last = [b_row[h][:, L - 1:L] for h in heads]
        dl = [b_last[h] - b_row[h] + i_row[h] for h in heads]
        m_new = [jnp.maximum(b_last[h] + m_prev[h], jnp.max(dl[h], axis=1, keepdims=True))
                 for h in heads]
        for h in heads:
            rows = slice(h * A_DV, (h + 1) * A_DV)
            den = nd[h][A_DV:A_DV + 1, :]
            ht = nd[h][0:A_DV, :] / jnp.maximum(jnp.abs(den), jnp.exp2(-m_t[h]))
            ht = (ht * lax.rsqrt(jnp.mean(ht * ht, axis=0, keepdims=True) + EPS)
                  * gain_ref[rows, :])
            z_scr[rows, :] = (jax.nn.sigmoid(ot_s[rows, :]) * ht).astype(_BF16)
        out_ref[...] = xc_ref[...] + g1_ref[...] * _dot_tn(z_scr[...], wout_ref[...])
        for h in heads:
            vw = (vt_s[h].astype(_F32) * jnp.exp2(dl[h] - m_new[h])).astype(_BF16)
            decay = jnp.exp2(b_last[h] + m_prev[h] - m_new[h])
            cn_scr[h] = decay * cn_prev[h] + _dot(vw, kh[h])
            m_scr[h] = jnp.broadcast_to(m_new[h], (1, LANES))

    @pl.when(lax.rem(g, 2) == 0)
    def _():
        step(set_a, set_b)

    @pl.when(lax.rem(g, 2) == 1)
    def _():
        step(set_b, set_a)


def _mlstm_layer(x, mod, layer, w_in, b_i, b_f, head_gain, w_out):
    batch, seq, _ = x.shape
    L = MLSTM_CHUNK
    nc = seq // L
    total = batch * nc
    w = w_in.astype(_BF16)
    wqt, wk = w[:, :A_QK_W].T, w[:, A_QK_W:2 * A_QK_W]
    wvt = w[:, 2 * A_QK_W:2 * A_QK_W + A_V_W].T
    wot = w[:, 2 * A_QK_W + A_V_W:2 * A_QK_W + 2 * A_V_W].T
    wgate = w[:, 2 * A_QK_W + 2 * A_V_W:]
    wg = jnp.pad(wgate, ((0, 0), (0, LANES - 2 * A_HEADS)))
    bias = jnp.concatenate([b_i, b_f]).astype(_F32)
    bcol = jnp.pad(bias, (0, LANES - 2 * A_HEADS)).reshape(1, LANES)
    brow = bias.reshape(2 * A_HEADS, 1)
    gain = jnp.broadcast_to(head_gain.reshape(A_V_W, 1).astype(_F32), (A_V_W, L))

    proj_chunk = lambda g: jnp.minimum(g, total - 1)
    rec_chunk = lambda g: jnp.maximum(g - 1, 0)
    tok = lambda chunk: pl.BlockSpec(
        (None, L, D_MODEL), lambda g: (chunk(g) // nc, lax.rem(chunk(g), nc), 0))
    modrow = lambda slot, chunk: pl.BlockSpec(
        (None, None, None, 1, D_MODEL), lambda g: (layer, slot, chunk(g) // nc, 0, 0))
    scratch_set = [
        pltpu.VMEM((A_QK_W, L), _BF16), pltpu.VMEM((L, A_QK_W), _BF16),
        pltpu.VMEM((A_HEADS, AV_ROWS, L), _BF16), pltpu.VMEM((A_V_W, L), _F32),
        pltpu.VMEM((L, LANES), _F32), pltpu.VMEM((2 * A_HEADS, L), _F32),
    ]
    return pl.pallas_call(
        functools.partial(_mlstm_layer_kernel, n_chunks=nc),
        grid=(total + 1,),
        in_specs=[
            tok(proj_chunk), tok(rec_chunk),
            modrow(0, proj_chunk), modrow(1, proj_chunk), modrow(2, rec_chunk),
            _const_spec((A_QK_W, D_MODEL)), _const_spec((D_MODEL, A_QK_W)),
            _const_spec((A_V_W, D_MODEL)), _const_spec((A_V_W, D_MODEL)),
            _const_spec((D_MODEL, LANES)), _const_spec((2 * A_HEADS, D_MODEL)),
            _const_spec((1, LANES)), _const_spec((2 * A_HEADS, 1)),
            _const_spec((A_V_W, L)), _const_spec((A_V_W, D_MODEL)),
        ],
        out_specs=tok(rec_chunk),
        out_shape=jax.ShapeDtypeStruct(x.shape, _F32),
        scratch_shapes=scratch_set + scratch_set + [
            pltpu.VMEM((A_HEADS, AV_ROWS, A_DQK), _F32),
            pltpu.VMEM((A_HEADS, 1, LANES), _F32),
            pltpu.VMEM((A_V_W, L), _BF16),
        ],
        compiler_params=_params(1),
        name="mlstm_layer",
    )(x, x, mod, mod, mod, wqt, wk, wvt, wot, wg, wgate.T, bcol, brow, gain,
      w_out.astype(_BF16))


def _mlp_kernel(x_ref, sh_ref, sc_ref, g_ref, w1_ref, w2_ref, fgain_ref, out_ref, u_scr, *,
                final_norm):
    x = x_ref[...]
    h = _modulate(x, sh_ref[...], sc_ref[...]).astype(_BF16)
    for c in range(D_FF // FF_CHUNK):
        u = jnp.maximum(_dot(h, w1_ref[:, c * FF_CHUNK:(c + 1) * FF_CHUNK]), 0.0)
        u_scr[:, c * FF_CHUNK:(c + 1) * FF_CHUNK] = (u * u).astype(_BF16)
    y = x + g_ref[...] * _dot(u_scr[...], w2_ref[...])
    if final_norm:
        y = _rms(y) * fgain_ref[...]
    out_ref[...] = y


def _layer_spec(layer, rows, cols):
    return pl.BlockSpec((None, rows, cols), lambda *_: (layer, 0, 0),
                        pipeline_mode=pl.Buffered(1))


def _mlp(x, mod, layer, w1, w2, final_gain, final_norm):
    batch, seq, _ = x.shape
    tm = MLP_TILE
    tok = pl.BlockSpec((None, tm, D_MODEL), lambda b, i: (b, i, 0))
    return pl.pallas_call(
        functools.partial(_mlp_kernel, final_norm=final_norm),
        grid=(batch, seq // tm),
        in_specs=[
            tok, _mod_spec(layer, 3), _mod_spec(layer, 4), _mod_spec(layer, 5),
            _layer_spec(layer, D_MODEL, D_FF), _layer_spec(layer, D_FF, D_MODEL),
            _const_spec((1, D_MODEL)),
        ],
        out_specs=tok,
        out_shape=jax.ShapeDtypeStruct(x.shape, _F32),
        scratch_shapes=[pltpu.VMEM((tm, D_FF), _BF16)],
        compiler_params=_params(2),
        name="mlp",
    )(x, mod, mod, mod, w1, w2, final_gain.reshape(1, D_MODEL).astype(_F32))


def _split_bf16(x):
    hi = x.astype(_BF16).astype(_F32)
    rest = x - hi
    mid = rest.astype(_BF16).astype(_F32)
    lo = (rest - mid).astype(_BF16).astype(_F32)
    return hi, mid, lo


N_BIAS = 3
MAX_LOG2_WEIGHT = 60.0
BOUND_SLACK = 1.0 + 2.0 ** -6


def _bias_placement():
    place = np.zeros((LANES, B_HEADS * LANES), np.float32)
    for hd in range(B_HEADS):
        spare = hd * LANES + (0 if hd % 2 else B_DH)
        for piece in range(N_BIAS):
            place[piece * B_HEADS + hd, spare + piece] = -1.0
            place[N_BIAS * B_HEADS, spare + N_BIAS + piece] = 1.0
    return jnp.asarray(place, _BF16)


def _fox_kv_kernel(x_ref, gain_ref, wk_ref, wvt_ref, wf_ref, wft_ref, bcol_ref, brow_ref,
                   place_ref, kaug_ref, vt_ref, frow_ref, kmax2_ref, ccol_scr, crow_scr):
    tm = KV_TILE

    @pl.when(pl.program_id(1) == 0)
    def _():
        ccol_scr[...] = jnp.zeros_like(ccol_scr)
        crow_scr[...] = jnp.zeros_like(crow_scr)
        kmax2_ref[...] = jnp.zeros_like(kmax2_ref)

    ones = jnp.ones((V_ROWS - B_DH, tm), _BF16)
    lane = lax.broadcasted_iota(jnp.int32, (tm, LANES), 1)
    pair_lane = lax.broadcasted_iota(jnp.int32, (1, LANES), 1)
    carry_col = ccol_scr[...]
    carry_row = crow_scr[:, 0:1]
    kmax2 = kmax2_ref[...]
    for blk in range(KV_STEP):
        h = (_rms(x_ref[blk * tm:(blk + 1) * tm, :]) * gain_ref[...]).astype(_BF16)
        vt = _dot_nt(wvt_ref[...], h).astype(_BF16)
        for hd in range(B_HEADS):
            vt_ref[blk, hd, 0:B_DH, :] = vt[hd * B_DH:(hd + 1) * B_DH, :]
            vt_ref[blk, hd, B_DH:V_ROWS, :] = ones
        k = _dot(h, wk_ref[...])
        fc = _segment_cumsum(_log_sigmoid(_dot(h, wf_ref[...]) + bcol_ref[...]), 0, tm)
        fc = fc + carry_col
        carry_col = fc[tm - 1:tm, :]
        fr = _segment_cumsum(_log_sigmoid(_dot_nt(wft_ref[...], h) + brow_ref[...]), 1, tm)
        fr = fr + carry_row
        carry_row = fr[:, tm - 1:tm]
        frow_ref[blk] = fr * LOG2E
        hi, mid, lo = _split_bf16(fc * LOG2E)
        pieces = jnp.where(lane < B_HEADS, hi, jnp.where(
            lane < 2 * B_HEADS, pltpu.roll(mid, B_HEADS, 1), jnp.where(
                lane < 3 * B_HEADS, pltpu.roll(lo, 2 * B_HEADS, 1), jnp.where(
                    lane == 3 * B_HEADS, 1.0, 0.0)))).astype(_BF16)
        bias = _dot(pieces, place_ref[...])
        for hd in range(B_HEADS):
            pair, odd = divmod(hd, 2)
            own = (lane >= B_DH) if odd else (lane < B_DH)
            kaug_ref[blk, hd] = jnp.where(
                own, k[:, pair * LANES:(pair + 1) * LANES],
                bias[:, hd * LANES:(hd + 1) * LANES]).astype(_BF16)
        for pair in range(B_HEADS // 2):
            kb = k[:, pair * LANES:(pair + 1) * LANES]
            n2 = jnp.max(jnp.sum(kb * kb, axis=1, keepdims=True), axis=0, keepdims=True)
            kmax2 = jnp.where(pair_lane == pair, jnp.maximum(kmax2, n2), kmax2)
    ccol_scr[...] = carry_col
    crow_scr[...] = jnp.broadcast_to(carry_row, crow_scr.shape)
    kmax2_ref[...] = kmax2


def _fox_kv(x, kv_gain, w_kv, fg_bias):
    batch, seq, _ = x.shape
    tm = KV_TILE
    w = w_kv.astype(_BF16)
    wk, wvt, wfg = w[:, :D_MODEL], w[:, D_MODEL:2 * D_MODEL].T, w[:, 2 * D_MODEL:]
    wf = jnp.pad(wfg, ((0, 0), (0, LANES - B_HEADS)))
    wft = wfg.T
    bias = fg_bias.astype(_F32)
    bcol = jnp.pad(bias, (0, LANES - B_HEADS)).reshape(1, LANES)
    brow = bias.reshape(B_HEADS, 1)
    ks = KV_STEP
    return pl.pallas_call(
        _fox_kv_kernel,
        grid=(batch, seq // (ks * tm)),
        in_specs=[
            pl.BlockSpec((None, ks * tm, D_MODEL), lambda b, i: (b, i, 0)),
            _const_spec((1, D_MODEL)),
            _const_spec((D_MODEL, D_MODEL)), _const_spec((D_MODEL, D_MODEL)),
            _const_spec((D_MODEL, LANES)), _const_spec((B_HEADS, D_MODEL)),
            _const_spec((1, LANES)), _const_spec((B_HEADS, 1)),
            _const_spec((LANES, B_HEADS * LANES)),
        ],
        out_specs=[
            pl.BlockSpec((None, ks, B_HEADS, tm, LANES), lambda b, i: (b, i, 0, 0, 0)),
            pl.BlockSpec((None, ks, B_HEADS, V_ROWS, tm), lambda b, i: (b, i, 0, 0, 0)),
            pl.BlockSpec((None, ks, B_HEADS, tm), lambda b, i: (b, i, 0, 0)),
            pl.BlockSpec((None, 1, LANES), lambda b, i: (b, 0, 0)),
        ],
        out_shape=[
            jax.ShapeDtypeStruct((batch, seq // tm, B_HEADS, tm, LANES), _BF16),
            jax.ShapeDtypeStruct((batch, seq // tm, B_HEADS, V_ROWS, tm), _BF16),
            jax.ShapeDtypeStruct((batch, seq // tm, B_HEADS, tm), _F32),
            jax.ShapeDtypeStruct((batch, 1, LANES), _F32),
        ],
        scratch_shapes=[pltpu.VMEM((1, LANES), _F32), pltpu.VMEM((B_HEADS, LANES), _F32)],
        compiler_params=_params(2),
        name="fox_kv",
    )(x, kv_gain.reshape(1, D_MODEL).astype(_F32), wk, wvt, wf, wft, bcol, brow,
      _bias_placement())


def _fox_attn_kernel(x_ref, sh_ref, sc_ref, g1_ref, wqt_ref, kaug_ref, vt_ref, frow_ref,
                     kmax2_ref, wout_ref, out_ref, qaug_scr, m_scr, acc_scr, o_scr):
    assert Q_STEP % 2 == 0
    for s in range(Q_STEP):
        rows = pl.ds(s * Q_TILE, Q_TILE)
        _fox_attn_block(pl.program_id(1) * Q_STEP + s, s % 2, x_ref.at[rows], sh_ref, sc_ref,
                        g1_ref, wqt_ref, kaug_ref, vt_ref, frow_ref, kmax2_ref, wout_ref,
                        out_ref.at[rows], qaug_scr, m_scr, acc_scr, o_scr)


def _fox_attn_block(qi, qi_parity, x_ref, sh_ref, sc_ref, g1_ref, wqt_ref, kaug_ref, vt_ref,
                    frow_ref, kmax2_ref, wout_ref, out_ref, qaug_scr, m_scr, acc_scr, o_scr):
    tq, tk = Q_TILE, KV_TILE
    n_groups = B_HEADS // HEAD_GROUP
    h = _modulate(x_ref[...], sh_ref[...], sc_ref[...]).astype(_BF16)
    ft = frow_ref[qi]
    sub = lax.broadcasted_iota(jnp.int32, (B_DH, tq), 0)
    slab = Q_PROJ_HEADS * B_DH
    qt = [(_dot_nt(wqt_ref[r * slab:(r + 1) * slab, :], h)
           * (B_DH ** -0.5 * LOG2E)).astype(_BF16)
          for r in range(B_HEADS // Q_PROJ_HEADS)]
    kmax2 = kmax2_ref[...]
    score_bound = []
    for hd in range(B_HEADS):
        hi, mid, lo = _split_bf16(ft[hd:hd + 1, :])
        bias = jnp.where(sub < N_BIAS, 1.0, jnp.where(sub == N_BIAS, hi, jnp.where(
            sub == N_BIAS + 1, mid, jnp.where(sub == N_BIAS + 2, lo, 0.0)))).astype(_BF16)
        r, off = divmod(hd, Q_PROJ_HEADS)
        qh = qt[r][off * B_DH:(off + 1) * B_DH, :]
        lo_half, hi_half = (bias, qh) if hd % 2 else (qh, bias)
        qaug_scr[hd, 0:B_DH, :] = lo_half
        qaug_scr[hd, B_DH:2 * B_DH, :] = hi_half
        qf = qh.astype(_F32)
        qn2 = jnp.sum(qf * qf, axis=0, keepdims=True)
        score_bound.append(jnp.sqrt(qn2 * kmax2[:, hd // 2:hd // 2 + 1]) * BOUND_SLACK + 1.0)
    m_scr[...] = jnp.full(m_scr.shape, -jnp.inf, _F32)
    acc_scr[...] = jnp.zeros_like(acc_scr)

    def kv_blocks(blocks, mode):
        if mode == "diagonal":
            keep = (lax.broadcasted_iota(jnp.int32, (tk, tq), 1)
                    >= lax.broadcasted_iota(jnp.int32, (tk, tq), 0))
        def scores(unit):
            j, grp = unit
            return [_dot(kaug_ref[j, hd], qaug_scr[hd])
                    for hd in range(grp * HEAD_GROUP, (grp + 1) * HEAD_GROUP)]

        units = [(j, grp) for j in blocks for grp in range(n_groups)]
        s_next = scores(units[0])
        for n, (j, grp) in enumerate(units):
            heads = range(grp * HEAD_GROUP, (grp + 1) * HEAD_GROUP)
            s_cur = s_next
            if n + 1 < len(units):
                s_next = scores(units[n + 1])
            if mode == "fixed":
                for i, hd in enumerate(heads):
                    p_i = jnp.exp2(s_cur[i] - m_scr[hd]).astype(_BF16)
                    acc_scr[hd] = acc_scr[hd] + _dot(vt_ref[j, hd], p_i)
                continue
            if mode == "diagonal":
                s_cur = [jnp.where(keep, s, -jnp.inf) for s in s_cur]
            m_old = [m_scr[hd] for hd in heads]
            m_new = [jnp.maximum(mo, jnp.max(s, axis=0, keepdims=True))
                     for mo, s in zip(m_old, s_cur)]
            alpha = [jnp.exp2(mo - mn) for mo, mn in zip(m_old, m_new)]
            p = [jnp.exp2(s - mn) for s, mn in zip(s_cur, m_new)]
            for i, hd in enumerate(heads):
                m_scr[hd] = m_new[i]
                acc_scr[hd] = alpha[i] * acc_scr[hd] + _dot(vt_ref[j, hd], p[i].astype(_BF16))

    kv_blocks([qi], "diagonal")
    margin = m_scr[0] + MAX_LOG2_WEIGHT - score_bound[0]
    for hd in range(1, B_HEADS):
        margin = jnp.minimum(margin, m_scr[hd] + MAX_LOG2_WEIGHT - score_bound[hd])
    bounded = jnp.min(margin) >= 0.0

    @pl.when(bounded)
    def _():
        def pair(t, carry):
            kv_blocks([2 * t, 2 * t + 1], "fixed")
            return carry
        lax.fori_loop(0, qi // 2, pair, 0)
        if qi_parity:
            kv_blocks([qi - 1], "fixed")

    @pl.when(jnp.logical_not(bounded))
    def _():
        def single(j, carry):
            kv_blocks([j], "rescale")
            return carry
        lax.fori_loop(0, qi, single, 0)

    for hd in range(B_HEADS):
        o_scr[hd * B_DH:(hd + 1) * B_DH, :] = (
            acc_scr[hd, 0:B_DH, :] / acc_scr[hd, B_DH:B_DH + 1, :]).astype(_BF16)
    out_ref[...] = x_ref[...] + g1_ref[...] * _dot_tn(o_scr[...], wout_ref[...])


def _fox_attn(x, mod, layer, w_q, w_out, kaug, vt, frow, kmax2):
    batch, seq, _ = x.shape
    tq, tk = Q_TILE, KV_TILE
    nkv = seq // tk
    tok = pl.BlockSpec((None, Q_STEP * tq, D_MODEL), lambda b, i: (b, i, 0))
    return pl.pallas_call(
        _fox_attn_kernel,
        grid=(batch, seq // (Q_STEP * tq)),
        in_specs=[
            tok, _mod_spec(layer, 0), _mod_spec(layer, 1), _mod_spec(layer, 2),
            _const_spec((D_MODEL, D_MODEL)),
            pl.BlockSpec((None, nkv, B_HEADS, tk, LANES), lambda b, i: (b, 0, 0, 0, 0)),
            pl.BlockSpec((None, nkv, B_HEADS, V_ROWS, tk), lambda b, i: (b, 0, 0, 0, 0)),
            pl.BlockSpec((None, nkv, B_HEADS, tk), lambda b, i: (b, 0, 0, 0)),
            pl.BlockSpec((None, 1, LANES), lambda b, i: (b, 0, 0)),
            _const_spec((D_MODEL, D_MODEL)),
        ],
        out_specs=tok,
        out_shape=jax.ShapeDtypeStruct(x.shape, _F32),
        scratch_shapes=[
            pltpu.VMEM((B_HEADS, 2 * B_DH, tq), _BF16),
            pltpu.VMEM((B_HEADS, 1, tq), _F32),
            pltpu.VMEM((B_HEADS, V_ROWS, tq), _F32), pltpu.VMEM((D_MODEL, tq), _BF16),
        ],
        compiler_params=_params(2),
        name="fox_attn",
    )(x, mod, mod, mod, w_q.T.astype(_BF16), kaug, vt, frow, kmax2, w_out.astype(_BF16))


def kernel(x, c, ada_w, ada_b, a_w_in, a_b_i, a_b_f, a_head_gain, a_w_out, kv_gain, b_w_kv,
           b_fg_bias, b_w_q, b_w_out, mlp_w1, mlp_w2, final_gain):
    batch, seq, d = x.shape
    assert d == D_MODEL and Q_TILE == KV_TILE
    assert all(seq % t == 0
               for t in (MLP_TILE, MLSTM_CHUNK, Q_STEP * Q_TILE, KV_STEP * KV_TILE))
    mod = _adaln_table(c, ada_w, ada_b).reshape(DEPTH, 6, batch, 1, D_MODEL)
    w1_all, w2_all = mlp_w1.astype(_BF16), mlp_w2.astype(_BF16)
    shared = None
    for l in range(DEPTH):
        if l < N_A_LAYERS:
            x = _mlstm_layer(x, mod, l, a_w_in[l], a_b_i[l], a_b_f[l], a_head_gain[l], a_w_out[l])
        else:
            if shared is None:
                shared = _fox_kv(x, kv_gain, b_w_kv, b_fg_bias)
            j = l - N_A_LAYERS
            x = _fox_attn(x, mod, l, b_w_q[j], b_w_out[j], *shared)
        x = _mlp(x, mod, l, w1_all, w2_all, final_gain, final_norm=(l == DEPTH - 1))
    return x
```

```python
import functools

import jax
import jax.numpy as jnp
import numpy as np
from jax import lax
from jax.experimental import pallas as pl
from jax.experimental.pallas import tpu as pltpu

D_MODEL = 1024
DEPTH = 4
N_A_LAYERS = DEPTH // 2
A_HEADS = 4
A_DV = D_MODEL // A_HEADS
A_DQK = A_DV // 2
A_QK_W = A_HEADS * A_DQK
A_V_W = A_HEADS * A_DV
B_HEADS = 16
B_DH = D_MODEL // B_HEADS
D_FF = 4 * D_MODEL
EPS = 1e-6
LOG2E = 1.4426950408889634
LANES = 128
BF16_ROWS = 16
AV_ROWS = A_DV + BF16_ROWS
V_ROWS = B_DH + BF16_ROWS

MLSTM_CHUNK = 256
KV_TILE = 256
KV_STEP = 4
Q_TILE = 256
Q_STEP = 2
HEAD_GROUP = 8
Q_PROJ_HEADS = 4
MLP_TILE = 1024
FF_CHUNK = 512
VMEM_LIMIT = 56 * 1024 * 1024

_BF16 = jnp.bfloat16
_F32 = jnp.float32


def _dot(a, b):
    return jnp.dot(a, b, preferred_element_type=_F32)


def _dot_nt(a, b):
    return lax.dot_general(a, b, (((1,), (1,)), ((), ())), preferred_element_type=_F32)


def _dot_tn(a, b):
    return lax.dot_general(a, b, (((0,), (0,)), ((), ())), preferred_element_type=_F32)


def _rms(x):
    return x * lax.rsqrt(jnp.mean(x * x, axis=-1, keepdims=True) + EPS)


def _modulate(x, shift, scale):
    return _rms(x) * (1.0 + scale) + shift


def _log_sigmoid(z):
    return jnp.minimum(z, 0.0) - jnp.log1p(jnp.exp(-jnp.abs(z)))


def _segment_cumsum(x, axis, seg):
    pos = lax.broadcasted_iota(jnp.int32, x.shape, axis) & (seg - 1)
    k = 1
    while k < seg:
        x = x + jnp.where(pos >= k, pltpu.roll(x, k, axis), 0.0)
        k *= 2
    return x


def _params(n_grid):
    return pltpu.CompilerParams(dimension_semantics=("arbitrary",) * n_grid,
                                vmem_limit_bytes=VMEM_LIMIT)


def _const_spec(shape):
    return pl.BlockSpec(shape, lambda *_: (0,) * len(shape), pipeline_mode=pl.Buffered(1))


def _mod_spec(layer, slot):
    return pl.BlockSpec((None, None, None, 1, D_MODEL), lambda b, i: (layer, slot, b, 0, 0))


def _adaln_kernel(c_ref, w_ref, b_ref, o_ref):
    c = c_ref[...]
    cond = (c * jax.nn.sigmoid(c)).astype(_BF16)
    o_ref[...] = _dot(cond, w_ref[...].astype(_BF16)) + b_ref[...]


def _adaln_table(c, ada_w, ada_b):
    batch = c.shape[0]
    return pl.pallas_call(
        _adaln_kernel,
        grid=(DEPTH, 6),
        in_specs=[
            pl.BlockSpec((batch, D_MODEL), lambda l, j: (0, 0)),
            pl.BlockSpec((None, D_MODEL, D_MODEL), lambda l, j: (l, 0, j)),
            pl.BlockSpec((None, None, 1, D_MODEL), lambda l, j: (l, j, 0, 0)),
        ],
        out_specs=pl.BlockSpec((None, None, batch, D_MODEL), lambda l, j: (l, j, 0, 0)),
        out_shape=jax.ShapeDtypeStruct((DEPTH, 6, batch, D_MODEL), _F32),
        compiler_params=_params(2),
        name="adaln_table",
    )(c, ada_w, ada_b.reshape(DEPTH, 6, 1, D_MODEL))


def _mlstm_project_qkg(h, w, dst):
    wqt_ref, wk_ref, _, _, wg_ref, wgt_ref, bcol_ref, brow_ref = w
    qt_s, k_s, _, _, gcol_s, grow_s = dst
    ct = MLSTM_CHUNK
    qt_s[...] = (_dot_nt(wqt_ref[...], h) * (A_DQK ** -0.5)).astype(_BF16)
    k_s[...] = _dot(h, wk_ref[...]).astype(_BF16)
    z = _dot(h, wg_ref[...]) + bcol_ref[...]
    bc = _segment_cumsum(_log_sigmoid(z), 0, ct)
    gcol_s[...] = (z - pltpu.roll(bc, LANES - A_HEADS, 1)) * LOG2E
    zr = _dot_nt(wgt_ref[...], h) + brow_ref[...]
    br = _segment_cumsum(_log_sigmoid(zr), 1, ct)
    sub = lax.broadcasted_iota(jnp.int32, zr.shape, 0)
    grow_s[...] = jnp.where(sub < A_HEADS, zr, br) * LOG2E


def _mlstm_project_v(h, w, dst):
    vt = _dot_nt(w[2][...], h).astype(_BF16)
    for hd in range(A_HEADS):
        dst[2][hd, 0:A_DV, :] = vt[hd * A_DV:(hd + 1) * A_DV, :]


def _mlstm_project_o(h, w, dst):
    dst[3][...] = _dot_nt(w[3][...], h)


def _mlstm_layer_kernel(xp_ref, xc_ref, sh_ref, sc_ref, g1_ref, wqt_ref, wk_ref, wvt_ref, wot_ref,
                        wg_ref, wgt_ref, bcol_ref, brow_ref, gain_ref, wout_ref, out_ref,
                        qt_a, k_a, vt_a, ot_a, gcol_a, grow_a, qt_b, k_b, vt_b, ot_b, gcol_b, grow_b,
                        cn_scr, m_scr, z_scr, *, n_chunks):
    L = MLSTM_CHUNK
    heads = range(A_HEADS)
    g = pl.program_id(0)
    weights = (wqt_ref, wk_ref, wvt_ref, wot_ref, wg_ref, wgt_ref, bcol_ref, brow_ref)
    set_a = (qt_a, k_a, vt_a, ot_a, gcol_a, grow_a)
    set_b = (qt_b, k_b, vt_b, ot_b, gcol_b, grow_b)

    @pl.when(g == 0)
    def _():
        for ref in set_b:
            ref[...] = jnp.zeros_like(ref)
        for vt_s in (vt_a, vt_b):
            vt_s[:, A_DV:AV_ROWS, :] = jnp.ones((A_HEADS, AV_ROWS - A_DV, L), _BF16)
        cn_scr[...] = jnp.zeros_like(cn_scr)
        m_scr[...] = jnp.zeros_like(m_scr)

    fresh = lax.rem(jnp.maximum(g - 1, 0), n_chunks) == 0
    causal = (lax.broadcasted_iota(jnp.int32, (L, L), 0)
              <= lax.broadcasted_iota(jnp.int32, (L, L), 1))

    def step(dst, src):
        qt_s, k_s, vt_s, ot_s, gcol_s, grow_s = src
        gcol = gcol_s[...]
        grow = grow_s[...]
        kh = [k_s[:, h * A_DQK:(h + 1) * A_DQK] for h in heads]
        qt = [qt_s[h * A_DQK:(h + 1) * A_DQK, :] for h in heads]
        cn_prev = [jnp.where(fresh, 0.0, cn_scr[h]) for h in heads]
        m_prev = [jnp.where(fresh, 0.0, m_scr[h][:, 0:1]) for h in heads]
        st = [_dot(kh[h], qt[h]) for h in heads]
        inter_mm = [_dot(cn_prev[h].astype(_BF16), qt[h]) for h in heads]

        i_row = [grow[h:h + 1, :] for h in heads]
        b_row = [grow[A_HEADS + h:A_HEADS + h + 1, :] for h in heads]
        d = [jnp.where(causal, gcol[:, h:h + 1] + b_row[h], -jnp.inf) for h in heads]
        inter = [b_row[h] + m_prev[h] for h in heads]
        m_t = [jnp.maximum(inter[h], jnp.max(d[h], axis=0, keepdims=True)) for h in heads]
        a = [(jnp.exp2(d[h] - m_t[h]) * st[h]).astype(_BF16) for h in heads]
        w_inter = [jnp.exp2(inter[h] - m_t[h]) for h in heads]

        hp = _modulate(xp_ref[...], sh_ref[...], sc_ref[...]).astype(_BF16)
        _mlstm_project_qkg(hp, weights, dst)
        _mlstm_project_v(hp, weights, dst)
        nd = [w_inter[h] * inter_mm[h] + _dot(vt_s[h], a[h]) for h in heads]
        _mlstm_project_o(hp, weights, dst)

        b_last = [b_row[h][:, L - 1:L] for h in heads]
        dl = [b_last[h] - b_row[h] + i_row[h] for h in heads]
        m_new = [jnp.maximum(b_last[h] + m_prev[h], jnp.max(dl[h], axis=1, keepdims=True))
                 for h in heads]
        for h in heads:
            rows = slice(h * A_DV, (h + 1) * A_DV)
            den = nd[h][A_DV:A_DV + 1, :]
            ht = nd[h][0:A_DV, :] / jnp.maximum(jnp.abs(den), jnp.exp2(-m_t[h]))
            ht = (ht * lax.rsqrt(jnp.mean(ht * ht, axis=0, keepdims=True) + EPS)
                  * gain_ref[rows, :])
            z_scr[rows, :] = (jax.nn.sigmoid(ot_s[rows, :]) * ht).astype(_BF16)
        out_ref[...] = xc_ref[...] + g1_ref[...] * _dot_tn(z_scr[...], wout_ref[...])
        for h in heads:
            vw = (vt_s[h].astype(_F32) * jnp.exp2(dl[h] - m_new[h])).astype(_BF16)
            decay = jnp.exp2(b_last[h] + m_prev[h] - m_new[h])
            cn_scr[h] = decay * cn_prev[h] + _dot(vw, kh[h])
            m_scr[h] = jnp.broadcast_to(m_new[h], (1, LANES))

    @pl.when(lax.rem(g, 2) == 0)
    def _():
        step(set_a, set_b)

    @pl.when(lax.rem(g, 2) == 1)
    def _():
        step(set_b, set_a)


def _mlstm_layer(x, mod, layer, w_in, b_i, b_f, head_gain, w_out):
    batch, seq, _ = x.shape
    L = MLSTM_CHUNK
    nc = seq // L
    total = batch * nc
    w = w_in.astype(_BF16)
    wqt, wk = w[:, :A_QK_W].T, w[:, A_QK_W:2 * A_QK_W]
    wvt = w[:, 2 * A_QK_W:2 * A_QK_W + A_V_W].T
    wot = w[:, 2 * A_QK_W + A_V_W:2 * A_QK_W + 2 * A_V_W].T
    wgate = w[:, 2 * A_QK_W + 2 * A_V_W:]
    wg = jnp.pad(wgate, ((0, 0), (0, LANES - 2 * A_HEADS)))
    bias = jnp.concatenate([b_i, b_f]).astype(_F32)
    bcol = jnp.pad(bias, (0, LANES - 2 * A_HEADS)).reshape(1, LANES)
    brow = bias.reshape(2 * A_HEADS, 1)
    gain = jnp.broadcast_to(head_gain.reshape(A_V_W, 1).astype(_F32), (A_V_W, L))

    proj_chunk = lambda g: jnp.minimum(g, total - 1)
    rec_chunk = lambda g: jnp.maximum(g - 1, 0)
    tok = lambda chunk: pl.BlockSpec(
        (None, L, D_MODEL), lambda g: (chunk(g) // nc, lax.rem(chunk(g), nc), 0))
    modrow = lambda slot, chunk: pl.BlockSpec(
        (None, None, None, 1, D_MODEL), lambda g: (layer, slot, chunk(g) // nc, 0, 0))
    scratch_set = [
        pltpu.VMEM((A_QK_W, L), _BF16), pltpu.VMEM((L, A_QK_W), _BF16),
        pltpu.VMEM((A_HEADS, AV_ROWS, L), _BF16), pltpu.VMEM((A_V_W, L), _F32),
        pltpu.VMEM((L, LANES), _F32), pltpu.VMEM((2 * A_HEADS, L), _F32),
    ]
    return pl.pallas_call(
        functools.partial(_mlstm_layer_kernel, n_chunks=nc),
        grid=(total + 1,),
        in_specs=[
            tok(proj_chunk), tok(rec_chunk),
            modrow(0, proj_chunk), modrow(1, proj_chunk), modrow(2, rec_chunk),
            _const_spec((A_QK_W, D_MODEL)), _const_spec((D_MODEL, A_QK_W)),
            _const_spec((A_V_W, D_MODEL)), _const_spec((A_V_W, D_MODEL)),
            _const_spec((D_MODEL, LANES)), _const_spec((2 * A_HEADS, D_MODEL)),
            _const_spec((1, LANES)), _const_spec((2 * A_HEADS, 1)),
            _const_spec((A_V_W, L)), _const_spec((A_V_W, D_MODEL)),
        ],
        out_specs=tok(rec_chunk),
        out_shape=jax.ShapeDtypeStruct(x.shape, _F32),
        scratch_shapes=scratch_set + scratch_set + [
            pltpu.VMEM((A_HEADS, AV_ROWS, A_DQK), _F32),
            pltpu.VMEM((A_HEADS, 1, LANES), _F32),
            pltpu.VMEM((A_V_W, L), _BF16),
        ],
        compiler_params=_params(1),
        name="mlstm_layer",
    )(x, x, mod, mod, mod, wqt, wk, wvt, wot, wg, wgate.T, bcol, brow, gain,
      w_out.astype(_BF16))


def _mlp_kernel(x_ref, sh_ref, sc_ref, g_ref, w1_ref, w2_ref, fgain_ref, out_ref, u_scr, *,
                final_norm):
    x = x_ref[...]
    h = _modulate(x, sh_ref[...], sc_ref[...]).astype(_BF16)
    for c in range(D_FF // FF_CHUNK):
        u = jnp.maximum(_dot(h, w1_ref[:, c * FF_CHUNK:(c + 1) * FF_CHUNK]), 0.0)
        u_scr[:, c * FF_CHUNK:(c + 1) * FF_CHUNK] = (u * u).astype(_BF16)
    y = x + g_ref[...] * _dot(u_scr[...], w2_ref[...])
    if final_norm:
        y = _rms(y) * fgain_ref[...]
    out_ref[...] = y


def _layer_spec(layer, rows, cols):
    return pl.BlockSpec((None, rows, cols), lambda *_: (layer, 0, 0),
                        pipeline_mode=pl.Buffered(1))


def _mlp(x, mod, layer, w1, w2, final_gain, final_norm):
    batch, seq, _ = x.shape
    tm = MLP_TILE
    tok = pl.BlockSpec((None, tm, D_MODEL), lambda b, i: (b, i, 0))
    return pl.pallas_call(
        functools.partial(_mlp_kernel, final_norm=final_norm),
        grid=(batch, seq // tm),
        in_specs=[
            tok, _mod_spec(layer, 3), _mod_spec(layer, 4), _mod_spec(layer, 5),
            _layer_spec(layer, D_MODEL, D_FF), _layer_spec(layer, D_FF, D_MODEL),
            _const_spec((1, D_MODEL)),
        ],
        out_specs=tok,
        out_shape=jax.ShapeDtypeStruct(x.shape, _F32),
        scratch_shapes=[pltpu.VMEM((tm, D_FF), _BF16)],
        compiler_params=_params(2),
        name="mlp",
    )(x, mod, mod, mod, w1, w2, final_gain.reshape(1, D_MODEL).astype(_F32))


def _split_bf16(x):
    hi = x.astype(_BF16).astype(_F32)
    rest = x - hi
    mid = rest.astype(_BF16).astype(_F32)
    lo = (rest - mid).astype(_BF16).astype(_F32)
    return hi, mid, lo


N_BIAS = 3
MAX_LOG2_WEIGHT = 60.0
BOUND_SLACK = 1.0 + 2.0 ** -6


def _bias_placement():
    place = np.zeros((LANES, B_HEADS * LANES), np.float32)
    for hd in range(B_HEADS):
        spare = hd * LANES + (0 if hd % 2 else B_DH)
        for piece in range(N_BIAS):
            place[piece * B_HEADS + hd, spare + piece] = -1.0
            place[N_BIAS * B_HEADS, spare + N_BIAS + piece] = 1.0
    return jnp.asarray(place, _BF16)


def _fox_kv_kernel(x_ref, gain_ref, wk_ref, wvt_ref, wf_ref, wft_ref, bcol_ref, brow_ref,
                   place_ref, kaug_ref, vt_ref, frow_ref, kmax2_ref, ccol_scr, crow_scr):
    tm = KV_TILE

    @pl.when(pl.program_id(1) == 0)
    def _():
        ccol_scr[...] = jnp.zeros_like(ccol_scr)
        crow_scr[...] = jnp.zeros_like(crow_scr)
        kmax2_ref[...] = jnp.zeros_like(kmax2_ref)

    ones = jnp.ones((V_ROWS - B_DH, tm), _BF16)
    lane = lax.broadcasted_iota(jnp.int32, (tm, LANES), 1)
    pair_lane = lax.broadcasted_iota(jnp.int32, (1, LANES), 1)
    carry_col = ccol_scr[...]
    carry_row = crow_scr[:, 0:1]
    kmax2 = kmax2_ref[...]
    def project(blk):
        nonlocal carry_col, carry_row
        h = (_rms(x_ref[blk * tm:(blk + 1) * tm, :]) * gain_ref[...]).astype(_BF16)
        vt = _dot_nt(wvt_ref[...], h).astype(_BF16)
        for hd in range(B_HEADS):
            vt_ref[blk, hd, 0:B_DH, :] = vt[hd * B_DH:(hd + 1) * B_DH, :]
            vt_ref[blk, hd, B_DH:V_ROWS, :] = ones
        k = _dot(h, wk_ref[...])
        fc = _segment_cumsum(_log_sigmoid(_dot(h, wf_ref[...]) + bcol_ref[...]), 0, tm)
        fc = fc + carry_col
        carry_col = fc[tm - 1:tm, :]
        fr = _segment_cumsum(_log_sigmoid(_dot_nt(wft_ref[...], h) + brow_ref[...]), 1, tm)
        fr = fr + carry_row
        carry_row = fr[:, tm - 1:tm]
        frow_ref[blk] = fr * LOG2E
        return k, fc

    def augment(blk, k, fc):
        nonlocal kmax2
        hi, mid, lo = _split_bf16(fc * LOG2E)
        pieces = jnp.where(lane < B_HEADS, hi, jnp.where(
            lane < 2 * B_HEADS, pltpu.roll(mid, B_HEADS, 1), jnp.where(
                lane < 3 * B_HEADS, pltpu.roll(lo, 2 * B_HEADS, 1), jnp.where(
                    lane == 3 * B_HEADS, 1.0, 0.0)))).astype(_BF16)
        bias = _dot(pieces, place_ref[...])
        for hd in range(B_HEADS):
            pair, odd = divmod(hd, 2)
            own = (lane >= B_DH) if odd else (lane < B_DH)
            kaug_ref[blk, hd] = jnp.where(
                own, k[:, pair * LANES:(pair + 1) * LANES],
                bias[:, hd * LANES:(hd + 1) * LANES]).astype(_BF16)
        for pair in range(B_HEADS // 2):
            kb = k[:, pair * LANES:(pair + 1) * LANES]
            n2 = jnp.max(jnp.sum(kb * kb, axis=1, keepdims=True), axis=0, keepdims=True)
            kmax2 = jnp.where(pair_lane == pair, jnp.maximum(kmax2, n2), kmax2)

    projected = project(0)
    for blk in range(KV_STEP):
        pending = projected
        if blk + 1 < KV_STEP:
            projected = project(blk + 1)
        augment(blk, *pending)
    ccol_scr[...] = carry_col
    crow_scr[...] = jnp.broadcast_to(carry_row, crow_scr.shape)
    kmax2_ref[...] = kmax2


def _fox_kv(x, kv_gain, w_kv, fg_bias):
    batch, seq, _ = x.shape
    tm = KV_TILE
    w = w_kv.astype(_BF16)
    wk, wvt, wfg = w[:, :D_MODEL], w[:, D_MODEL:2 * D_MODEL].T, w[:, 2 * D_MODEL:]
    wf = jnp.pad(wfg, ((0, 0), (0, LANES - B_HEADS)))
    wft = wfg.T
    bias = fg_bias.astype(_F32)
    bcol = jnp.pad(bias, (0, LANES - B_HEADS)).reshape(1, LANES)
    brow = bias.reshape(B_HEADS, 1)
    ks = KV_STEP
    return pl.pallas_call(
        _fox_kv_kernel,
        grid=(batch, seq // (ks * tm)),
        in_specs=[
            pl.BlockSpec((None, ks * tm, D_MODEL), lambda b, i: (b, i, 0)),
            _const_spec((1, D_MODEL)),
            _const_spec((D_MODEL, D_MODEL)), _const_spec((D_MODEL, D_MODEL)),
            _const_spec((D_MODEL, LANES)), _const_spec((B_HEADS, D_MODEL)),
            _const_spec((1, LANES)), _const_spec((B_HEADS, 1)),
            _const_spec((LANES, B_HEADS * LANES)),
        ],
        out_specs=[
            pl.BlockSpec((None, ks, B_HEADS, tm, LANES), lambda b, i: (b, i, 0, 0, 0)),
            pl.BlockSpec((None, ks, B_HEADS, V_ROWS, tm), lambda b, i: (b, i, 0, 0, 0)),
            pl.BlockSpec((None, ks, B_HEADS, tm), lambda b, i: (b, i, 0, 0)),
            pl.BlockSpec((None, 1, LANES), lambda b, i: (b, 0, 0)),
        ],
        out_shape=[
            jax.ShapeDtypeStruct((batch, seq // tm, B_HEADS, tm, LANES), _BF16),
            jax.ShapeDtypeStruct((batch, seq // tm, B_HEADS, V_ROWS, tm), _BF16),
            jax.ShapeDtypeStruct((batch, seq // tm, B_HEADS, tm), _F32),
            jax.ShapeDtypeStruct((batch, 1, LANES), _F32),
        ],
        scratch_shapes=[pltpu.VMEM((1, LANES), _F32), pltpu.VMEM((B_HEADS, LANES), _F32)],
        compiler_params=_params(2),
        name="fox_kv",
    )(x, kv_gain.reshape(1, D_MODEL).astype(_F32), wk, wvt, wf, wft, bcol, brow,
      _bias_placement())


def _fox_attn_kernel(x_ref, sh_ref, sc_ref, g1_ref, wqt_ref, kaug_ref, vt_ref, frow_ref,
                     kmax2_ref, wout_ref, out_ref, qaug_scr, m_scr, acc_scr, o_scr):
    assert Q_STEP % 2 == 0
    for s in range(Q_STEP):
        rows = pl.ds(s * Q_TILE, Q_TILE)
        _fox_attn_block(pl.program_id(1) * Q_STEP + s, s % 2, x_ref.at[rows], sh_ref, sc_ref,
                        g1_ref, wqt_ref, kaug_ref, vt_ref, frow_ref, kmax2_ref, wout_ref,
                        out_ref.at[rows], qaug_scr, m_scr, acc_scr, o_scr)


def _fox_attn_block(qi, qi_parity, x_ref, sh_ref, sc_ref, g1_ref, wqt_ref, kaug_ref, vt_ref,
                    frow_ref, kmax2_ref, wout_ref, out_ref, qaug_scr, m_scr, acc_scr, o_scr):
    tq, tk = Q_TILE, KV_TILE
    n_groups = B_HEADS // HEAD_GROUP
    h = _modulate(x_ref[...], sh_ref[...], sc_ref[...]).astype(_BF16)
    ft = frow_ref[qi]
    sub = lax.broadcasted_iota(jnp.int32, (B_DH, tq), 0)
    slab = Q_PROJ_HEADS * B_DH
    qt = [(_dot_nt(wqt_ref[r * slab:(r + 1) * slab, :], h)
           * (B_DH ** -0.5 * LOG2E)).astype(_BF16)
          for r in range(B_HEADS // Q_PROJ_HEADS)]
    kmax2 = kmax2_ref[...]
    score_bound = []
    for hd in range(B_HEADS):
        hi, mid, lo = _split_bf16(ft[hd:hd + 1, :])
        bias = jnp.where(sub < N_BIAS, 1.0, jnp.where(sub == N_BIAS, hi, jnp.where(
            sub == N_BIAS + 1, mid, jnp.where(sub == N_BIAS + 2, lo, 0.0)))).astype(_BF16)
        r, off = divmod(hd, Q_PROJ_HEADS)
        qh = qt[r][off * B_DH:(off + 1) * B_DH, :]
        lo_half, hi_half = (bias, qh) if hd % 2 else (qh, bias)
        qaug_scr[hd, 0:B_DH, :] = lo_half
        qaug_scr[hd, B_DH:2 * B_DH, :] = hi_half
        qf = qh.astype(_F32)
        qn2 = jnp.sum(qf * qf, axis=0, keepdims=True)
        score_bound.append(jnp.sqrt(qn2 * kmax2[:, hd // 2:hd // 2 + 1]) * BOUND_SLACK + 1.0)
    m_scr[...] = jnp.full(m_scr.shape, -jnp.inf, _F32)
    acc_scr[...] = jnp.zeros_like(acc_scr)

    def kv_blocks(blocks, mode):
        if mode == "diagonal":
            keep = (lax.broadcasted_iota(jnp.int32, (tk, tq), 1)
                    >= lax.broadcasted_iota(jnp.int32, (tk, tq), 0))
        def scores(unit):
            j, grp = unit
            return [_dot(kaug_ref[j, hd], qaug_scr[hd])
                    for hd in range(grp * HEAD_GROUP, (grp + 1) * HEAD_GROUP)]

        units = [(j, grp) for j in blocks for grp in range(n_groups)]
        s_next = scores(units[0])
        for n, (j, grp) in enumerate(units):
            heads = range(grp * HEAD_GROUP, (grp + 1) * HEAD_GROUP)
            s_cur = s_next
            if n + 1 < len(units):
                s_next = scores(units[n + 1])
            if mode == "fixed":
                for i, hd in enumerate(heads):
                    p_i = jnp.exp2(s_cur[i] - m_scr[hd]).astype(_BF16)
                    acc_scr[hd] = acc_scr[hd] + _dot(vt_ref[j, hd], p_i)
                continue
            if mode == "diagonal":
                s_cur = [jnp.where(keep, s, -jnp.inf) for s in s_cur]
            m_old = [m_scr[hd] for hd in heads]
            m_new = [jnp.maximum(mo, jnp.max(s, axis=0, keepdims=True))
                     for mo, s in zip(m_old, s_cur)]
            alpha = [jnp.exp2(mo - mn) for mo, mn in zip(m_old, m_new)]
            p = [jnp.exp2(s - mn) for s, mn in zip(s_cur, m_new)]
            for i, hd in enumerate(heads):
                m_scr[hd] = m_new[i]
                acc_scr[hd] = alpha[i] * acc_scr[hd] + _dot(vt_ref[j, hd], p[i].astype(_BF16))

    kv_blocks([qi], "diagonal")
    margin = m_scr[0] + MAX_LOG2_WEIGHT - score_bound[0]
    for hd in range(1, B_HEADS):
        margin = jnp.minimum(margin, m_scr[hd] + MAX_LOG2_WEIGHT - score_bound[hd])
    bounded = jnp.min(margin) >= 0.0

    @pl.when(bounded)
    def _():
        def pair(t, carry):
            kv_blocks([2 * t, 2 * t + 1], "fixed")
            return carry
        lax.fori_loop(0, qi // 2, pair, 0)
        if qi_parity:
            kv_blocks([qi - 1], "fixed")

    @pl.when(jnp.logical_not(bounded))
    def _():
        def single(j, carry):
            kv_blocks([j], "rescale")
            return carry
        lax.fori_loop(0, qi, single, 0)

    for hd in range(B_HEADS):
        o_scr[hd * B_DH:(hd + 1) * B_DH, :] = (
            acc_scr[hd, 0:B_DH, :] / acc_scr[hd, B_DH:B_DH + 1, :]).astype(_BF16)
    out_ref[...] = x_ref[...] + g1_ref[...] * _dot_tn(o_scr[...], wout_ref[...])


def _fox_attn(x, mod, layer, w_q, w_out, kaug, vt, frow, kmax2):
    batch, seq, _ = x.shape
    tq, tk = Q_TILE, KV_TILE
    nkv = seq // tk
    tok = pl.BlockSpec((None, Q_STEP * tq, D_MODEL), lambda b, i: (b, i, 0))
    return pl.pallas_call(
        _fox_attn_kernel,
        grid=(batch, seq // (Q_STEP * tq)),
        in_specs=[
            tok, _mod_spec(layer, 0), _mod_spec(layer, 1), _mod_spec(layer, 2),
            _const_spec((D_MODEL, D_MODEL)),
            pl.BlockSpec((None, nkv, B_HEADS, tk, LANES), lambda b, i: (b, 0, 0, 0, 0)),
            pl.BlockSpec((None, nkv, B_HEADS, V_ROWS, tk), lambda b, i: (b, 0, 0, 0, 0)),
            pl.BlockSpec((None, nkv, B_HEADS, tk), lambda b, i: (b, 0, 0, 0)),
            pl.BlockSpec((None, 1, LANES), lambda b, i: (b, 0, 0)),
            _const_spec((D_MODEL, D_MODEL)),
        ],
        out_specs=tok,
        out_shape=jax.ShapeDtypeStruct(x.shape, _F32),
        scratch_shapes=[
            pltpu.VMEM((B_HEADS, 2 * B_DH, tq), _BF16),
            pltpu.VMEM((B_HEADS, 1, tq), _F32),
            pltpu.VMEM((B_HEADS, V_ROWS, tq), _F32), pltpu.VMEM((D_MODEL, tq), _BF16),
        ],
        compiler_params=_params(2),
        name="fox_attn",
    )(x, mod, mod, mod, w_q.T.astype(_BF16), kaug, vt, frow, kmax2, w_out.astype(_BF16))


def kernel(x, c, ada_w, ada_b, a_w_in, a_b_i, a_b_f, a_head_gain, a_w_out, kv_gain, b_w_kv,
           b_fg_bias, b_w_q, b_w_out, mlp_w1, mlp_w2, final_gain):
    batch, seq, d = x.shape
    assert d == D_MODEL and Q_TILE == KV_TILE
    assert all(seq % t == 0
               for t in (MLP_TILE, MLSTM_CHUNK, Q_STEP * Q_TILE, KV_STEP * KV_TILE))
    mod = _adaln_table(c, ada_w, ada_b).reshape(DEPTH, 6, batch, 1, D_MODEL)
    w1_all, w2_all = mlp_w1.astype(_BF16), mlp_w2.astype(_BF16)
    shared = None
    for l in range(DEPTH):
        if l < N_A_LAYERS:
            x = _mlstm_layer(x, mod, l, a_w_in[l], a_b_i[l], a_b_f[l], a_head_gain[l], a_w_out[l])
        else:
            if shared is None:
                shared = _fox_kv(x, kv_gain, b_w_kv, b_fg_bias)
            j = l - N_A_LAYERS
            x = _fox_attn(x, mod, l, b_w_q[j], b_w_out[j], *shared)
        x = _mlp(x, mod, l, w1_all, w2_all, final_gain, final_norm=(l == DEPTH - 1))
    return x
```

```python
import functools

import jax
import jax.numpy as jnp
import numpy as np
from jax import lax
from jax.experimental import pallas as pl
from jax.experimental.pallas import tpu as pltpu

D_MODEL = 1024
DEPTH = 4
N_A_LAYERS = DEPTH // 2
A_HEADS = 4
A_DV = D_MODEL // A_HEADS
A_DQK = A_DV // 2
A_QK_W = A_HEADS * A_DQK
A_V_W = A_HEADS * A_DV
B_HEADS = 16
B_DH = D_MODEL // B_HEADS
D_FF = 4 * D_MODEL
EPS = 1e-6
LOG2E = 1.4426950408889634
LANES = 128
BF16_ROWS = 16
AV_ROWS = A_DV + BF16_ROWS
V_ROWS = B_DH + BF16_ROWS

MLSTM_CHUNK = 256
KV_TILE = 256
KV_STEP = 4
Q_TILE = 256
Q_STEP = 4
HEAD_GROUP = 8
Q_PROJ_HEADS = 4
MLP_TILE = 1024
FF_CHUNK = 512
VMEM_LIMIT = 56 * 1024 * 1024

_BF16 = jnp.bfloat16
_F32 = jnp.float32


def _dot(a, b):
    return jnp.dot(a, b, preferred_element_type=_F32)


def _dot_nt(a, b):
    return lax.dot_general(a, b, (((1,), (1,)), ((), ())), preferred_element_type=_F32)


def _dot_tn(a, b):
    return lax.dot_general(a, b, (((0,), (0,)), ((), ())), preferred_element_type=_F32)


def _rms(x):
    return x * lax.rsqrt(jnp.mean(x * x, axis=-1, keepdims=True) + EPS)


def _modulate(x, shift, scale):
    return _rms(x) * (1.0 + scale) + shift


def _log_sigmoid(z):
    return jnp.minimum(z, 0.0) - jnp.log1p(jnp.exp(-jnp.abs(z)))


def _segment_cumsum(x, axis, seg):
    pos = lax.broadcasted_iota(jnp.int32, x.shape, axis) & (seg - 1)
    k = 1
    while k < seg:
        x = x + jnp.where(pos >= k, pltpu.roll(x, k, axis), 0.0)
        k *= 2
    return x


def _params(n_grid):
    return pltpu.CompilerParams(dimension_semantics=("arbitrary",) * n_grid,
                                vmem_limit_bytes=VMEM_LIMIT)


def _const_spec(shape):
    return pl.BlockSpec(shape, lambda *_: (0,) * len(shape), pipeline_mode=pl.Buffered(1))


def _mod_spec(layer, slot):
    return pl.BlockSpec((None, None, None, 1, D_MODEL), lambda b, i: (layer, slot, b, 0, 0))


def _adaln_kernel(c_ref, w_ref, b_ref, o_ref):
    c = c_ref[...]
    cond = (c * jax.nn.sigmoid(c)).astype(_BF16)
    o_ref[...] = _dot(cond, w_ref[...].astype(_BF16)) + b_ref[...]


def _adaln_table(c, ada_w, ada_b):
    batch = c.shape[0]
    return pl.pallas_call(
        _adaln_kernel,
        grid=(DEPTH, 6),
        in_specs=[
            pl.BlockSpec((batch, D_MODEL), lambda l, j: (0, 0)),
            pl.BlockSpec((None, D_MODEL, D_MODEL), lambda l, j: (l, 0, j)),
            pl.BlockSpec((None, None, 1, D_MODEL), lambda l, j: (l, j, 0, 0)),
        ],
        out_specs=pl.BlockSpec((None, None, batch, D_MODEL), lambda l, j: (l, j, 0, 0)),
        out_shape=jax.ShapeDtypeStruct((DEPTH, 6, batch, D_MODEL), _F32),
        compiler_params=_params(2),
        name="adaln_table",
    )(c, ada_w, ada_b.reshape(DEPTH, 6, 1, D_MODEL))


def _mlstm_project_qkg(h, w, dst):
    wqt_ref, wk_ref, _, _, wg_ref, wgt_ref, bcol_ref, brow_ref = w
    qt_s, k_s, _, _, gcol_s, grow_s = dst
    ct = MLSTM_CHUNK
    qt_s[...] = (_dot_nt(wqt_ref[...], h) * (A_DQK ** -0.5)).astype(_BF16)
    k_s[...] = _dot(h, wk_ref[...]).astype(_BF16)
    z = _dot(h, wg_ref[...]) + bcol_ref[...]
    bc = _segment_cumsum(_log_sigmoid(z), 0, ct)
    gcol_s[...] = (z - pltpu.roll(bc, LANES - A_HEADS, 1)) * LOG2E
    zr = _dot_nt(wgt_ref[...], h) + brow_ref[...]
    br = _segment_cumsum(_log_sigmoid(zr), 1, ct)
    sub = lax.broadcasted_iota(jnp.int32, zr.shape, 0)
    grow_s[...] = jnp.where(sub < A_HEADS, zr, br) * LOG2E


def _mlstm_project_v(h, w, dst):
    vt = _dot_nt(w[2][...], h).astype(_BF16)
    for hd in range(A_HEADS):
        dst[2][hd, 0:A_DV, :] = vt[hd * A_DV:(hd + 1) * A_DV, :]


def _mlstm_project_o(h, w, dst):
    dst[3][...] = _dot_nt(w[3][...], h)


def _mlstm_layer_kernel(xp_ref, xc_ref, sh_ref, sc_ref, g1_ref, wqt_ref, wk_ref, wvt_ref, wot_ref,
                        wg_ref, wgt_ref, bcol_ref, brow_ref, gain_ref, wout_ref, out_ref,
                        qt_a, k_a, vt_a, ot_a, gcol_a, grow_a, qt_b, k_b, vt_b, ot_b, gcol_b, grow_b,
                        cn_scr, m_scr, z_scr, *, n_chunks):
    L = MLSTM_CHUNK
    heads = range(A_HEADS)
    g = pl.program_id(0)
    weights = (wqt_ref, wk_ref, wvt_ref, wot_ref, wg_ref, wgt_ref, bcol_ref, brow_ref)
    set_a = (qt_a, k_a, vt_a, ot_a, gcol_a, grow_a)
    set_b = (qt_b, k_b, vt_b, ot_b, gcol_b, grow_b)

    @pl.when(g == 0)
    def _():
        for ref in set_b:
            ref[...] = jnp.zeros_like(ref)
        for vt_s in (vt_a, vt_b):
            vt_s[:, A_DV:AV_ROWS, :] = jnp.ones((A_HEADS, AV_ROWS - A_DV, L), _BF16)
        cn_scr[...] = jnp.zeros_like(cn_scr)
        m_scr[...] = jnp.zeros_like(m_scr)

    fresh = lax.rem(jnp.maximum(g - 1, 0), n_chunks) == 0
    causal = (lax.broadcasted_iota(jnp.int32, (L, L), 0)
              <= lax.broadcasted_iota(jnp.int32, (L, L), 1))

    def step(dst, src):
        qt_s, k_s, vt_s, ot_s, gcol_s, grow_s = src
        gcol = gcol_s[...]
        grow = grow_s[...]
        kh = [k_s[:, h * A_DQK:(h + 1) * A_DQK] for h in heads]
        qt = [qt_s[h * A_DQK:(h + 1) * A_DQK, :] for h in heads]
        cn_prev = [jnp.where(fresh, 0.0, cn_scr[h]) for h in heads]
        m_prev = [jnp.where(fresh, 0.0, m_scr[h][:, 0:1]) for h in heads]
        st = [_dot(kh[h], qt[h]) for h in heads]
        inter_mm = [_dot(cn_prev[h].astype(_BF16), qt[h]) for h in heads]

        i_row = [grow[h:h + 1, :] for h in heads]
        b_row = [grow[A_HEADS + h:A_HEADS + h + 1, :] for h in heads]
        d = [jnp.where(causal, gcol[:, h:h + 1] + b_row[h], -jnp.inf) for h in heads]
        inter = [b_row[h] + m_prev[h] for h in heads]
        m_t = [jnp.maximum(inter[h], jnp.max(d[h], axis=0, keepdims=True)) for h in heads]
        a = [(jnp.exp2(d[h] - m_t[h]) * st[h]).astype(_BF16) for h in heads]
        w_inter = [jnp.exp2(inter[h] - m_t[h]) for h in heads]

        hp = _modulate(xp_ref[...], sh_ref[...], sc_ref[...]).astype(_BF16)
        _mlstm_project_qkg(hp, weights, dst)
        _mlstm_project_v(hp, weights, dst)
        nd = [w_inter[h] * inter_mm[h] + _dot(vt_s[h], a[h]) for h in heads]
        _mlstm_project_o(hp, weights, dst)

        b_last = [b_row[h][:, L - 1:L] for h in heads]
        dl = [b_last[h] - b_row[h] + i_row[h] for h in heads]
        m_new = [jnp.maximum(b_last[h] + m_prev[h], jnp.max(dl[h], axis=1, keepdims=True))
                 for h in heads]
        for h in heads:
            rows = slice(h * A_DV, (h + 1) * A_DV)
            den = nd[h][A_DV:A_DV + 1, :]
            ht = nd[h][0:A_DV, :] / jnp.maximum(jnp.abs(den), jnp.exp2(-m_t[h]))
            ht = (ht * lax.rsqrt(jnp.mean(ht * ht, axis=0, keepdims=True) + EPS)
                  * gain_ref[rows, :])
            z_scr[rows, :] = (jax.nn.sigmoid(ot_s[rows, :]) * ht).astype(_BF16)
        out_ref[...] = xc_ref[...] + g1_ref[...] * _dot_tn(z_scr[...], wout_ref[...])
        for h in heads:
            vw = (vt_s[h].astype(_F32) * jnp.exp2(dl[h] - m_new[h])).astype(_BF16)
            decay = jnp.exp2(b_last[h] + m_prev[h] - m_new[h])
            cn_scr[h] = decay * cn_prev[h] + _dot(vw, kh[h])
            m_scr[h] = jnp.broadcast_to(m_new[h], (1, LANES))

    @pl.when(lax.rem(g, 2) == 0)
    def _():
        step(set_a, set_b)

    @pl.when(lax.rem(g, 2) == 1)
    def _():
        step(set_b, set_a)


def _mlstm_layer(x, mod, layer, w_in, b_i, b_f, head_gain, w_out):
    batch, seq, _ = x.shape
    L = MLSTM_CHUNK
    nc = seq // L
    total = batch * nc
    w = w_in.astype(_BF16)
    wqt, wk = w[:, :A_QK_W].T, w[:, A_QK_W:2 * A_QK_W]
    wvt = w[:, 2 * A_QK_W:2 * A_QK_W + A_V_W].T
    wot = w[:, 2 * A_QK_W + A_V_W:2 * A_QK_W + 2 * A_V_W].T
    wgate = w[:, 2 * A_QK_W + 2 * A_V_W:]
    wg = jnp.pad(wgate, ((0, 0), (0, LANES - 2 * A_HEADS)))
    bias = jnp.concatenate([b_i, b_f]).astype(_F32)
    bcol = jnp.pad(bias, (0, LANES - 2 * A_HEADS)).reshape(1, LANES)
    brow = bias.reshape(2 * A_HEADS, 1)
    gain = jnp.broadcast_to(head_gain.reshape(A_V_W, 1).astype(_F32), (A_V_W, L))

    proj_chunk = lambda g: jnp.minimum(g, total - 1)
    rec_chunk = lambda g: jnp.maximum(g - 1, 0)
    tok = lambda chunk: pl.BlockSpec(
        (None, L, D_MODEL), lambda g: (chunk(g) // nc, lax.rem(chunk(g), nc), 0))
    modrow = lambda slot, chunk: pl.BlockSpec(
        (None, None, None, 1, D_MODEL), lambda g: (layer, slot, chunk(g) // nc, 0, 0))
    scratch_set = [
        pltpu.VMEM((A_QK_W, L), _BF16), pltpu.VMEM((L, A_QK_W), _BF16),
        pltpu.VMEM((A_HEADS, AV_ROWS, L), _BF16), pltpu.VMEM((A_V_W, L), _F32),
        pltpu.VMEM((L, LANES), _F32), pltpu.VMEM((2 * A_HEADS, L), _F32),
    ]
    return pl.pallas_call(
        functools.partial(_mlstm_layer_kernel, n_chunks=nc),
        grid=(total + 1,),
        in_specs=[
            tok(proj_chunk), tok(rec_chunk),
            modrow(0, proj_chunk), modrow(1, proj_chunk), modrow(2, rec_chunk),
            _const_spec((A_QK_W, D_MODEL)), _const_spec((D_MODEL, A_QK_W)),
            _const_spec((A_V_W, D_MODEL)), _const_spec((A_V_W, D_MODEL)),
            _const_spec((D_MODEL, LANES)), _const_spec((2 * A_HEADS, D_MODEL)),
            _const_spec((1, LANES)), _const_spec((2 * A_HEADS, 1)),
            _const_spec((A_V_W, L)), _const_spec((A_V_W, D_MODEL)),
        ],
        out_specs=tok(rec_chunk),
        out_shape=jax.ShapeDtypeStruct(x.shape, _F32),
        scratch_shapes=scratch_set + scratch_set + [
            pltpu.VMEM((A_HEADS, AV_ROWS, A_DQK), _F32),
            pltpu.VMEM((A_HEADS, 1, LANES), _F32),
            pltpu.VMEM((A_V_W, L), _BF16),
        ],
        compiler_params=_params(1),
        name="mlstm_layer",
    )(x, x, mod, mod, mod, wqt, wk, wvt, wot, wg, wgate.T, bcol, brow, gain,
      w_out.astype(_BF16))


def _mlp_kernel(x_ref, sh_ref, sc_ref, g_ref, w1_ref, w2_ref, fgain_ref, out_ref, u_scr, *,
                final_norm):
    x = x_ref[...]
    h = _modulate(x, sh_ref[...], sc_ref[...]).astype(_BF16)
    for c in range(D_FF // FF_CHUNK):
        u = jnp.maximum(_dot(h, w1_ref[:, c * FF_CHUNK:(c + 1) * FF_CHUNK]), 0.0)
        u_scr[:, c * FF_CHUNK:(c + 1) * FF_CHUNK] = (u * u).astype(_BF16)
    y = x + g_ref[...] * _dot(u_scr[...], w2_ref[...])
    if final_norm:
        y = _rms(y) * fgain_ref[...]
    out_ref[...] = y


def _layer_spec(layer, rows, cols):
    return pl.BlockSpec((None, rows, cols), lambda *_: (layer, 0, 0),
                        pipeline_mode=pl.Buffered(1))


def _mlp(x, mod, layer, w1, w2, final_gain, final_norm):
    batch, seq, _ = x.shape
    tm = MLP_TILE
    tok = pl.BlockSpec((None, tm, D_MODEL), lambda b, i: (b, i, 0))
    return pl.pallas_call(
        functools.partial(_mlp_kernel, final_norm=final_norm),
        grid=(batch, seq // tm),
        in_specs=[
            tok, _mod_spec(layer, 3), _mod_spec(layer, 4), _mod_spec(layer, 5),
            _layer_spec(layer, D_MODEL, D_FF), _layer_spec(layer, D_FF, D_MODEL),
            _const_spec((1, D_MODEL)),
        ],
        out_specs=tok,
        out_shape=jax.ShapeDtypeStruct(x.shape, _F32),
        scratch_shapes=[pltpu.VMEM((tm, D_FF), _BF16)],
        compiler_params=_params(2),
        name="mlp",
    )(x, mod, mod, mod, w1, w2, final_gain.reshape(1, D_MODEL).astype(_F32))


def _split_bf16(x):
    hi = x.astype(_BF16).astype(_F32)
    rest = x - hi
    mid = rest.astype(_BF16).astype(_F32)
    lo = (rest - mid).astype(_BF16).astype(_F32)
    return hi, mid, lo


N_BIAS = 3
MAX_LOG2_WEIGHT = 60.0
BOUND_SLACK = 1.0 + 2.0 ** -6


def _bias_placement():
    place = np.zeros((LANES, B_HEADS * LANES), np.float32)
    for hd in range(B_HEADS):
        spare = hd * LANES + (0 if hd % 2 else B_DH)
        for piece in range(N_BIAS):
            place[piece * B_HEADS + hd, spare + piece] = -1.0
            place[N_BIAS * B_HEADS, spare + N_BIAS + piece] = 1.0
    return jnp.asarray(place, _BF16)


def _fox_kv_kernel(x_ref, gain_ref, wk_ref, wvt_ref, wf_ref, wft_ref, bcol_ref, brow_ref,
                   place_ref, kaug_ref, vt_ref, frow_ref, kmax2_ref, ccol_scr, crow_scr):
    tm = KV_TILE

    @pl.when(pl.program_id(1) == 0)
    def _():
        ccol_scr[...] = jnp.zeros_like(ccol_scr)
        crow_scr[...] = jnp.zeros_like(crow_scr)
        kmax2_ref[...] = jnp.zeros_like(kmax2_ref)

    ones = jnp.ones((V_ROWS - B_DH, tm), _BF16)
    lane = lax.broadcasted_iota(jnp.int32, (tm, LANES), 1)
    pair_lane = lax.broadcasted_iota(jnp.int32, (1, LANES), 1)
    carry_col = ccol_scr[...]
    carry_row = crow_scr[:, 0:1]
    kmax2 = kmax2_ref[...]
    def project(blk):
        nonlocal carry_col, carry_row
        h = (_rms(x_ref[blk * tm:(blk + 1) * tm, :]) * gain_ref[...]).astype(_BF16)
        vt = _dot_nt(wvt_ref[...], h).astype(_BF16)
        for hd in range(B_HEADS):
            vt_ref[blk, hd, 0:B_DH, :] = vt[hd * B_DH:(hd + 1) * B_DH, :]
            vt_ref[blk, hd, B_DH:V_ROWS, :] = ones
        k = _dot(h, wk_ref[...])
        fc = _segment_cumsum(_log_sigmoid(_dot(h, wf_ref[...]) + bcol_ref[...]), 0, tm)
        fc = fc + carry_col
        carry_col = fc[tm - 1:tm, :]
        fr = _segment_cumsum(_log_sigmoid(_dot_nt(wft_ref[...], h) + brow_ref[...]), 1, tm)
        fr = fr + carry_row
        carry_row = fr[:, tm - 1:tm]
        frow_ref[blk] = fr * LOG2E
        return k, fc

    def augment(blk, k, fc):
        nonlocal kmax2
        hi, mid, lo = _split_bf16(fc * LOG2E)
        pieces = jnp.where(lane < B_HEADS, hi, jnp.where(
            lane < 2 * B_HEADS, pltpu.roll(mid, B_HEADS, 1), jnp.where(
                lane < 3 * B_HEADS, pltpu.roll(lo, 2 * B_HEADS, 1), jnp.where(
                    lane == 3 * B_HEADS, 1.0, 0.0)))).astype(_BF16)
        bias = _dot(pieces, place_ref[...])
        for hd in range(B_HEADS):
            pair, odd = divmod(hd, 2)
            own = (lane >= B_DH) if odd else (lane < B_DH)
            kaug_ref[blk, hd] = jnp.where(
                own, k[:, pair * LANES:(pair + 1) * LANES],
                bias[:, hd * LANES:(hd + 1) * LANES]).astype(_BF16)
        for pair in range(B_HEADS // 2):
            kb = k[:, pair * LANES:(pair + 1) * LANES]
            n2 = jnp.max(jnp.sum(kb * kb, axis=1, keepdims=True), axis=0, keepdims=True)
            kmax2 = jnp.where(pair_lane == pair, jnp.maximum(kmax2, n2), kmax2)

    projected = project(0)
    for blk in range(KV_STEP):
        pending = projected
        if blk + 1 < KV_STEP:
            projected = project(blk + 1)
        augment(blk, *pending)
    ccol_scr[...] = carry_col
    crow_scr[...] = jnp.broadcast_to(carry_row, crow_scr.shape)
    kmax2_ref[...] = kmax2


def _fox_kv(x, kv_gain, w_kv, fg_bias):
    batch, seq, _ = x.shape
    tm = KV_TILE
    w = w_kv.astype(_BF16)
    wk, wvt, wfg = w[:, :D_MODEL], w[:, D_MODEL:2 * D_MODEL].T, w[:, 2 * D_MODEL:]
    wf = jnp.pad(wfg, ((0, 0), (0, LANES - B_HEADS)))
    wft = wfg.T
    bias = fg_bias.astype(_F32)
    bcol = jnp.pad(bias, (0, LANES - B_HEADS)).reshape(1, LANES)
    brow = bias.reshape(B_HEADS, 1)
    ks = KV_STEP
    return pl.pallas_call(
        _fox_kv_kernel,
        grid=(batch, seq // (ks * tm)),
        in_specs=[
            pl.BlockSpec((None, ks * tm, D_MODEL), lambda b, i: (b, i, 0)),
            _const_spec((1, D_MODEL)),
            _const_spec((D_MODEL, D_MODEL)), _const_spec((D_MODEL, D_MODEL)),
            _const_spec((D_MODEL, LANES)), _const_spec((B_HEADS, D_MODEL)),
            _const_spec((1, LANES)), _const_spec((B_HEADS, 1)),
            _const_spec((LANES, B_HEADS * LANES)),
        ],
        out_specs=[
            pl.BlockSpec((None, ks, B_HEADS, tm, LANES), lambda b, i: (b, i, 0, 0, 0)),
            pl.BlockSpec((None, ks, B_HEADS, V_ROWS, tm), lambda b, i: (b, i, 0, 0, 0)),
            pl.BlockSpec((None, ks, B_HEADS, tm), lambda b, i: (b, i, 0, 0)),
            pl.BlockSpec((None, 1, LANES), lambda b, i: (b, 0, 0)),
        ],
        out_shape=[
            jax.ShapeDtypeStruct((batch, seq // tm, B_HEADS, tm, LANES), _BF16),
            jax.ShapeDtypeStruct((batch, seq // tm, B_HEADS, V_ROWS, tm), _BF16),
            jax.ShapeDtypeStruct((batch, seq // tm, B_HEADS, tm), _F32),
            jax.ShapeDtypeStruct((batch, 1, LANES), _F32),
        ],
        scratch_shapes=[pltpu.VMEM((1, LANES), _F32), pltpu.VMEM((B_HEADS, LANES), _F32)],
        compiler_params=_params(2),
        name="fox_kv",
    )(x, kv_gain.reshape(1, D_MODEL).astype(_F32), wk, wvt, wf, wft, bcol, brow,
      _bias_placement())


def _fox_attn_kernel(x_ref, sh_ref, sc_ref, g1_ref, wqt_ref, kaug_ref, vt_ref, frow_ref,
                     kmax2_ref, wout_ref, out_ref, qaug_scr, m_scr, acc_scr, o_scr):
    assert Q_STEP % 2 == 0
    for s in range(Q_STEP):
        rows = pl.ds(s * Q_TILE, Q_TILE)
        _fox_attn_block(pl.program_id(1) * Q_STEP + s, s % 2, x_ref.at[rows], sh_ref, sc_ref,
                        g1_ref, wqt_ref, kaug_ref, vt_ref, frow_ref, kmax2_ref, wout_ref,
                        out_ref.at[rows], qaug_scr, m_scr, acc_scr, o_scr)


def _fox_attn_block(qi, qi_parity, x_ref, sh_ref, sc_ref, g1_ref, wqt_ref, kaug_ref, vt_ref,
                    frow_ref, kmax2_ref, wout_ref, out_ref, qaug_scr, m_scr, acc_scr, o_scr):
    tq, tk = Q_TILE, KV_TILE
    n_groups = B_HEADS // HEAD_GROUP
    h = _modulate(x_ref[...], sh_ref[...], sc_ref[...]).astype(_BF16)
    ft = frow_ref[qi]
    sub = lax.broadcasted_iota(jnp.int32, (B_DH, tq), 0)
    slab = Q_PROJ_HEADS * B_DH
    qt = [(_dot_nt(wqt_ref[r * slab:(r + 1) * slab, :], h)
           * (B_DH ** -0.5 * LOG2E)).astype(_BF16)
          for r in range(B_HEADS // Q_PROJ_HEADS)]
    kmax2 = kmax2_ref[...]
    score_bound = []
    for hd in range(B_HEADS):
        hi, mid, lo = _split_bf16(ft[hd:hd + 1, :])
        bias = jnp.where(sub < N_BIAS, 1.0, jnp.where(sub == N_BIAS, hi, jnp.where(
            sub == N_BIAS + 1, mid, jnp.where(sub == N_BIAS + 2, lo, 0.0)))).astype(_BF16)
        r, off = divmod(hd, Q_PROJ_HEADS)
        qh = qt[r][off * B_DH:(off + 1) * B_DH, :]
        lo_half, hi_half = (bias, qh) if hd % 2 else (qh, bias)
        qaug_scr[hd, 0:B_DH, :] = lo_half
        qaug_scr[hd, B_DH:2 * B_DH, :] = hi_half
        qf = qh.astype(_F32)
        qn2 = jnp.sum(qf * qf, axis=0, keepdims=True)
        score_bound.append(jnp.sqrt(qn2 * kmax2[:, hd // 2:hd // 2 + 1]) * BOUND_SLACK + 1.0)
    m_scr[...] = jnp.full(m_scr.shape, -jnp.inf, _F32)
    acc_scr[...] = jnp.zeros_like(acc_scr)

    def kv_blocks(blocks, mode):
        if mode == "diagonal":
            keep = (lax.broadcasted_iota(jnp.int32, (tk, tq), 1)
                    >= lax.broadcasted_iota(jnp.int32, (tk, tq), 0))
        def scores(unit):
            j, grp = unit
            return [_dot(kaug_ref[j, hd], qaug_scr[hd])
                    for hd in range(grp * HEAD_GROUP, (grp + 1) * HEAD_GROUP)]

        units = [(j, grp) for j in blocks for grp in range(n_groups)]
        s_next = scores(units[0])
        for n, (j, grp) in enumerate(units):
            heads = range(grp * HEAD_GROUP, (grp + 1) * HEAD_GROUP)
            s_cur = s_next
            if n + 1 < len(units):
                s_next = scores(units[n + 1])
            if mode == "fixed":
                for i, hd in enumerate(heads):
                    p_i = jnp.exp2(s_cur[i] - m_scr[hd]).astype(_BF16)
                    acc_scr[hd] = acc_scr[hd] + _dot(vt_ref[j, hd], p_i)
                continue
            if mode == "diagonal":
                s_cur = [jnp.where(keep, s, -jnp.inf) for s in s_cur]
            m_old = [m_scr[hd] for hd in heads]
            m_new = [jnp.maximum(mo, jnp.max(s, axis=0, keepdims=True))
                     for mo, s in zip(m_old, s_cur)]
            alpha = [jnp.exp2(mo - mn) for mo, mn in zip(m_old, m_new)]
            p = [jnp.exp2(s - mn) for s, mn in zip(s_cur, m_new)]
            for i, hd in enumerate(heads):
                m_scr[hd] = m_new[i]
                acc_scr[hd] = alpha[i] * acc_scr[hd] + _dot(vt_ref[j, hd], p[i].astype(_BF16))

    kv_blocks([qi], "diagonal")
    margin = m_scr[0] + MAX_LOG2_WEIGHT - score_bound[0]
    for hd in range(1, B_HEADS):
        margin = jnp.minimum(margin, m_scr[hd] + MAX_LOG2_WEIGHT - score_bound[hd])
    bounded = jnp.min(margin) >= 0.0

    @pl.when(bounded)
    def _():
        def pair(t, carry):
            kv_blocks([2 * t, 2 * t + 1], "fixed")
            return carry
        lax.fori_loop(0, qi // 2, pair, 0)
        if qi_parity:
            kv_blocks([qi - 1], "fixed")

    @pl.when(jnp.logical_not(bounded))
    def _():
        def single(j, carry):
            kv_blocks([j], "rescale")
            return carry
        lax.fori_loop(0, qi, single, 0)

    for hd in range(B_HEADS):
        o_scr[hd * B_DH:(hd + 1) * B_DH, :] = (
            acc_scr[hd, 0:B_DH, :] / acc_scr[hd, B_DH:B_DH + 1, :]).astype(_BF16)
    out_ref[...] = x_ref[...] + g1_ref[...] * _dot_tn(o_scr[...], wout_ref[...])


def _fox_attn(x, mod, layer, w_q, w_out, kaug, vt, frow, kmax2):
    batch, seq, _ = x.shape
    tq, tk = Q_TILE, KV_TILE
    nkv = seq // tk
    tok = pl.BlockSpec((None, Q_STEP * tq, D_MODEL), lambda b, i: (b, i, 0))
    return pl.pallas_call(
        _fox_attn_kernel,
        grid=(batch, seq // (Q_STEP * tq)),
        in_specs=[
            tok, _mod_spec(layer, 0), _mod_spec(layer, 1), _mod_spec(layer, 2),
            _const_spec((D_MODEL, D_MODEL)),
            pl.BlockSpec((None, nkv, B_HEADS, tk, LANES), lambda b, i: (b, 0, 0, 0, 0)),
            pl.BlockSpec((None, nkv, B_HEADS, V_ROWS, tk), lambda b, i: (b, 0, 0, 0, 0)),
            pl.BlockSpec((None, nkv, B_HEADS, tk), lambda b, i: (b, 0, 0, 0)),
            pl.BlockSpec((None, 1, LANES), lambda b, i: (b, 0, 0)),
            _const_spec((D_MODEL, D_MODEL)),
        ],
        out_specs=tok,
        out_shape=jax.ShapeDtypeStruct(x.shape, _F32),
        scratch_shapes=[
            pltpu.VMEM((B_HEADS, 2 * B_DH, tq), _BF16),
            pltpu.VMEM((B_HEADS, 1, tq), _F32),
            pltpu.VMEM((B_HEADS, V_ROWS, tq), _F32), pltpu.VMEM((D_MODEL, tq), _BF16),
        ],
        compiler_params=_params(2),
        name="fox_attn",
    )(x, mod, mod, mod, w_q.T.astype(_BF16), kaug, vt, frow, kmax2, w_out.astype(_BF16))


def kernel(x, c, ada_w, ada_b, a_w_in, a_b_i, a_b_f, a_head_gain, a_w_out, kv_gain, b_w_kv,
           b_fg_bias, b_w_q, b_w_out, mlp_w1, mlp_w2, final_gain):
    batch, seq, d = x.shape
    assert d == D_MODEL and Q_TILE == KV_TILE
    assert all(seq % t == 0
               for t in (MLP_TILE, MLSTM_CHUNK, Q_STEP * Q_TILE, KV_STEP * KV_TILE))
    mod = _adaln_table(c, ada_w, ada_b).reshape(DEPTH, 6, batch, 1, D_MODEL)
    w1_all, w2_all = mlp_w1.astype(_BF16), mlp_w2.astype(_BF16)
    shared = None
    for l in range(DEPTH):
        if l < N_A_LAYERS:
            x = _mlstm_layer(x, mod, l, a_w_in[l], a_b_i[l], a_b_f[l], a_head_gain[l], a_w_out[l])
        else:
            if shared is None:
                shared = _fox_kv(x, kv_gain, b_w_kv, b_fg_bias)
            j = l - N_A_LAYERS
            x = _fox_attn(x, mod, l, b_w_q[j], b_w_out[j], *shared)
        x = _mlp(x, mod, l, w1_all, w2_all, final_gain, final_norm=(l == DEPTH - 1))
    return x
```

```python
import functools

import jax
import jax.numpy as jnp
import numpy as np
from jax import lax
from jax.experimental import pallas as pl
from jax.experimental.pallas import tpu as pltpu

D_MODEL = 1024
DEPTH = 4
N_A_LAYERS = DEPTH // 2
A_HEADS = 4
A_DV = D_MODEL // A_HEADS
A_DQK = A_DV // 2
A_QK_W = A_HEADS * A_DQK
A_V_W = A_HEADS * A_DV
B_HEADS = 16
B_DH = D_MODEL // B_HEADS
D_FF = 4 * D_MODEL
EPS = 1e-6
LOG2E = 1.4426950408889634
LANES = 128
BF16_ROWS = 16
AV_ROWS = A_DV + BF16_ROWS
V_ROWS = B_DH + BF16_ROWS

MLSTM_CHUNK = 256
KV_TILE = 256
KV_STEP = 4
Q_TILE = 256
Q_STEP = 4
HEAD_GROUP = 8
Q_PROJ_HEADS = 4
MLP_TILE = 1024
FF_CHUNK = 512
VMEM_LIMIT = 56 * 1024 * 1024

_BF16 = jnp.bfloat16
_F32 = jnp.float32


def _dot(a, b):
    return jnp.dot(a, b, preferred_element_type=_F32)


def _dot_nt(a, b):
    return lax.dot_general(a, b, (((1,), (1,)), ((), ())), preferred_element_type=_F32)


def _dot_tn(a, b):
    return lax.dot_general(a, b, (((0,), (0,)), ((), ())), preferred_element_type=_F32)


def _rms(x):
    return x * lax.rsqrt(jnp.mean(x * x, axis=-1, keepdims=True) + EPS)


def _modulate(x, shift, scale):
    return _rms(x) * (1.0 + scale) + shift


def _log_sigmoid(z):
    return jnp.minimum(z, 0.0) - jnp.log1p(jnp.exp(-jnp.abs(z)))


def _segment_cumsum(x, axis, seg):
    pos = lax.broadcasted_iota(jnp.int32, x.shape, axis) & (seg - 1)
    k = 1
    while k < seg:
        x = x + jnp.where(pos >= k, pltpu.roll(x, k, axis), 0.0)
        k *= 2
    return x


def _params(n_grid):
    return pltpu.CompilerParams(dimension_semantics=("arbitrary",) * n_grid,
                                vmem_limit_bytes=VMEM_LIMIT)


def _const_spec(shape):
    return pl.BlockSpec(shape, lambda *_: (0,) * len(shape), pipeline_mode=pl.Buffered(1))


def _mod_spec(layer, slot):
    return pl.BlockSpec((None, None, None, 1, D_MODEL), lambda b, i: (layer, slot, b, 0, 0))


def _adaln_kernel(c_ref, w_ref, b_ref, o_ref):
    c = c_ref[...]
    cond = (c * jax.nn.sigmoid(c)).astype(_BF16)
    o_ref[...] = _dot(cond, w_ref[...].astype(_BF16)) + b_ref[...]


def _adaln_table(c, ada_w, ada_b):
    batch = c.shape[0]
    return pl.pallas_call(
        _adaln_kernel,
        grid=(DEPTH, 6),
        in_specs=[
            pl.BlockSpec((batch, D_MODEL), lambda l, j: (0, 0)),
            pl.BlockSpec((None, D_MODEL, D_MODEL), lambda l, j: (l, 0, j)),
            pl.BlockSpec((None, None, 1, D_MODEL), lambda l, j: (l, j, 0, 0)),
        ],
        out_specs=pl.BlockSpec((None, None, batch, D_MODEL), lambda l, j: (l, j, 0, 0)),
        out_shape=jax.ShapeDtypeStruct((DEPTH, 6, batch, D_MODEL), _F32),
        compiler_params=_params(2),
        name="adaln_table",
    )(c, ada_w, ada_b.reshape(DEPTH, 6, 1, D_MODEL))


def _mlstm_project_qkg(h, w, dst):
    wqt_ref, wk_ref, _, _, wg_ref, wgt_ref, bcol_ref, brow_ref = w
    qt_s, k_s, _, _, gcol_s, grow_s = dst
    ct = MLSTM_CHUNK
    qt_s[...] = (_dot_nt(wqt_ref[...], h) * (A_DQK ** -0.5)).astype(_BF16)
    k_s[...] = _dot(h, wk_ref[...]).astype(_BF16)
    z = _dot(h, wg_ref[...]) + bcol_ref[...]
    bc = _segment_cumsum(_log_sigmoid(z), 0, ct)
    gcol_s[...] = (z - pltpu.roll(bc, LANES - A_HEADS, 1)) * LOG2E
    zr = _dot_nt(wgt_ref[...], h) + brow_ref[...]
    br = _segment_cumsum(_log_sigmoid(zr), 1, ct)
    sub = lax.broadcasted_iota(jnp.int32, zr.shape, 0)
    grow_s[...] = jnp.where(sub < A_HEADS, zr, br) * LOG2E


def _mlstm_project_v(h, w, dst):
    vt = _dot_nt(w[2][...], h).astype(_BF16)
    for hd in range(A_HEADS):
        dst[2][hd, 0:A_DV, :] = vt[hd * A_DV:(hd + 1) * A_DV, :]


def _mlstm_project_o(h, w, dst):
    dst[3][...] = _dot_nt(w[3][...], h)


def _mlstm_layer_kernel(xcur_ref, xprev_ref, sh_ref, sc_ref, g1prev_ref, g1cur_ref, wqt_ref, wk_ref,
                        wvt_ref, wot_ref, wg_ref, wgt_ref, bcol_ref, brow_ref, gain_ref, wout_ref,
                        out_ref, qt_a, k_a, vt_a, ot_a, gcol_a, grow_a, qt_b, k_b, vt_b, ot_b,
                        gcol_b, grow_b, cn_scr, m_scr, z_scr, hold_scr, *, n_chunks):
    L = MLSTM_CHUNK
    heads = range(A_HEADS)
    t = pl.program_id(0)
    weights = (wqt_ref, wk_ref, wvt_ref, wot_ref, wg_ref, wgt_ref, bcol_ref, brow_ref)
    set_a = (qt_a, k_a, vt_a, ot_a, gcol_a, grow_a)
    set_b = (qt_b, k_b, vt_b, ot_b, gcol_b, grow_b)

    @pl.when(t == 0)
    def _():
        for ref in set_b + (hold_scr, cn_scr, m_scr):
            ref[...] = jnp.zeros_like(ref)
        for vt_s in (vt_a, vt_b):
            vt_s[:, A_DV:AV_ROWS, :] = jnp.ones((A_HEADS, AV_ROWS - A_DV, L), _BF16)

    causal = (lax.broadcasted_iota(jnp.int32, (L, L), 0)
              <= lax.broadcasted_iota(jnp.int32, (L, L), 1))

    def step(dst, src, x_proj, x_res, g1, fresh, store):
        qt_s, k_s, vt_s, ot_s, gcol_s, grow_s = src
        gcol = gcol_s[...]
        grow = grow_s[...]
        kh = [k_s[:, h * A_DQK:(h + 1) * A_DQK] for h in heads]
        qt = [qt_s[h * A_DQK:(h + 1) * A_DQK, :] for h in heads]
        if fresh is None:
            cn_prev = [cn_scr[h] for h in heads]
            m_prev = [m_scr[h][:, 0:1] for h in heads]
        else:
            cn_prev = [jnp.where(fresh, 0.0, cn_scr[h]) for h in heads]
            m_prev = [jnp.where(fresh, 0.0, m_scr[h][:, 0:1]) for h in heads]
        st = [_dot(kh[h], qt[h]) for h in heads]
        inter_mm = [_dot(cn_prev[h].astype(_BF16), qt[h]) for h in heads]

        i_row = [grow[h:h + 1, :] for h in heads]
        b_row = [grow[A_HEADS + h:A_HEADS + h + 1, :] for h in heads]
        d = [jnp.where(causal, gcol[:, h:h + 1] + b_row[h], -jnp.inf) for h in heads]
        inter = [b_row[h] + m_prev[h] for h in heads]
        m_t = [jnp.maximum(inter[h], jnp.max(d[h], axis=0, keepdims=True)) for h in heads]
        a = [(jnp.exp2(d[h] - m_t[h]) * st[h]).astype(_BF16) for h in heads]
        w_inter = [jnp.exp2(inter[h] - m_t[h]) for h in heads]

        hp = _modulate(x_proj, sh_ref[...], sc_ref[...]).astype(_BF16)
        _mlstm_project_qkg(hp, weights, dst)
        _mlstm_project_v(hp, weights, dst)
        nd = [w_inter[h] * inter_mm[h] + _dot(vt_s[h], a[h]) for h in heads]
        _mlstm_project_o(hp, weights, dst)

        b_last = [b_row[h][:, L - 1:L] for h in heads]
        dl = [b_last[h] - b_row[h] + i_row[h] for h in heads]
        m_new = [jnp.maximum(b_last[h] + m_prev[h], jnp.max(dl[h], axis=1, keepdims=True))
                 for h in heads]
        for h in heads:
            rows = slice(h * A_DV, (h + 1) * A_DV)
            den = nd[h][A_DV:A_DV + 1, :]
            ht = nd[h][0:A_DV, :] / jnp.maximum(jnp.abs(den), jnp.exp2(-m_t[h]))
            ht = (ht * lax.rsqrt(jnp.mean(ht * ht, axis=0, keepdims=True) + EPS)
                  * gain_ref[rows, :])
            z_scr[rows, :] = (jax.nn.sigmoid(ot_s[rows, :]) * ht).astype(_BF16)
        store(x_res + g1 * _dot_tn(z_scr[...], wout_ref[...]))
        for h in heads:
            vw = (vt_s[h].astype(_F32) * jnp.exp2(dl[h] - m_new[h])).astype(_BF16)
            decay = jnp.exp2(b_last[h] + m_prev[h] - m_new[h])
            cn_scr[h] = decay * cn_prev[h] + _dot(vw, kh[h])
            m_scr[h] = jnp.broadcast_to(m_new[h], (1, LANES))

    def store_second_half(y):
        out_ref[L:2 * L, :] = y

    def store_held(y):
        hold_scr[...] = y

    out_ref[0:L, :] = hold_scr[...]
    step(set_a, set_b, xcur_ref[0:L, :], xprev_ref[L:2 * L, :], g1prev_ref[...], None,
         store_second_half)
    step(set_b, set_a, xcur_ref[L:2 * L, :], xcur_ref[0:L, :], g1cur_ref[...],
         lax.rem(2 * t, n_chunks) == 0, store_held)


def _mlstm_layer(x, mod, layer, w_in, b_i, b_f, head_gain, w_out):
    batch, seq, _ = x.shape
    L = MLSTM_CHUNK
    nc = seq // L
    total = batch * nc
    w = w_in.astype(_BF16)
    wqt, wk = w[:, :A_QK_W].T, w[:, A_QK_W:2 * A_QK_W]
    wvt = w[:, 2 * A_QK_W:2 * A_QK_W + A_V_W].T
    wot = w[:, 2 * A_QK_W + A_V_W:2 * A_QK_W + 2 * A_V_W].T
    wgate = w[:, 2 * A_QK_W + 2 * A_V_W:]
    wg = jnp.pad(wgate, ((0, 0), (0, LANES - 2 * A_HEADS)))
    bias = jnp.concatenate([b_i, b_f]).astype(_F32)
    bcol = jnp.pad(bias, (0, LANES - 2 * A_HEADS)).reshape(1, LANES)
    brow = bias.reshape(2 * A_HEADS, 1)
    gain = jnp.broadcast_to(head_gain.reshape(A_V_W, 1).astype(_F32), (A_V_W, L))

    assert nc % 2 == 0
    ppb = nc // 2
    n_pairs = batch * ppb
    cur_pair = lambda t: jnp.minimum(t, n_pairs - 1)
    prev_pair = lambda t: jnp.maximum(t - 1, 0)
    tok = lambda pair: pl.BlockSpec(
        (None, 2 * L, D_MODEL), lambda t: (pair(t) // ppb, lax.rem(pair(t), ppb), 0))
    modrow = lambda slot, pair: pl.BlockSpec(
        (None, None, None, 1, D_MODEL), lambda t: (layer, slot, pair(t) // ppb, 0, 0))
    scratch_set = [
        pltpu.VMEM((A_QK_W, L), _BF16), pltpu.VMEM((L, A_QK_W), _BF16),
        pltpu.VMEM((A_HEADS, AV_ROWS, L), _BF16), pltpu.VMEM((A_V_W, L), _F32),
        pltpu.VMEM((L, LANES), _F32), pltpu.VMEM((2 * A_HEADS, L), _F32),
    ]
    return pl.pallas_call(
        functools.partial(_mlstm_layer_kernel, n_chunks=nc),
        grid=(n_pairs + 1,),
        in_specs=[
            tok(cur_pair), tok(prev_pair),
            modrow(0, cur_pair), modrow(1, cur_pair), modrow(2, prev_pair), modrow(2, cur_pair),
            _const_spec((A_QK_W, D_MODEL)), _const_spec((D_MODEL, A_QK_W)),
            _const_spec((A_V_W, D_MODEL)), _const_spec((A_V_W, D_MODEL)),
            _const_spec((D_MODEL, LANES)), _const_spec((2 * A_HEADS, D_MODEL)),
            _const_spec((1, LANES)), _const_spec((2 * A_HEADS, 1)),
            _const_spec((A_V_W, L)), _const_spec((A_V_W, D_MODEL)),
        ],
        out_specs=tok(prev_pair),
        out_shape=jax.ShapeDtypeStruct(x.shape, _F32),
        scratch_shapes=scratch_set + scratch_set + [
            pltpu.VMEM((A_HEADS, AV_ROWS, A_DQK), _F32),
            pltpu.VMEM((A_HEADS, 1, LANES), _F32),
            pltpu.VMEM((A_V_W, L), _BF16),
            pltpu.VMEM((L, D_MODEL), _F32),
        ],
        compiler_params=_params(1),
        name="mlstm_layer",
    )(x, x, mod, mod, mod, mod, wqt, wk, wvt, wot, wg, wgate.T, bcol, brow, gain,
      w_out.astype(_BF16))


def _mlp_kernel(x_ref, sh_ref, sc_ref, g_ref, w1_ref, w2_ref, fgain_ref, out_ref, u_scr, *,
                final_norm):
    x = x_ref[...]
    h = _modulate(x, sh_ref[...], sc_ref[...]).astype(_BF16)
    for c in range(D_FF // FF_CHUNK):
        u = jnp.maximum(_dot(h, w1_ref[:, c * FF_CHUNK:(c + 1) * FF_CHUNK]), 0.0)
        u_scr[:, c * FF_CHUNK:(c + 1) * FF_CHUNK] = (u * u).astype(_BF16)
    y = x + g_ref[...] * _dot(u_scr[...], w2_ref[...])
    if final_norm:
        y = _rms(y) * fgain_ref[...]
    out_ref[...] = y


def _layer_spec(layer, rows, cols):
    return pl.BlockSpec((None, rows, cols), lambda *_: (layer, 0, 0),
                        pipeline_mode=pl.Buffered(1))


def _mlp(x, mod, layer, w1, w2, final_gain, final_norm):
    batch, seq, _ = x.shape
    tm = MLP_TILE
    tok = pl.BlockSpec((None, tm, D_MODEL), lambda b, i: (b, i, 0))
    return pl.pallas_call(
        functools.partial(_mlp_kernel, final_norm=final_norm),
        grid=(batch, seq // tm),
        in_specs=[
            tok, _mod_spec(layer, 3), _mod_spec(layer, 4), _mod_spec(layer, 5),
            _layer_spec(layer, D_MODEL, D_FF), _layer_spec(layer, D_FF, D_MODEL),
            _const_spec((1, D_MODEL)),
        ],
        out_specs=tok,
        out_shape=jax.ShapeDtypeStruct(x.shape, _F32),
        scratch_shapes=[pltpu.VMEM((tm, D_FF), _BF16)],
        compiler_params=_params(2),
        name="mlp",
    )(x, mod, mod, mod, w1, w2, final_gain.reshape(1, D_MODEL).astype(_F32))


def _split_bf16(x):
    hi = x.astype(_BF16).astype(_F32)
    rest = x - hi
    mid = rest.astype(_BF16).astype(_F32)
    lo = (rest - mid).astype(_BF16).astype(_F32)
    return hi, mid, lo


N_BIAS = 3
MAX_LOG2_WEIGHT = 60.0
BOUND_SLACK = 1.0 + 2.0 ** -6


def _bias_placement():
    place = np.zeros((LANES, B_HEADS * LANES), np.float32)
    for hd in range(B_HEADS):
        spare = hd * LANES + (0 if hd % 2 else B_DH)
        for piece in range(N_BIAS):
            place[piece * B_HEADS + hd, spare + piece] = -1.0
            place[N_BIAS * B_HEADS, spare + N_BIAS + piece] = 1.0
    return jnp.asarray(place, _BF16)


def _fox_kv_kernel(x_ref, gain_ref, wk_ref, wvt_ref, wf_ref, wft_ref, bcol_ref, brow_ref,
                   place_ref, kaug_ref, vt_ref, frow_ref, kmax2_ref, ccol_scr, crow_scr):
    tm = KV_TILE

    @pl.when(pl.program_id(1) == 0)
    def _():
        ccol_scr[...] = jnp.zeros_like(ccol_scr)
        crow_scr[...] = jnp.zeros_like(crow_scr)
        kmax2_ref[...] = jnp.zeros_like(kmax2_ref)

    ones = jnp.ones((V_ROWS - B_DH, tm), _BF16)
    lane = lax.broadcasted_iota(jnp.int32, (tm, LANES), 1)
    pair_lane = lax.broadcasted_iota(jnp.int32, (1, LANES), 1)
    carry_col = ccol_scr[...]
    carry_row = crow_scr[:, 0:1]
    kmax2 = kmax2_ref[...]
    def project(blk):
        nonlocal carry_col, carry_row
        h = (_rms(x_ref[blk * tm:(blk + 1) * tm, :]) * gain_ref[...]).astype(_BF16)
        vt = _dot_nt(wvt_ref[...], h).astype(_BF16)
        for hd in range(B_HEADS):
            vt_ref[blk, hd, 0:B_DH, :] = vt[hd * B_DH:(hd + 1) * B_DH, :]
            vt_ref[blk, hd, B_DH:V_ROWS, :] = ones
        k = _dot(h, wk_ref[...])
        fc = _segment_cumsum(_log_sigmoid(_dot(h, wf_ref[...]) + bcol_ref[...]), 0, tm)
        fc = fc + carry_col
        carry_col = fc[tm - 1:tm, :]
        fr = _segment_cumsum(_log_sigmoid(_dot_nt(wft_ref[...], h) + brow_ref[...]), 1, tm)
        fr = fr + carry_row
        carry_row = fr[:, tm - 1:tm]
        frow_ref[blk] = fr * LOG2E
        return k, fc

    def augment(blk, k, fc):
        nonlocal kmax2
        hi, mid, lo = _split_bf16(fc * LOG2E)
        pieces = jnp.where(lane < B_HEADS, hi, jnp.where(
            lane < 2 * B_HEADS, pltpu.roll(mid, B_HEADS, 1), jnp.where(
                lane < 3 * B_HEADS, pltpu.roll(lo, 2 * B_HEADS, 1), jnp.where(
                    lane == 3 * B_HEADS, 1.0, 0.0)))).astype(_BF16)
        bias = _dot(pieces, place_ref[...])
        for hd in range(B_HEADS):
            pair, odd = divmod(hd, 2)
            own = (lane >= B_DH) if odd else (lane < B_DH)
            kaug_ref[blk, hd] = jnp.where(
                own, k[:, pair * LANES:(pair + 1) * LANES],
                bias[:, hd * LANES:(hd + 1) * LANES]).astype(_BF16)
        for pair in range(B_HEADS // 2):
            kb = k[:, pair * LANES:(pair + 1) * LANES]
            n2 = jnp.max(jnp.sum(kb * kb, axis=1, keepdims=True), axis=0, keepdims=True)
            kmax2 = jnp.where(pair_lane == pair, jnp.maximum(kmax2, n2), kmax2)

    projected = project(0)
    for blk in range(KV_STEP):
        pending = projected
        if blk + 1 < KV_STEP:
            projected = project(blk + 1)
        augment(blk, *pending)
    ccol_scr[...] = carry_col
    crow_scr[...] = jnp.broadcast_to(carry_row, crow_scr.shape)
    kmax2_ref[...] = kmax2


def _fox_kv(x, kv_gain, w_kv, fg_bias):
    batch, seq, _ = x.shape
    tm = KV_TILE
    w = w_kv.astype(_BF16)
    wk, wvt, wfg = w[:, :D_MODEL], w[:, D_MODEL:2 * D_MODEL].T, w[:, 2 * D_MODEL:]
    wf = jnp.pad(wfg, ((0, 0), (0, LANES - B_HEADS)))
    wft = wfg.T
    bias = fg_bias.astype(_F32)
    bcol = jnp.pad(bias, (0, LANES - B_HEADS)).reshape(1, LANES)
    brow = bias.reshape(B_HEADS, 1)
    ks = KV_STEP
    return pl.pallas_call(
        _fox_kv_kernel,
        grid=(batch, seq // (ks * tm)),
        in_specs=[
            pl.BlockSpec((None, ks * tm, D_MODEL), lambda b, i: (b, i, 0)),
            _const_spec((1, D_MODEL)),
            _const_spec((D_MODEL, D_MODEL)), _const_spec((D_MODEL, D_MODEL)),
            _const_spec((D_MODEL, LANES)), _const_spec((B_HEADS, D_MODEL)),
            _const_spec((1, LANES)), _const_spec((B_HEADS, 1)),
            _const_spec((LANES, B_HEADS * LANES)),
        ],
        out_specs=[
            pl.BlockSpec((None, ks, B_HEADS, tm, LANES), lambda b, i: (b, i, 0, 0, 0)),
            pl.BlockSpec((None, ks, B_HEADS, V_ROWS, tm), lambda b, i: (b, i, 0, 0, 0)),
            pl.BlockSpec((None, ks, B_HEADS, tm), lambda b, i: (b, i, 0, 0)),
            pl.BlockSpec((None, 1, LANES), lambda b, i: (b, 0, 0)),
        ],
        out_shape=[
            jax.ShapeDtypeStruct((batch, seq // tm, B_HEADS, tm, LANES), _BF16),
            jax.ShapeDtypeStruct((batch, seq // tm, B_HEADS, V_ROWS, tm), _BF16),
            jax.ShapeDtypeStruct((batch, seq // tm, B_HEADS, tm), _F32),
            jax.ShapeDtypeStruct((batch, 1, LANES), _F32),
        ],
        scratch_shapes=[pltpu.VMEM((1, LANES), _F32), pltpu.VMEM((B_HEADS, LANES), _F32)],
        compiler_params=_params(2),
        name="fox_kv",
    )(x, kv_gain.reshape(1, D_MODEL).astype(_F32), wk, wvt, wf, wft, bcol, brow,
      _bias_placement())


def _fox_attn_kernel(x_ref, sh_ref, sc_ref, g1_ref, wqt_ref, kaug_ref, vt_ref, frow_ref,
                     kmax2_ref, wout_ref, out_ref, qaug_scr, m_scr, acc_scr, o_scr):
    assert Q_STEP % 2 == 0
    for s in range(Q_STEP):
        rows = pl.ds(s * Q_TILE, Q_TILE)
        _fox_attn_block(pl.program_id(1) * Q_STEP + s, s % 2, x_ref.at[rows], sh_ref, sc_ref,
                        g1_ref, wqt_ref, kaug_ref, vt_ref, frow_ref, kmax2_ref, wout_ref,
                        out_ref.at[rows], qaug_scr, m_scr, acc_scr, o_scr)


def _fox_attn_block(qi, qi_parity, x_ref, sh_ref, sc_ref, g1_ref, wqt_ref, kaug_ref, vt_ref,
                    frow_ref, kmax2_ref, wout_ref, out_ref, qaug_scr, m_scr, acc_scr, o_scr):
    tq, tk = Q_TILE, KV_TILE
    n_groups = B_HEADS // HEAD_GROUP
    h = _modulate(x_ref[...], sh_ref[...], sc_ref[...]).astype(_BF16)
    ft = frow_ref[qi]
    sub = lax.broadcasted_iota(jnp.int32, (B_DH, tq), 0)
    slab = Q_PROJ_HEADS * B_DH
    qt = [(_dot_nt(wqt_ref[r * slab:(r + 1) * slab, :], h)
           * (B_DH ** -0.5 * LOG2E)).astype(_BF16)
          for r in range(B_HEADS // Q_PROJ_HEADS)]
    kmax2 = kmax2_ref[...]
    score_bound = []
    for hd in range(B_HEADS):
        hi, mid, lo = _split_bf16(ft[hd:hd + 1, :])
        bias = jnp.where(sub < N_BIAS, 1.0, jnp.where(sub == N_BIAS, hi, jnp.where(
            sub == N_BIAS + 1, mid, jnp.where(sub == N_BIAS + 2, lo, 0.0)))).astype(_BF16)
        r, off = divmod(hd, Q_PROJ_HEADS)
        qh = qt[r][off * B_DH:(off + 1) * B_DH, :]
        lo_half, hi_half = (bias, qh) if hd % 2 else (qh, bias)
        qaug_scr[hd, 0:B_DH, :] = lo_half
        qaug_scr[hd, B_DH:2 * B_DH, :] = hi_half
        qf = qh.astype(_F32)
        qn2 = jnp.sum(qf * qf, axis=0, keepdims=True)
        score_bound.append(jnp.sqrt(qn2 * kmax2[:, hd // 2:hd // 2 + 1]) * BOUND_SLACK + 1.0)
    m_scr[...] = jnp.full(m_scr.shape, -jnp.inf, _F32)
    acc_scr[...] = jnp.zeros_like(acc_scr)

    def kv_blocks(blocks, mode):
        if mode == "diagonal":
            keep = (lax.broadcasted_iota(jnp.int32, (tk, tq), 1)
                    >= lax.broadcasted_iota(jnp.int32, (tk, tq), 0))
        def scores(unit):
            j, grp = unit
            return [_dot(kaug_ref[j, hd], qaug_scr[hd])
                    for hd in range(grp * HEAD_GROUP, (grp + 1) * HEAD_GROUP)]

        units = [(j, grp) for j in blocks for grp in range(n_groups)]
        s_next = scores(units[0])
        for n, (j, grp) in enumerate(units):
            heads = range(grp * HEAD_GROUP, (grp + 1) * HEAD_GROUP)
            s_cur = s_next
            if n + 1 < len(units):
                s_next = scores(units[n + 1])
            if mode == "fixed":
                for i, hd in enumerate(heads):
                    p_i = jnp.exp2(s_cur[i] - m_scr[hd]).astype(_BF16)
                    acc_scr[hd] = acc_scr[hd] + _dot(vt_ref[j, hd], p_i)
                continue
            if mode == "diagonal":
                s_cur = [jnp.where(keep, s, -jnp.inf) for s in s_cur]
            m_old = [m_scr[hd] for hd in heads]
            m_new = [jnp.maximum(mo, jnp.max(s, axis=0, keepdims=True))
                     for mo, s in zip(m_old, s_cur)]
            alpha = [jnp.exp2(mo - mn) for mo, mn in zip(m_old, m_new)]
            p = [jnp.exp2(s - mn) for s, mn in zip(s_cur, m_new)]
            for i, hd in enumerate(heads):
                m_scr[hd] = m_new[i]
                acc_scr[hd] = alpha[i] * acc_scr[hd] + _dot(vt_ref[j, hd], p[i].astype(_BF16))

    kv_blocks([qi], "diagonal")
    margin = m_scr[0] + MAX_LOG2_WEIGHT - score_bound[0]
    for hd in range(1, B_HEADS):
        margin = jnp.minimum(margin, m_scr[hd] + MAX_LOG2_WEIGHT - score_bound[hd])
    bounded = jnp.min(margin) >= 0.0

    @pl.when(bounded)
    def _():
        def pair(t, carry):
            kv_blocks([2 * t, 2 * t + 1], "fixed")
            return carry
        lax.fori_loop(0, qi // 2, pair, 0)
        if qi_parity:
            kv_blocks([qi - 1], "fixed")

    @pl.when(jnp.logical_not(bounded))
    def _():
        def single(j, carry):
            kv_blocks([j], "rescale")
            return carry
        lax.fori_loop(0, qi, single, 0)

    for hd in range(B_HEADS):
        o_scr[hd * B_DH:(hd + 1) * B_DH, :] = (
            acc_scr[hd, 0:B_DH, :] / acc_scr[hd, B_DH:B_DH + 1, :]).astype(_BF16)
    out_ref[...] = x_ref[...] + g1_ref[...] * _dot_tn(o_scr[...], wout_ref[...])


def _fox_attn(x, mod, layer, w_q, w_out, kaug, vt, frow, kmax2):
    batch, seq, _ = x.shape
    tq, tk = Q_TILE, KV_TILE
    nkv = seq // tk
    tok = pl.BlockSpec((None, Q_STEP * tq, D_MODEL), lambda b, i: (b, i, 0))
    return pl.pallas_call(
        _fox_attn_kernel,
        grid=(batch, seq // (Q_STEP * tq)),
        in_specs=[
            tok, _mod_spec(layer, 0), _mod_spec(layer, 1), _mod_spec(layer, 2),
            _const_spec((D_MODEL, D_MODEL)),
            pl.BlockSpec((None, nkv, B_HEADS, tk, LANES), lambda b, i: (b, 0, 0, 0, 0)),
            pl.BlockSpec((None, nkv, B_HEADS, V_ROWS, tk), lambda b, i: (b, 0, 0, 0, 0)),
            pl.BlockSpec((None, nkv, B_HEADS, tk), lambda b, i: (b, 0, 0, 0)),
            pl.BlockSpec((None, 1, LANES), lambda b, i: (b, 0, 0)),
            _const_spec((D_MODEL, D_MODEL)),
        ],
        out_specs=tok,
        out_shape=jax.ShapeDtypeStruct(x.shape, _F32),
        scratch_shapes=[
            pltpu.VMEM((B_HEADS, 2 * B_DH, tq), _BF16),
            pltpu.VMEM((B_HEADS, 1, tq), _F32),
            pltpu.VMEM((B_HEADS, V_ROWS, tq), _F32), pltpu.VMEM((D_MODEL, tq), _BF16),
        ],
        compiler_params=_params(2),
        name="fox_attn",
    )(x, mod, mod, mod, w_q.T.astype(_BF16), kaug, vt, frow, kmax2, w_out.astype(_BF16))


def kernel(x, c, ada_w, ada_b, a_w_in, a_b_i, a_b_f, a_head_gain, a_w_out, kv_gain, b_w_kv,
           b_fg_bias, b_w_q, b_w_out, mlp_w1, mlp_w2, final_gain):
    batch, seq, d = x.shape
    assert d == D_MODEL and Q_TILE == KV_TILE
    assert all(seq % t == 0
               for t in (MLP_TILE, MLSTM_CHUNK, Q_STEP * Q_TILE, KV_STEP * KV_TILE))
    mod = _adaln_table(c, ada_w, ada_b).reshape(DEPTH, 6, batch, 1, D_MODEL)
    w1_all, w2_all = mlp_w1.astype(_BF16), mlp_w2.astype(_BF16)
    shared = None
    for l in range(DEPTH):
        if l < N_A_LAYERS:
            x = _mlstm_layer(x, mod, l, a_w_in[l], a_b_i[l], a_b_f[l], a_head_gain[l], a_w_out[l])
        else:
            if shared is None:
                shared = _fox_kv(x, kv_gain, b_w_kv, b_fg_bias)
            j = l - N_A_LAYERS
            x = _fox_attn(x, mod, l, b_w_q[j], b_w_out[j], *shared)
        x = _mlp(x, mod, l, w1_all, w2_all, final_gain, final_norm=(l == DEPTH - 1))
    return x
```

```python
import functools

import jax
import jax.numpy as jnp
import numpy as np
from jax import lax
from jax.experimental import pallas as pl
from jax.experimental.pallas import tpu as pltpu

D_MODEL = 1024
DEPTH = 4
N_A_LAYERS = DEPTH // 2
A_HEADS = 4
A_DV = D_MODEL // A_HEADS
A_DQK = A_DV // 2
A_QK_W = A_HEADS * A_DQK
A_V_W = A_HEADS * A_DV
B_HEADS = 16
B_DH = D_MODEL // B_HEADS
D_FF = 4 * D_MODEL
EPS = 1e-6
LOG2E = 1.4426950408889634
LANES = 128
BF16_ROWS = 16
AV_ROWS = A_DV + BF16_ROWS
V_ROWS = B_DH + BF16_ROWS

MLSTM_CHUNK = 256
KV_TILE = 256
KV_STEP = 4
Q_TILE = 256
Q_STEP = 4
HEAD_GROUP = 8
Q_PROJ_HEADS = 4
MLP_TILE = 1024
FF_CHUNK = 512
W_SLAB = 256
VMEM_LIMIT = 56 * 1024 * 1024

_BF16 = jnp.bfloat16
_F32 = jnp.float32


def _dot(a, b):
    return jnp.dot(a, b, preferred_element_type=_F32)


def _dot_nt(a, b):
    return lax.dot_general(a, b, (((1,), (1,)), ((), ())), preferred_element_type=_F32)


def _dot_tn(a, b):
    return lax.dot_general(a, b, (((0,), (0,)), ((), ())), preferred_element_type=_F32)


def _rms(x):
    return x * lax.rsqrt(jnp.mean(x * x, axis=-1, keepdims=True) + EPS)


def _modulate(x, shift, scale):
    return _rms(x) * (1.0 + scale) + shift


def _log_sigmoid(z):
    return jnp.minimum(z, 0.0) - jnp.log1p(jnp.exp(-jnp.abs(z)))


def _segment_cumsum(x, axis, seg):
    pos = lax.broadcasted_iota(jnp.int32, x.shape, axis) & (seg - 1)
    k = 1
    while k < seg:
        x = x + jnp.where(pos >= k, pltpu.roll(x, k, axis), 0.0)
        k *= 2
    return x


def _params(n_grid):
    return pltpu.CompilerParams(dimension_semantics=("arbitrary",) * n_grid,
                                vmem_limit_bytes=VMEM_LIMIT)


def _const_spec(shape):
    return pl.BlockSpec(shape, lambda *_: (0,) * len(shape), pipeline_mode=pl.Buffered(1))


def _mod_spec(layer, slot):
    return pl.BlockSpec((None, None, None, 1, D_MODEL), lambda b, i: (layer, slot, b, 0, 0))


def _adaln_kernel(c_ref, w_ref, b_ref, o_ref):
    c = c_ref[...]
    cond = (c * jax.nn.sigmoid(c)).astype(_BF16)
    o_ref[...] = _dot(cond, w_ref[...].astype(_BF16)) + b_ref[...]


def _adaln_table(c, ada_w, ada_b):
    batch = c.shape[0]
    return pl.pallas_call(
        _adaln_kernel,
        grid=(DEPTH, 6),
        in_specs=[
            pl.BlockSpec((batch, D_MODEL), lambda l, j: (0, 0)),
            pl.BlockSpec((None, D_MODEL, D_MODEL), lambda l, j: (l, 0, j)),
            pl.BlockSpec((None, None, 1, D_MODEL), lambda l, j: (l, j, 0, 0)),
        ],
        out_specs=pl.BlockSpec((None, None, batch, D_MODEL), lambda l, j: (l, j, 0, 0)),
        out_shape=jax.ShapeDtypeStruct((DEPTH, 6, batch, D_MODEL), _F32),
        compiler_params=_params(2),
        name="adaln_table",
    )(c, ada_w, ada_b.reshape(DEPTH, 6, 1, D_MODEL))


def _mlstm_project_qkg(h, w, dst):
    wqt_ref, wk_ref, _, _, wg_ref, wgt_ref, bcol_ref, brow_ref = w
    qt_s, k_s, _, _, gcol_s, grow_s = dst
    ct = MLSTM_CHUNK
    qt_s[...] = (_dot_nt(wqt_ref[...], h) * (A_DQK ** -0.5)).astype(_BF16)
    k_s[...] = _dot(h, wk_ref[...]).astype(_BF16)
    z = _dot(h, wg_ref[...]) + bcol_ref[...]
    bc = _segment_cumsum(_log_sigmoid(z), 0, ct)
    gcol_s[...] = (z - pltpu.roll(bc, LANES - A_HEADS, 1)) * LOG2E
    zr = _dot_nt(wgt_ref[...], h) + brow_ref[...]
    br = _segment_cumsum(_log_sigmoid(zr), 1, ct)
    sub = lax.broadcasted_iota(jnp.int32, zr.shape, 0)
    grow_s[...] = jnp.where(sub < A_HEADS, zr, br) * LOG2E


def _mlstm_project_v(h, w, dst):
    vt = _dot_nt(w[2][...], h).astype(_BF16)
    for hd in range(A_HEADS):
        dst[2][hd, 0:A_DV, :] = vt[hd * A_DV:(hd + 1) * A_DV, :]


def _mlstm_project_o(h, w, dst):
    dst[3][...] = _dot_nt(w[3][...], h)


def _mlstm_layer_kernel(xcur_ref, xprev_ref, sh_ref, sc_ref, g1prev_ref, g1cur_ref, wqt_ref, wk_ref,
                        wvt_ref, wot_ref, wg_ref, wgt_ref, bcol_ref, brow_ref, gain_ref, wout_ref,
                        out_ref, qt_a, k_a, vt_a, ot_a, gcol_a, grow_a, qt_b, k_b, vt_b, ot_b,
                        gcol_b, grow_b, cn_scr, m_scr, z_scr, hold_scr, *, n_chunks):
    L = MLSTM_CHUNK
    heads = range(A_HEADS)
    t = pl.program_id(0)
    weights = (wqt_ref, wk_ref, wvt_ref, wot_ref, wg_ref, wgt_ref, bcol_ref, brow_ref)
    set_a = (qt_a, k_a, vt_a, ot_a, gcol_a, grow_a)
    set_b = (qt_b, k_b, vt_b, ot_b, gcol_b, grow_b)

    @pl.when(t == 0)
    def _():
        for ref in set_b + (hold_scr, cn_scr, m_scr):
            ref[...] = jnp.zeros_like(ref)
        for vt_s in (vt_a, vt_b):
            vt_s[:, A_DV:AV_ROWS, :] = jnp.ones((A_HEADS, AV_ROWS - A_DV, L), _BF16)

    causal = (lax.broadcasted_iota(jnp.int32, (L, L), 0)
              <= lax.broadcasted_iota(jnp.int32, (L, L), 1))

    def step(dst, src, x_proj, x_res, g1, fresh, store):
        qt_s, k_s, vt_s, ot_s, gcol_s, grow_s = src
        gcol = gcol_s[...]
        grow = grow_s[...]
        kh = [k_s[:, h * A_DQK:(h + 1) * A_DQK] for h in heads]
        qt = [qt_s[h * A_DQK:(h + 1) * A_DQK, :] for h in heads]
        if fresh is None:
            cn_prev = [cn_scr[h] for h in heads]
            m_prev = [m_scr[h][:, 0:1] for h in heads]
        else:
            cn_prev = [jnp.where(fresh, 0.0, cn_scr[h]) for h in heads]
            m_prev = [jnp.where(fresh, 0.0, m_scr[h][:, 0:1]) for h in heads]
        st = [_dot(kh[h], qt[h]) for h in heads]
        inter_mm = [_dot(cn_prev[h].astype(_BF16), qt[h]) for h in heads]

        i_row = [grow[h:h + 1, :] for h in heads]
        b_row = [grow[A_HEADS + h:A_HEADS + h + 1, :] for h in heads]
        d = [jnp.where(causal, gcol[:, h:h + 1] + b_row[h], -jnp.inf) for h in heads]
        inter = [b_row[h] + m_prev[h] for h in heads]
        m_t = [jnp.maximum(inter[h], jnp.max(d[h], axis=0, keepdims=True)) for h in heads]
        a = [(jnp.exp2(d[h] - m_t[h]) * st[h]).astype(_BF16) for h in heads]
        w_inter = [jnp.exp2(inter[h] - m_t[h]) for h in heads]

        hp = _modulate(x_proj, sh_ref[...], sc_ref[...]).astype(_BF16)
        _mlstm_project_qkg(hp, weights, dst)
        _mlstm_project_v(hp, weights, dst)
        nd = [w_inter[h] * inter_mm[h] + _dot(vt_s[h], a[h]) for h in heads]
        _mlstm_project_o(hp, weights, dst)

        b_last = [b_row[h][:, L - 1:L] for h in heads]
        dl = [b_last[h] - b_row[h] + i_row[h] for h in heads]
        m_new = [jnp.maximum(b_last[h] + m_prev[h], jnp.max(dl[h], axis=1, keepdims=True))
                 for h in heads]
        for h in heads:
            rows = slice(h * A_DV, (h + 1) * A_DV)
            den = nd[h][A_DV:A_DV + 1, :]
            ht = nd[h][0:A_DV, :] / jnp.maximum(jnp.abs(den), jnp.exp2(-m_t[h]))
            ht = (ht * lax.rsqrt(jnp.mean(ht * ht, axis=0, keepdims=True) + EPS)
                  * gain_ref[rows, :])
            z_scr[rows, :] = (jax.nn.sigmoid(ot_s[rows, :]) * ht).astype(_BF16)
        store(x_res + g1 * _dot_tn(z_scr[...], wout_ref[...]))
        for h in heads:
            vw = (vt_s[h].astype(_F32) * jnp.exp2(dl[h] - m_new[h])).astype(_BF16)
            decay = jnp.exp2(b_last[h] + m_prev[h] - m_new[h])
            cn_scr[h] = decay * cn_prev[h] + _dot(vw, kh[h])
            m_scr[h] = jnp.broadcast_to(m_new[h], (1, LANES))

    def store_second_half(y):
        out_ref[L:2 * L, :] = y

    def store_held(y):
        hold_scr[...] = y

    out_ref[0:L, :] = hold_scr[...]
    step(set_a, set_b, xcur_ref[0:L, :], xprev_ref[L:2 * L, :], g1prev_ref[...], None,
         store_second_half)
    step(set_b, set_a, xcur_ref[L:2 * L, :], xcur_ref[0:L, :], g1cur_ref[...],
         lax.rem(2 * t, n_chunks) == 0, store_held)


def _mlstm_layer(x, mod, layer, w_in, b_i, b_f, head_gain, w_out):
    batch, seq, _ = x.shape
    L = MLSTM_CHUNK
    nc = seq // L
    total = batch * nc
    w = w_in.astype(_BF16)
    wqt, wk = w[:, :A_QK_W].T, w[:, A_QK_W:2 * A_QK_W]
    wvt = w[:, 2 * A_QK_W:2 * A_QK_W + A_V_W].T
    wot = w[:, 2 * A_QK_W + A_V_W:2 * A_QK_W + 2 * A_V_W].T
    wgate = w[:, 2 * A_QK_W + 2 * A_V_W:]
    wg = jnp.pad(wgate, ((0, 0), (0, LANES - 2 * A_HEADS)))
    bias = jnp.concatenate([b_i, b_f]).astype(_F32)
    bcol = jnp.pad(bias, (0, LANES - 2 * A_HEADS)).reshape(1, LANES)
    brow = bias.reshape(2 * A_HEADS, 1)
    gain = jnp.broadcast_to(head_gain.reshape(A_V_W, 1).astype(_F32), (A_V_W, L))

    assert nc % 2 == 0
    ppb = nc // 2
    n_pairs = batch * ppb
    cur_pair = lambda t: jnp.minimum(t, n_pairs - 1)
    prev_pair = lambda t: jnp.maximum(t - 1, 0)
    tok = lambda pair: pl.BlockSpec(
        (None, 2 * L, D_MODEL), lambda t: (pair(t) // ppb, lax.rem(pair(t), ppb), 0))
    modrow = lambda slot, pair: pl.BlockSpec(
        (None, None, None, 1, D_MODEL), lambda t: (layer, slot, pair(t) // ppb, 0, 0))
    scratch_set = [
        pltpu.VMEM((A_QK_W, L), _BF16), pltpu.VMEM((L, A_QK_W), _BF16),
        pltpu.VMEM((A_HEADS, AV_ROWS, L), _BF16), pltpu.VMEM((A_V_W, L), _F32),
        pltpu.VMEM((L, LANES), _F32), pltpu.VMEM((2 * A_HEADS, L), _F32),
    ]
    return pl.pallas_call(
        functools.partial(_mlstm_layer_kernel, n_chunks=nc),
        grid=(n_pairs + 1,),
        in_specs=[
            tok(cur_pair), tok(prev_pair),
            modrow(0, cur_pair), modrow(1, cur_pair), modrow(2, prev_pair), modrow(2, cur_pair),
            _const_spec((A_QK_W, D_MODEL)), _const_spec((D_MODEL, A_QK_W)),
            _const_spec((A_V_W, D_MODEL)), _const_spec((A_V_W, D_MODEL)),
            _const_spec((D_MODEL, LANES)), _const_spec((2 * A_HEADS, D_MODEL)),
            _const_spec((1, LANES)), _const_spec((2 * A_HEADS, 1)),
            _const_spec((A_V_W, L)), _const_spec((A_V_W, D_MODEL)),
        ],
        out_specs=tok(prev_pair),
        out_shape=jax.ShapeDtypeStruct(x.shape, _F32),
        scratch_shapes=scratch_set + scratch_set + [
            pltpu.VMEM((A_HEADS, AV_ROWS, A_DQK), _F32),
            pltpu.VMEM((A_HEADS, 1, LANES), _F32),
            pltpu.VMEM((A_V_W, L), _BF16),
            pltpu.VMEM((L, D_MODEL), _F32),
        ],
        compiler_params=_params(1),
        name="mlstm_layer",
    )(x, x, mod, mod, mod, mod, wqt, wk, wvt, wot, wg, wgate.T, bcol, brow, gain,
      w_out.astype(_BF16))


def _mlp_kernel(x_ref, sh_ref, sc_ref, g_ref, w1_hbm, w2_hbm, fgain_ref, out_ref,
                w1_scr, w2_scr, stage1, stage2, sems, u_scr, *, layer, final_norm):
    n_slabs = D_FF // W_SLAB

    def slab_copy(n, slot):
        if n < n_slabs:
            return pltpu.make_async_copy(w1_hbm.at[layer, :, pl.ds(n * W_SLAB, W_SLAB)],
                                         stage1.at[slot], sems.at[slot])
        n -= n_slabs
        return pltpu.make_async_copy(w2_hbm.at[layer, pl.ds(n * W_SLAB, W_SLAB), :],
                                     stage2.at[slot], sems.at[slot])

    @pl.when((pl.program_id(0) == 0) & (pl.program_id(1) == 0))
    def _():
        slab_copy(0, 0).start()
        for n in range(2 * n_slabs):
            slot = n % 2
            if n + 1 < 2 * n_slabs:
                slab_copy(n + 1, 1 - slot).start()
            slab_copy(n, slot).wait()
            if n < n_slabs:
                w1_scr[:, n * W_SLAB:(n + 1) * W_SLAB] = stage1[slot].astype(_BF16)
            else:
                m = n - n_slabs
                w2_scr[m * W_SLAB:(m + 1) * W_SLAB, :] = stage2[slot].astype(_BF16)

    x = x_ref[...]
    h = _modulate(x, sh_ref[...], sc_ref[...]).astype(_BF16)
    for c in range(D_FF // FF_CHUNK):
        u = jnp.maximum(_dot(h, w1_scr[:, c * FF_CHUNK:(c + 1) * FF_CHUNK]), 0.0)
        u_scr[:, c * FF_CHUNK:(c + 1) * FF_CHUNK] = (u * u).astype(_BF16)
    y = x + g_ref[...] * _dot(u_scr[...], w2_scr[...])
    if final_norm:
        y = _rms(y) * fgain_ref[...]
    out_ref[...] = y


def _mlp(x, mod, layer, w1, w2, final_gain, final_norm):
    batch, seq, _ = x.shape
    tm = MLP_TILE
    tok = pl.BlockSpec((None, tm, D_MODEL), lambda b, i: (b, i, 0))
    return pl.pallas_call(
        functools.partial(_mlp_kernel, layer=layer, final_norm=final_norm),
        grid=(batch, seq // tm),
        in_specs=[
            tok, _mod_spec(layer, 3), _mod_spec(layer, 4), _mod_spec(layer, 5),
            pl.BlockSpec(memory_space=pl.ANY), pl.BlockSpec(memory_space=pl.ANY),
            _const_spec((1, D_MODEL)),
        ],
        out_specs=tok,
        out_shape=jax.ShapeDtypeStruct(x.shape, _F32),
        scratch_shapes=[
            pltpu.VMEM((D_MODEL, D_FF), _BF16), pltpu.VMEM((D_FF, D_MODEL), _BF16),
            pltpu.VMEM((2, D_MODEL, W_SLAB), _F32), pltpu.VMEM((2, W_SLAB, D_MODEL), _F32),
            pltpu.SemaphoreType.DMA((2,)),
            pltpu.VMEM((tm, D_FF), _BF16),
        ],
        compiler_params=_params(2),
        name="mlp",
    )(x, mod, mod, mod, w1, w2, final_gain.reshape(1, D_MODEL).astype(_F32))


def _split_bf16(x):
    hi = x.astype(_BF16).astype(_F32)
    rest = x - hi
    mid = rest.astype(_BF16).astype(_F32)
    lo = (rest - mid).astype(_BF16).astype(_F32)
    return hi, mid, lo


N_BIAS = 3
MAX_LOG2_WEIGHT = 60.0
BOUND_SLACK = 1.0 + 2.0 ** -6


def _bias_placement():
    place = np.zeros((LANES, B_HEADS * LANES), np.float32)
    for hd in range(B_HEADS):
        spare = hd * LANES + (0 if hd % 2 else B_DH)
        for piece in range(N_BIAS):
            place[piece * B_HEADS + hd, spare + piece] = -1.0
            place[N_BIAS * B_HEADS, spare + N_BIAS + piece] = 1.0
    return jnp.asarray(place, _BF16)


def _fox_kv_kernel(x_ref, gain_ref, wk_ref, wvt_ref, wf_ref, wft_ref, bcol_ref, brow_ref,
                   place_ref, kaug_ref, vt_ref, frow_ref, kmax2_ref, ccol_scr, crow_scr):
    tm = KV_TILE

    @pl.when(pl.program_id(1) == 0)
    def _():
        ccol_scr[...] = jnp.zeros_like(ccol_scr)
        crow_scr[...] = jnp.zeros_like(crow_scr)
        kmax2_ref[...] = jnp.zeros_like(kmax2_ref)

    ones = jnp.ones((V_ROWS - B_DH, tm), _BF16)
    lane = lax.broadcasted_iota(jnp.int32, (tm, LANES), 1)
    pair_lane = lax.broadcasted_iota(jnp.int32, (1, LANES), 1)
    carry_col = ccol_scr[...]
    carry_row = crow_scr[:, 0:1]
    kmax2 = kmax2_ref[...]
    def project(blk):
        nonlocal carry_col, carry_row
        h = (_rms(x_ref[blk * tm:(blk + 1) * tm, :]) * gain_ref[...]).astype(_BF16)
        vt = _dot_nt(wvt_ref[...], h).astype(_BF16)
        for hd in range(B_HEADS):
            vt_ref[blk, hd, 0:B_DH, :] = vt[hd * B_DH:(hd + 1) * B_DH, :]
            vt_ref[blk, hd, B_DH:V_ROWS, :] = ones
        k = _dot(h, wk_ref[...])
        fc = _segment_cumsum(_log_sigmoid(_dot(h, wf_ref[...]) + bcol_ref[...]), 0, tm)
        fc = fc + carry_col
        carry_col = fc[tm - 1:tm, :]
        fr = _segment_cumsum(_log_sigmoid(_dot_nt(wft_ref[...], h) + brow_ref[...]), 1, tm)
        fr = fr + carry_row
        carry_row = fr[:, tm - 1:tm]
        frow_ref[blk] = fr * LOG2E
        return k, fc

    def augment(blk, k, fc):
        nonlocal kmax2
        hi, mid, lo = _split_bf16(fc * LOG2E)
        pieces = jnp.where(lane < B_HEADS, hi, jnp.where(
            lane < 2 * B_HEADS, pltpu.roll(mid, B_HEADS, 1), jnp.where(
                lane < 3 * B_HEADS, pltpu.roll(lo, 2 * B_HEADS, 1), jnp.where(
                    lane == 3 * B_HEADS, 1.0, 0.0)))).astype(_BF16)
        bias = _dot(pieces, place_ref[...])
        for hd in range(B_HEADS):
            pair, odd = divmod(hd, 2)
            own = (lane >= B_DH) if odd else (lane < B_DH)
            kaug_ref[blk, hd] = jnp.where(
                own, k[:, pair * LANES:(pair + 1) * LANES],
                bias[:, hd * LANES:(hd + 1) * LANES]).astype(_BF16)
        for pair in range(B_HEADS // 2):
            kb = k[:, pair * LANES:(pair + 1) * LANES]
            n2 = jnp.max(jnp.sum(kb * kb, axis=1, keepdims=True), axis=0, keepdims=True)
            kmax2 = jnp.where(pair_lane == pair, jnp.maximum(kmax2, n2), kmax2)

    projected = project(0)
    for blk in range(KV_STEP):
        pending = projected
        if blk + 1 < KV_STEP:
            projected = project(blk + 1)
        augment(blk, *pending)
    ccol_scr[...] = carry_col
    crow_scr[...] = jnp.broadcast_to(carry_row, crow_scr.shape)
    kmax2_ref[...] = kmax2


def _fox_kv(x, kv_gain, w_kv, fg_bias):
    batch, seq, _ = x.shape
    tm = KV_TILE
    w = w_kv.astype(_BF16)
    wk, wvt, wfg = w[:, :D_MODEL], w[:, D_MODEL:2 * D_MODEL].T, w[:, 2 * D_MODEL:]
    wf = jnp.pad(wfg, ((0, 0), (0, LANES - B_HEADS)))
    wft = wfg.T
    bias = fg_bias.astype(_F32)
    bcol = jnp.pad(bias, (0, LANES - B_HEADS)).reshape(1, LANES)
    brow = bias.reshape(B_HEADS, 1)
    ks = KV_STEP
    return pl.pallas_call(
        _fox_kv_kernel,
        grid=(batch, seq // (ks * tm)),
        in_specs=[
            pl.BlockSpec((None, ks * tm, D_MODEL), lambda b, i: (b, i, 0)),
            _const_spec((1, D_MODEL)),
            _const_spec((D_MODEL, D_MODEL)), _const_spec((D_MODEL, D_MODEL)),
            _const_spec((D_MODEL, LANES)), _const_spec((B_HEADS, D_MODEL)),
            _const_spec((1, LANES)), _const_spec((B_HEADS, 1)),
            _const_spec((LANES, B_HEADS * LANES)),
        ],
        out_specs=[
            pl.BlockSpec((None, ks, B_HEADS, tm, LANES), lambda b, i: (b, i, 0, 0, 0)),
            pl.BlockSpec((None, ks, B_HEADS, V_ROWS, tm), lambda b, i: (b, i, 0, 0, 0)),
            pl.BlockSpec((None, ks, B_HEADS, tm), lambda b, i: (b, i, 0, 0)),
            pl.BlockSpec((None, 1, LANES), lambda b, i: (b, 0, 0)),
        ],
        out_shape=[
            jax.ShapeDtypeStruct((batch, seq // tm, B_HEADS, tm, LANES), _BF16),
            jax.ShapeDtypeStruct((batch, seq // tm, B_HEADS, V_ROWS, tm), _BF16),
            jax.ShapeDtypeStruct((batch, seq // tm, B_HEADS, tm), _F32),
            jax.ShapeDtypeStruct((batch, 1, LANES), _F32),
        ],
        scratch_shapes=[pltpu.VMEM((1, LANES), _F32), pltpu.VMEM((B_HEADS, LANES), _F32)],
        compiler_params=_params(2),
        name="fox_kv",
    )(x, kv_gain.reshape(1, D_MODEL).astype(_F32), wk, wvt, wf, wft, bcol, brow,
      _bias_placement())


def _fox_attn_kernel(x_ref, sh_ref, sc_ref, g1_ref, wqt_ref, kaug_ref, vt_ref, frow_ref,
                     kmax2_ref, wout_ref, out_ref, qaug_scr, m_scr, acc_scr, o_scr):
    assert Q_STEP % 2 == 0
    for s in range(Q_STEP):
        rows = pl.ds(s * Q_TILE, Q_TILE)
        _fox_attn_block(pl.program_id(1) * Q_STEP + s, s % 2, x_ref.at[rows], sh_ref, sc_ref,
                        g1_ref, wqt_ref, kaug_ref, vt_ref, frow_ref, kmax2_ref, wout_ref,
                        out_ref.at[rows], qaug_scr, m_scr, acc_scr, o_scr)


def _fox_attn_block(qi, qi_parity, x_ref, sh_ref, sc_ref, g1_ref, wqt_ref, kaug_ref, vt_ref,
                    frow_ref, kmax2_ref, wout_ref, out_ref, qaug_scr, m_scr, acc_scr, o_scr):
    tq, tk = Q_TILE, KV_TILE
    n_groups = B_HEADS // HEAD_GROUP
    h = _modulate(x_ref[...], sh_ref[...], sc_ref[...]).astype(_BF16)
    ft = frow_ref[qi]
    sub = lax.broadcasted_iota(jnp.int32, (B_DH, tq), 0)
    slab = Q_PROJ_HEADS * B_DH
    qt = [(_dot_nt(wqt_ref[r * slab:(r + 1) * slab, :], h)
           * (B_DH ** -0.5 * LOG2E)).astype(_BF16)
          for r in range(B_HEADS // Q_PROJ_HEADS)]
    kmax2 = kmax2_ref[...]
    score_bound = []
    for hd in range(B_HEADS):
        hi, mid, lo = _split_bf16(ft[hd:hd + 1, :])
        bias = jnp.where(sub < N_BIAS, 1.0, jnp.where(sub == N_BIAS, hi, jnp.where(
            sub == N_BIAS + 1, mid, jnp.where(sub == N_BIAS + 2, lo, 0.0)))).astype(_BF16)
        r, off = divmod(hd, Q_PROJ_HEADS)
        qh = qt[r][off * B_DH:(off + 1) * B_DH, :]
        lo_half, hi_half = (bias, qh) if hd % 2 else (qh, bias)
        qaug_scr[hd, 0:B_DH, :] = lo_half
        qaug_scr[hd, B_DH:2 * B_DH, :] = hi_half
        qf = qh.astype(_F32)
        qn2 = jnp.sum(qf * qf, axis=0, keepdims=True)
        score_bound.append(jnp.sqrt(qn2 * kmax2[:, hd // 2:hd // 2 + 1]) * BOUND_SLACK + 1.0)
    m_scr[...] = jnp.full(m_scr.shape, -jnp.inf, _F32)
    acc_scr[...] = jnp.zeros_like(acc_scr)

    def kv_blocks(blocks, mode):
        if mode == "diagonal":
            keep = (lax.broadcasted_iota(jnp.int32, (tk, tq), 1)
                    >= lax.broadcasted_iota(jnp.int32, (tk, tq), 0))
        def scores(unit):
            j, grp = unit
            return [_dot(kaug_ref[j, hd], qaug_scr[hd])
                    for hd in range(grp * HEAD_GROUP, (grp + 1) * HEAD_GROUP)]

        units = [(j, grp) for j in blocks for grp in range(n_groups)]
        s_next = scores(units[0])
        for n, (j, grp) in enumerate(units):
            heads = range(grp * HEAD_GROUP, (grp + 1) * HEAD_GROUP)
            s_cur = s_next
            if n + 1 < len(units):
                s_next = scores(units[n + 1])
            if mode == "fixed":
                for i, hd in enumerate(heads):
                    p_i = jnp.exp2(s_cur[i] - m_scr[hd]).astype(_BF16)
                    acc_scr[hd] = acc_scr[hd] + _dot(vt_ref[j, hd], p_i)
                continue
            if mode == "diagonal":
                s_cur = [jnp.where(keep, s, -jnp.inf) for s in s_cur]
            m_old = [m_scr[hd] for hd in heads]
            m_new = [jnp.maximum(mo, jnp.max(s, axis=0, keepdims=True))
                     for mo, s in zip(m_old, s_cur)]
            alpha = [jnp.exp2(mo - mn) for mo, mn in zip(m_old, m_new)]
            p = [jnp.exp2(s - mn) for s, mn in zip(s_cur, m_new)]
            for i, hd in enumerate(heads):
                m_scr[hd] = m_new[i]
                acc_scr[hd] = alpha[i] * acc_scr[hd] + _dot(vt_ref[j, hd], p[i].astype(_BF16))

    kv_blocks([qi], "diagonal")
    margin = m_scr[0] + MAX_LOG2_WEIGHT - score_bound[0]
    for hd in range(1, B_HEADS):
        margin = jnp.minimum(margin, m_scr[hd] + MAX_LOG2_WEIGHT - score_bound[hd])
    bounded = jnp.min(margin) >= 0.0

    @pl.when(bounded)
    def _():
        def pair(t, carry):
            kv_blocks([2 * t, 2 * t + 1], "fixed")
            return carry
        lax.fori_loop(0, qi // 2, pair, 0)
        if qi_parity:
            kv_blocks([qi - 1], "fixed")

    @pl.when(jnp.logical_not(bounded))
    def _():
        def single(j, carry):
            kv_blocks([j], "rescale")
            return carry
        lax.fori_loop(0, qi, single, 0)

    for hd in range(B_HEADS):
        o_scr[hd * B_DH:(hd + 1) * B_DH, :] = (
            acc_scr[hd, 0:B_DH, :] / acc_scr[hd, B_DH:B_DH + 1, :]).astype(_BF16)
    out_ref[...] = x_ref[...] + g1_ref[...] * _dot_tn(o_scr[...], wout_ref[...])


def _fox_attn(x, mod, layer, w_q, w_out, kaug, vt, frow, kmax2):
    batch, seq, _ = x.shape
    tq, tk = Q_TILE, KV_TILE
    nkv = seq // tk
    tok = pl.BlockSpec((None, Q_STEP * tq, D_MODEL), lambda b, i: (b, i, 0))
    return pl.pallas_call(
        _fox_attn_kernel,
        grid=(batch, seq // (Q_STEP * tq)),
        in_specs=[
            tok, _mod_spec(layer, 0), _mod_spec(layer, 1), _mod_spec(layer, 2),
            _const_spec((D_MODEL, D_MODEL)),
            pl.BlockSpec((None, nkv, B_HEADS, tk, LANES), lambda b, i: (b, 0, 0, 0, 0)),
            pl.BlockSpec((None, nkv, B_HEADS, V_ROWS, tk), lambda b, i: (b, 0, 0, 0, 0)),
            pl.BlockSpec((None, nkv, B_HEADS, tk), lambda b, i: (b, 0, 0, 0)),
            pl.BlockSpec((None, 1, LANES), lambda b, i: (b, 0, 0)),
            _const_spec((D_MODEL, D_MODEL)),
        ],
        out_specs=tok,
        out_shape=jax.ShapeDtypeStruct(x.shape, _F32),
        scratch_shapes=[
            pltpu.VMEM((B_HEADS, 2 * B_DH, tq), _BF16),
            pltpu.VMEM((B_HEADS, 1, tq), _F32),
            pltpu.VMEM((B_HEADS, V_ROWS, tq), _F32), pltpu.VMEM((D_MODEL, tq), _BF16),
        ],
        compiler_params=_params(2),
        name="fox_attn",
    )(x, mod, mod, mod, w_q.T.astype(_BF16), kaug, vt, frow, kmax2, w_out.astype(_BF16))


def kernel(x, c, ada_w, ada_b, a_w_in, a_b_i, a_b_f, a_head_gain, a_w_out, kv_gain, b_w_kv,
           b_fg_bias, b_w_q, b_w_out, mlp_w1, mlp_w2, final_gain):
    batch, seq, d = x.shape
    assert d == D_MODEL and Q_TILE == KV_TILE
    assert all(seq % t == 0
               for t in (MLP_TILE, MLSTM_CHUNK, Q_STEP * Q_TILE, KV_STEP * KV_TILE))
    mod = _adaln_table(c, ada_w, ada_b).reshape(DEPTH, 6, batch, 1, D_MODEL)
    shared = None
    for l in range(DEPTH):
        if l < N_A_LAYERS:
            x = _mlstm_layer(x, mod, l, a_w_in[l], a_b_i[l], a_b_f[l], a_head_gain[l], a_w_out[l])
        else:
            if shared is None:
                shared = _fox_kv(x, kv_gain, b_w_kv, b_fg_bias)
            j = l - N_A_LAYERS
            x = _fox_attn(x, mod, l, b_w_q[j], b_w_out[j], *shared)
        x = _mlp(x, mod, l, mlp_w1, mlp_w2, final_gain, final_norm=(l == DEPTH - 1))
    return x
```

```python
import functools

import jax
import jax.numpy as jnp
import numpy as np
from jax import lax
from jax.experimental import pallas as pl
from jax.experimental.pallas import tpu as pltpu

D_MODEL = 1024
DEPTH = 4
N_A_LAYERS = DEPTH // 2
A_HEADS = 4
A_DV = D_MODEL // A_HEADS
A_DQK = A_DV // 2
A_QK_W = A_HEADS * A_DQK
A_V_W = A_HEADS * A_DV
B_HEADS = 16
B_DH = D_MODEL // B_HEADS
D_FF = 4 * D_MODEL
EPS = 1e-6
LOG2E = 1.4426950408889634
LANES = 128
BF16_ROWS = 16
AV_ROWS = A_DV + BF16_ROWS
V_ROWS = B_DH + BF16_ROWS

MLSTM_CHUNK = 256
KV_TILE = 256
KV_STEP = 4
Q_TILE = 256
Q_STEP = 4
HEAD_GROUP = 8
Q_PROJ_HEADS = 4
MLP_TILE = 1024
FF_CHUNK = 512
VMEM_LIMIT = 56 * 1024 * 1024

_BF16 = jnp.bfloat16
_F32 = jnp.float32


def _dot(a, b):
    return jnp.dot(a, b, preferred_element_type=_F32)


def _dot_nt(a, b):
    return lax.dot_general(a, b, (((1,), (1,)), ((), ())), preferred_element_type=_F32)


def _dot_tn(a, b):
    return lax.dot_general(a, b, (((0,), (0,)), ((), ())), preferred_element_type=_F32)


def _rms(x):
    return x * lax.rsqrt(jnp.mean(x * x, axis=-1, keepdims=True) + EPS)


def _modulate(x, shift, scale):
    return _rms(x) * (1.0 + scale) + shift


def _log_sigmoid(z):
    return jnp.minimum(z, 0.0) - jnp.log1p(jnp.exp(-jnp.abs(z)))


def _segment_cumsum(x, axis, seg):
    pos = lax.broadcasted_iota(jnp.int32, x.shape, axis) & (seg - 1)
    k = 1
    while k < seg:
        x = x + jnp.where(pos >= k, pltpu.roll(x, k, axis), 0.0)
        k *= 2
    return x


def _params(n_grid):
    return pltpu.CompilerParams(dimension_semantics=("arbitrary",) * n_grid,
                                vmem_limit_bytes=VMEM_LIMIT)


def _const_spec(shape):
    return pl.BlockSpec(shape, lambda *_: (0,) * len(shape), pipeline_mode=pl.Buffered(1))


def _mod_spec(layer, slot):
    return pl.BlockSpec((None, None, None, 1, D_MODEL), lambda b, i: (layer, slot, b, 0, 0))


def _adaln_kernel(c_ref, w_ref, b_ref, o_ref):
    c = c_ref[...]
    cond = (c * jax.nn.sigmoid(c)).astype(_BF16)
    o_ref[...] = _dot(cond, w_ref[...].astype(_BF16)) + b_ref[...]


def _adaln_table(c, ada_w, ada_b):
    batch = c.shape[0]
    return pl.pallas_call(
        _adaln_kernel,
        grid=(DEPTH, 6),
        in_specs=[
            pl.BlockSpec((batch, D_MODEL), lambda l, j: (0, 0)),
            pl.BlockSpec((None, D_MODEL, D_MODEL), lambda l, j: (l, 0, j)),
            pl.BlockSpec((None, None, 1, D_MODEL), lambda l, j: (l, j, 0, 0)),
        ],
        out_specs=pl.BlockSpec((None, None, batch, D_MODEL), lambda l, j: (l, j, 0, 0)),
        out_shape=jax.ShapeDtypeStruct((DEPTH, 6, batch, D_MODEL), _F32),
        compiler_params=_params(2),
        name="adaln_table",
    )(c, ada_w, ada_b.reshape(DEPTH, 6, 1, D_MODEL))


def _mlstm_project_qkg(h, w, dst):
    wqt_ref, wk_ref, _, _, wg_ref, bcol_ref = w
    qt_s, k_s, _, _, gcol_s, grow_s = dst
    ct = MLSTM_CHUNK
    qt_s[...] = (_dot_nt(wqt_ref[...], h) * (A_DQK ** -0.5)).astype(_BF16)
    k_s[...] = _dot(h, wk_ref[...]).astype(_BF16)
    z = _dot(h, wg_ref[...]) + bcol_ref[...]
    bc = _segment_cumsum(_log_sigmoid(z), 0, ct)
    gcol_s[...] = (z - pltpu.roll(bc, LANES - A_HEADS, 1)) * LOG2E
    zr = z.T[:2 * A_HEADS, :]
    br = _segment_cumsum(_log_sigmoid(zr), 1, ct)
    sub = lax.broadcasted_iota(jnp.int32, zr.shape, 0)
    grow_s[...] = jnp.where(sub < A_HEADS, zr, br) * LOG2E


def _mlstm_project_v(h, w, dst):
    vt = _dot_nt(w[2][...], h).astype(_BF16)
    for hd in range(A_HEADS):
        dst[2][hd, 0:A_DV, :] = vt[hd * A_DV:(hd + 1) * A_DV, :]


def _mlstm_project_o(h, w, dst):
    dst[3][...] = _dot_nt(w[3][...], h)


def _mlstm_layer_kernel(xcur_ref, xprev_ref, sh_ref, sc_ref, g1prev_ref, g1cur_ref, wqt_ref, wk_ref,
                        wvt_ref, wot_ref, wg_ref, bcol_ref, gain_ref, wout_ref,
                        out_ref, qt_a, k_a, vt_a, ot_a, gcol_a, grow_a, qt_b, k_b, vt_b, ot_b,
                        gcol_b, grow_b, cn_scr, m_scr, z_scr, hold_scr, *, n_chunks):
    L = MLSTM_CHUNK
    heads = range(A_HEADS)
    t = pl.program_id(0)
    weights = (wqt_ref, wk_ref, wvt_ref, wot_ref, wg_ref, bcol_ref)
    set_a = (qt_a, k_a, vt_a, ot_a, gcol_a, grow_a)
    set_b = (qt_b, k_b, vt_b, ot_b, gcol_b, grow_b)

    @pl.when(t == 0)
    def _():
        for ref in set_b + (hold_scr, cn_scr, m_scr):
            ref[...] = jnp.zeros_like(ref)
        for vt_s in (vt_a, vt_b):
            vt_s[:, A_DV:AV_ROWS, :] = jnp.ones((A_HEADS, AV_ROWS - A_DV, L), _BF16)

    causal = (lax.broadcasted_iota(jnp.int32, (L, L), 0)
              <= lax.broadcasted_iota(jnp.int32, (L, L), 1))

    def step(dst, src, x_proj, x_res, g1, fresh, store):
        qt_s, k_s, vt_s, ot_s, gcol_s, grow_s = src
        gcol = gcol_s[...]
        grow = grow_s[...]
        kh = [k_s[:, h * A_DQK:(h + 1) * A_DQK] for h in heads]
        qt = [qt_s[h * A_DQK:(h + 1) * A_DQK, :] for h in heads]
        if fresh is None:
            cn_prev = [cn_scr[h] for h in heads]
            m_prev = [m_scr[h][:, 0:1] for h in heads]
        else:
            cn_prev = [jnp.where(fresh, 0.0, cn_scr[h]) for h in heads]
            m_prev = [jnp.where(fresh, 0.0, m_scr[h][:, 0:1]) for h in heads]
        st = [_dot(kh[h], qt[h]) for h in heads]
        inter_mm = [_dot(cn_prev[h].astype(_BF16), qt[h]) for h in heads]

        i_row = [grow[h:h + 1, :] for h in heads]
        b_row = [grow[A_HEADS + h:A_HEADS + h + 1, :] for h in heads]
        d = [jnp.where(causal, gcol[:, h:h + 1] + b_row[h], -jnp.inf) for h in heads]
        inter = [b_row[h] + m_prev[h] for h in heads]
        m_t = [jnp.maximum(inter[h], jnp.max(d[h], axis=0, keepdims=True)) for h in heads]
        a = [(jnp.exp2(d[h] - m_t[h]) * st[h]).astype(_BF16) for h in heads]
        w_inter = [jnp.exp2(inter[h] - m_t[h]) for h in heads]

        hp = _modulate(x_proj, sh_ref[...], sc_ref[...]).astype(_BF16)
        _mlstm_project_qkg(hp, weights, dst)
        _mlstm_project_v(hp, weights, dst)
        nd = [w_inter[h] * inter_mm[h] + _dot(vt_s[h], a[h]) for h in heads]
        _mlstm_project_o(hp, weights, dst)

        b_last = [b_row[h][:, L - 1:L] for h in heads]
        dl = [b_last[h] - b_row[h] + i_row[h] for h in heads]
        m_new = [jnp.maximum(b_last[h] + m_prev[h], jnp.max(dl[h], axis=1, keepdims=True))
                 for h in heads]
        for h in heads:
            rows = slice(h * A_DV, (h + 1) * A_DV)
            den = nd[h][A_DV:A_DV + 1, :]
            ht = nd[h][0:A_DV, :] / jnp.maximum(jnp.abs(den), jnp.exp2(-m_t[h]))
            ht = (ht * lax.rsqrt(jnp.mean(ht * ht, axis=0, keepdims=True) + EPS)
                  * gain_ref[rows, :])
            z_scr[rows, :] = (jax.nn.sigmoid(ot_s[rows, :]) * ht).astype(_BF16)
        store(x_res + g1 * _dot_tn(z_scr[...], wout_ref[...]))
        for h in heads:
            vw = (vt_s[h].astype(_F32) * jnp.exp2(dl[h] - m_new[h])).astype(_BF16)
            decay = jnp.exp2(b_last[h] + m_prev[h] - m_new[h])
            cn_scr[h] = decay * cn_prev[h] + _dot(vw, kh[h])
            m_scr[h] = jnp.broadcast_to(m_new[h], (1, LANES))

    def store_second_half(y):
        out_ref[L:2 * L, :] = y

    def store_held(y):
        hold_scr[...] = y

    out_ref[0:L, :] = hold_scr[...]
    step(set_a, set_b, xcur_ref[0:L, :], xprev_ref[L:2 * L, :], g1prev_ref[...], None,
         store_second_half)
    step(set_b, set_a, xcur_ref[L:2 * L, :], xcur_ref[0:L, :], g1cur_ref[...],
         lax.rem(2 * t, n_chunks) == 0, store_held)


def _mlstm_layer(x, mod, layer, w_in, b_i, b_f, head_gain, w_out):
    batch, seq, _ = x.shape
    L = MLSTM_CHUNK
    nc = seq // L
    w = w_in.astype(_BF16)
    wqt, wk = w[:, :A_QK_W].T, w[:, A_QK_W:2 * A_QK_W]
    wvt = w[:, 2 * A_QK_W:2 * A_QK_W + A_V_W].T
    wot = w[:, 2 * A_QK_W + A_V_W:2 * A_QK_W + 2 * A_V_W].T
    wgate = w[:, 2 * A_QK_W + 2 * A_V_W:]
    wg = jnp.pad(wgate, ((0, 0), (0, LANES - 2 * A_HEADS)))
    bias = jnp.concatenate([b_i, b_f]).astype(_F32)
    bcol = jnp.pad(bias, (0, LANES - 2 * A_HEADS)).reshape(1, LANES)
    gain = jnp.broadcast_to(head_gain.reshape(A_V_W, 1).astype(_F32), (A_V_W, L))

    assert nc % 2 == 0
    ppb = nc // 2
    n_pairs = batch * ppb
    cur_pair = lambda t: jnp.minimum(t, n_pairs - 1)
    prev_pair = lambda t: jnp.maximum(t - 1, 0)
    tok = lambda pair: pl.BlockSpec(
        (None, 2 * L, D_MODEL), lambda t: (pair(t) // ppb, lax.rem(pair(t), ppb), 0))
    modrow = lambda slot, pair: pl.BlockSpec(
        (None, None, None, 1, D_MODEL), lambda t: (layer, slot, pair(t) // ppb, 0, 0))
    scratch_set = [
        pltpu.VMEM((A_QK_W, L), _BF16), pltpu.VMEM((L, A_QK_W), _BF16),
        pltpu.VMEM((A_HEADS, AV_ROWS, L), _BF16), pltpu.VMEM((A_V_W, L), _F32),
        pltpu.VMEM((L, LANES), _F32), pltpu.VMEM((2 * A_HEADS, L), _F32),
    ]
    return pl.pallas_call(
        functools.partial(_mlstm_layer_kernel, n_chunks=nc),
        grid=(n_pairs + 1,),
        in_specs=[
            tok(cur_pair), tok(prev_pair),
            modrow(0, cur_pair), modrow(1, cur_pair), modrow(2, prev_pair), modrow(2, cur_pair),
            _const_spec((A_QK_W, D_MODEL)), _const_spec((D_MODEL, A_QK_W)),
            _const_spec((A_V_W, D_MODEL)), _const_spec((A_V_W, D_MODEL)),
            _const_spec((D_MODEL, LANES)), _const_spec((1, LANES)),
            _const_spec((A_V_W, L)), _const_spec((A_V_W, D_MODEL)),
        ],
        out_specs=tok(prev_pair),
        out_shape=jax.ShapeDtypeStruct(x.shape, _F32),
        scratch_shapes=scratch_set + scratch_set + [
            pltpu.VMEM((A_HEADS, AV_ROWS, A_DQK), _F32),
            pltpu.VMEM((A_HEADS, 1, LANES), _F32),
            pltpu.VMEM((A_V_W, L), _BF16),
            pltpu.VMEM((L, D_MODEL), _F32),
        ],
        compiler_params=_params(1),
        name="mlstm_layer",
    )(x, x, mod, mod, mod, mod, wqt, wk, wvt, wot, wg, bcol, gain, w_out.astype(_BF16))


def _mlp_kernel(x_ref, sh_ref, sc_ref, g_ref, w1_ref, w2_ref, fgain_ref, out_ref, u_scr, *,
                final_norm):
    x = x_ref[...]
    h = _modulate(x, sh_ref[...], sc_ref[...]).astype(_BF16)
    for c in range(D_FF // FF_CHUNK):
        u = jnp.maximum(_dot(h, w1_ref[:, c * FF_CHUNK:(c + 1) * FF_CHUNK]), 0.0)
        u_scr[:, c * FF_CHUNK:(c + 1) * FF_CHUNK] = (u * u).astype(_BF16)
    y = x + g_ref[...] * _dot(u_scr[...], w2_ref[...])
    if final_norm:
        y = _rms(y) * fgain_ref[...]
    out_ref[...] = y


def _layer_spec(layer, rows, cols):
    return pl.BlockSpec((None, rows, cols), lambda *_: (layer, 0, 0),
                        pipeline_mode=pl.Buffered(1))


def _mlp(x, mod, layer, w1, w2, final_gain, final_norm):
    batch, seq, _ = x.shape
    tm = MLP_TILE
    tok = pl.BlockSpec((None, tm, D_MODEL), lambda b, i: (b, i, 0))
    return pl.pallas_call(
        functools.partial(_mlp_kernel, final_norm=final_norm),
        grid=(batch, seq // tm),
        in_specs=[
            tok, _mod_spec(layer, 3), _mod_spec(layer, 4), _mod_spec(layer, 5),
            _layer_spec(layer, D_MODEL, D_FF), _layer_spec(layer, D_FF, D_MODEL),
            _const_spec((1, D_MODEL)),
        ],
        out_specs=tok,
        out_shape=jax.ShapeDtypeStruct(x.shape, _F32),
        scratch_shapes=[pltpu.VMEM((tm, D_FF), _BF16)],
        compiler_params=_params(2),
        name="mlp",
    )(x, mod, mod, mod, w1, w2, final_gain.reshape(1, D_MODEL).astype(_F32))


def _split_bf16(x):
    hi = x.astype(_BF16).astype(_F32)
    rest = x - hi
    mid = rest.astype(_BF16).astype(_F32)
    lo = (rest - mid).astype(_BF16).astype(_F32)
    return hi, mid, lo


N_BIAS = 3
MAX_LOG2_WEIGHT = 60.0
BOUND_SLACK = 1.0 + 2.0 ** -6


def _bias_placement():
    place = np.zeros((LANES, B_HEADS * LANES), np.float32)
    for hd in range(B_HEADS):
        spare = hd * LANES + (0 if hd % 2 else B_DH)
        for piece in range(N_BIAS):
            place[piece * B_HEADS + hd, spare + piece] = -1.0
            place[N_BIAS * B_HEADS, spare + N_BIAS + piece] = 1.0
    return jnp.asarray(place, _BF16)


def _fox_kv_kernel(x_ref, gain_ref, wk_ref, wvt_ref, wf_ref, bcol_ref,
                   place_ref, kaug_ref, vt_ref, frow_ref, kmax2_ref, ccol_scr, crow_scr):
    tm = KV_TILE

    @pl.when(pl.program_id(1) == 0)
    def _():
        ccol_scr[...] = jnp.zeros_like(ccol_scr)
        crow_scr[...] = jnp.zeros_like(crow_scr)
        kmax2_ref[...] = jnp.zeros_like(kmax2_ref)

    ones = jnp.ones((V_ROWS - B_DH, tm), _BF16)
    lane = lax.broadcasted_iota(jnp.int32, (tm, LANES), 1)
    pair_lane = lax.broadcasted_iota(jnp.int32, (1, LANES), 1)
    carry_col = ccol_scr[...]
    carry_row = crow_scr[:, 0:1]
    kmax2 = kmax2_ref[...]
    def project(blk):
        nonlocal carry_col, carry_row
        h = (_rms(x_ref[blk * tm:(blk + 1) * tm, :]) * gain_ref[...]).astype(_BF16)
        vt = _dot_nt(wvt_ref[...], h).astype(_BF16)
        for hd in range(B_HEADS):
            vt_ref[blk, hd, 0:B_DH, :] = vt[hd * B_DH:(hd + 1) * B_DH, :]
            vt_ref[blk, hd, B_DH:V_ROWS, :] = ones
        k = _dot(h, wk_ref[...])
        logf = _log_sigmoid(_dot(h, wf_ref[...]) + bcol_ref[...])
        fc = _segment_cumsum(logf, 0, tm) + carry_col
        carry_col = fc[tm - 1:tm, :]
        fr = _segment_cumsum(logf.T[:B_HEADS, :], 1, tm)
        fr = fr + carry_row
        carry_row = fr[:, tm - 1:tm]
        frow_ref[blk] = fr * LOG2E
        return k, fc

    def augment(blk, k, fc):
        nonlocal kmax2
        hi, mid, lo = _split_bf16(fc * LOG2E)
        pieces = jnp.where(lane < B_HEADS, hi, jnp.where(
            lane < 2 * B_HEADS, pltpu.roll(mid, B_HEADS, 1), jnp.where(
                lane < 3 * B_HEADS, pltpu.roll(lo, 2 * B_HEADS, 1), jnp.where(
                    lane == 3 * B_HEADS, 1.0, 0.0)))).astype(_BF16)
        bias = _dot(pieces, place_ref[...])
        for hd in range(B_HEADS):
            pair, odd = divmod(hd, 2)
            own = (lane >= B_DH) if odd else (lane < B_DH)
            kaug_ref[blk, hd] = jnp.where(
                own, k[:, pair * LANES:(pair + 1) * LANES],
                bias[:, hd * LANES:(hd + 1) * LANES]).astype(_BF16)
        for pair in range(B_HEADS // 2):
            kb = k[:, pair * LANES:(pair + 1) * LANES]
            n2 = jnp.max(jnp.sum(kb * kb, axis=1, keepdims=True), axis=0, keepdims=True)
            kmax2 = jnp.where(pair_lane == pair, jnp.maximum(kmax2, n2), kmax2)

    projected = project(0)
    for blk in range(KV_STEP):
        pending = projected
        if blk + 1 < KV_STEP:
            projected = project(blk + 1)
        augment(blk, *pending)
    ccol_scr[...] = carry_col
    crow_scr[...] = jnp.broadcast_to(carry_row, crow_scr.shape)
    kmax2_ref[...] = kmax2


def _fox_kv(x, kv_gain, w_kv, fg_bias):
    batch, seq, _ = x.shape
    tm = KV_TILE
    w = w_kv.astype(_BF16)
    wk, wvt, wfg = w[:, :D_MODEL], w[:, D_MODEL:2 * D_MODEL].T, w[:, 2 * D_MODEL:]
    wf = jnp.pad(wfg, ((0, 0), (0, LANES - B_HEADS)))
    bcol = jnp.pad(fg_bias.astype(_F32), (0, LANES - B_HEADS)).reshape(1, LANES)
    ks = KV_STEP
    return pl.pallas_call(
        _fox_kv_kernel,
        grid=(batch, seq // (ks * tm)),
        in_specs=[
            pl.BlockSpec((None, ks * tm, D_MODEL), lambda b, i: (b, i, 0)),
            _const_spec((1, D_MODEL)),
            _const_spec((D_MODEL, D_MODEL)), _const_spec((D_MODEL, D_MODEL)),
            _const_spec((D_MODEL, LANES)), _const_spec((1, LANES)),
            _const_spec((LANES, B_HEADS * LANES)),
        ],
        out_specs=[
            pl.BlockSpec((None, ks, B_HEADS, tm, LANES), lambda b, i: (b, i, 0, 0, 0)),
            pl.BlockSpec((None, ks, B_HEADS, V_ROWS, tm), lambda b, i: (b, i, 0, 0, 0)),
            pl.BlockSpec((None, ks, B_HEADS, tm), lambda b, i: (b, i, 0, 0)),
            pl.BlockSpec((None, 1, LANES), lambda b, i: (b, 0, 0)),
        ],
        out_shape=[
            jax.ShapeDtypeStruct((batch, seq // tm, B_HEADS, tm, LANES), _BF16),
            jax.ShapeDtypeStruct((batch, seq // tm, B_HEADS, V_ROWS, tm), _BF16),
            jax.ShapeDtypeStruct((batch, seq // tm, B_HEADS, tm), _F32),
            jax.ShapeDtypeStruct((batch, 1, LANES), _F32),
        ],
        scratch_shapes=[pltpu.VMEM((1, LANES), _F32), pltpu.VMEM((B_HEADS, LANES), _F32)],
        compiler_params=_params(2),
        name="fox_kv",
    )(x, kv_gain.reshape(1, D_MODEL).astype(_F32), wk, wvt, wf, bcol, _bias_placement())


def _fox_attn_kernel(x_ref, sh_ref, sc_ref, g1_ref, wqt_ref, kaug_ref, vt_ref, frow_ref,
                     kmax2_ref, wout_ref, out_ref, qaug_scr, m_scr, acc_scr, o_scr):
    assert Q_STEP % 2 == 0
    for s in range(Q_STEP):
        rows = pl.ds(s * Q_TILE, Q_TILE)
        _fox_attn_block(pl.program_id(1) * Q_STEP + s, s % 2, x_ref.at[rows], sh_ref, sc_ref,
                        g1_ref, wqt_ref, kaug_ref, vt_ref, frow_ref, kmax2_ref, wout_ref,
                        out_ref.at[rows], qaug_scr, m_scr, acc_scr, o_scr)


def _fox_attn_block(qi, qi_parity, x_ref, sh_ref, sc_ref, g1_ref, wqt_ref, kaug_ref, vt_ref,
                    frow_ref, kmax2_ref, wout_ref, out_ref, qaug_scr, m_scr, acc_scr, o_scr):
    tq, tk = Q_TILE, KV_TILE
    n_groups = B_HEADS // HEAD_GROUP
    h = _modulate(x_ref[...], sh_ref[...], sc_ref[...]).astype(_BF16)
    ft = frow_ref[qi]
    sub = lax.broadcasted_iota(jnp.int32, (B_DH, tq), 0)
    slab = Q_PROJ_HEADS * B_DH
    qt = [(_dot_nt(wqt_ref[r * slab:(r + 1) * slab, :], h)
           * (B_DH ** -0.5 * LOG2E)).astype(_BF16)
          for r in range(B_HEADS // Q_PROJ_HEADS)]
    kmax2 = kmax2_ref[...]
    score_bound = []
    for hd in range(B_HEADS):
        hi, mid, lo = _split_bf16(ft[hd:hd + 1, :])
        bias = jnp.where(sub < N_BIAS, 1.0, jnp.where(sub == N_BIAS, hi, jnp.where(
            sub == N_BIAS + 1, mid, jnp.where(sub == N_BIAS + 2, lo, 0.0)))).astype(_BF16)
        r, off = divmod(hd, Q_PROJ_HEADS)
        qh = qt[r][off * B_DH:(off + 1) * B_DH, :]
        lo_half, hi_half = (bias, qh) if hd % 2 else (qh, bias)
        qaug_scr[hd, 0:B_DH, :] = lo_half
        qaug_scr[hd, B_DH:2 * B_DH, :] = hi_half
        qf = qh.astype(_F32)
        qn2 = jnp.sum(qf * qf, axis=0, keepdims=True)
        score_bound.append(jnp.sqrt(qn2 * kmax2[:, hd // 2:hd // 2 + 1]) * BOUND_SLACK + 1.0)
    m_scr[...] = jnp.full(m_scr.shape, -jnp.inf, _F32)
    acc_scr[...] = jnp.zeros_like(acc_scr)

    def kv_blocks(blocks, mode):
        if mode == "diagonal":
            keep = (lax.broadcasted_iota(jnp.int32, (tk, tq), 1)
                    >= lax.broadcasted_iota(jnp.int32, (tk, tq), 0))
        def scores(unit):
            j, grp = unit
            return [_dot(kaug_ref[j, hd], qaug_scr[hd])
                    for hd in range(grp * HEAD_GROUP, (grp + 1) * HEAD_GROUP)]

        units = [(j, grp) for j in blocks for grp in range(n_groups)]
        s_next = scores(units[0])
        for n, (j, grp) in enumerate(units):
            heads = range(grp * HEAD_GROUP, (grp + 1) * HEAD_GROUP)
            s_cur = s_next
            if n + 1 < len(units):
                s_next = scores(units[n + 1])
            if mode == "fixed":
                for i, hd in enumerate(heads):
                    p_i = jnp.exp2(s_cur[i] - m_scr[hd]).astype(_BF16)
                    acc_scr[hd] = acc_scr[hd] + _dot(vt_ref[j, hd], p_i)
                continue
            if mode == "diagonal":
                s_cur = [jnp.where(keep, s, -jnp.inf) for s in s_cur]
            m_old = [m_scr[hd] for hd in heads]
            m_new = [jnp.maximum(mo, jnp.max(s, axis=0, keepdims=True))
                     for mo, s in zip(m_old, s_cur)]
            alpha = [jnp.exp2(mo - mn) for mo, mn in zip(m_old, m_new)]
            p = [jnp.exp2(s - mn) for s, mn in zip(s_cur, m_new)]
            for i, hd in enumerate(heads):
                m_scr[hd] = m_new[i]
                acc_scr[hd] = alpha[i] * acc_scr[hd] + _dot(vt_ref[j, hd], p[i].astype(_BF16))

    kv_blocks([qi], "diagonal")
    margin = m_scr[0] + MAX_LOG2_WEIGHT - score_bound[0]
    for hd in range(1, B_HEADS):
        margin = jnp.minimum(margin, m_scr[hd] + MAX_LOG2_WEIGHT - score_bound[hd])
    bounded = jnp.min(margin) >= 0.0

    @pl.when(bounded)
    def _():
        def pair(t, carry):
            kv_blocks([2 * t, 2 * t + 1], "fixed")
            return carry
        lax.fori_loop(0, qi // 2, pair, 0)
        if qi_parity:
            kv_blocks([qi - 1], "fixed")

    @pl.when(jnp.logical_not(bounded))
    def _():
        def single(j, carry):
            kv_blocks([j], "rescale")
            return carry
        lax.fori_loop(0, qi, single, 0)

    for hd in range(B_HEADS):
        o_scr[hd * B_DH:(hd + 1) * B_DH, :] = (
            acc_scr[hd, 0:B_DH, :] / acc_scr[hd, B_DH:B_DH + 1, :]).astype(_BF16)
    out_ref[...] = x_ref[...] + g1_ref[...] * _dot_tn(o_scr[...], wout_ref[...])


def _fox_attn(x, mod, layer, w_q, w_out, kaug, vt, frow, kmax2):
    batch, seq, _ = x.shape
    tq, tk = Q_TILE, KV_TILE
    nkv = seq // tk
    tok = pl.BlockSpec((None, Q_STEP * tq, D_MODEL), lambda b, i: (b, i, 0))
    return pl.pallas_call(
        _fox_attn_kernel,
        grid=(batch, seq // (Q_STEP * tq)),
        in_specs=[
            tok, _mod_spec(layer, 0), _mod_spec(layer, 1), _mod_spec(layer, 2),
            _const_spec((D_MODEL, D_MODEL)),
            pl.BlockSpec((None, nkv, B_HEADS, tk, LANES), lambda b, i: (b, 0, 0, 0, 0)),
            pl.BlockSpec((None, nkv, B_HEADS, V_ROWS, tk), lambda b, i: (b, 0, 0, 0, 0)),
            pl.BlockSpec((None, nkv, B_HEADS, tk), lambda b, i: (b, 0, 0, 0)),
            pl.BlockSpec((None, 1, LANES), lambda b, i: (b, 0, 0)),
            _const_spec((D_MODEL, D_MODEL)),
        ],
        out_specs=tok,
        out_shape=jax.ShapeDtypeStruct(x.shape, _F32),
        scratch_shapes=[
            pltpu.VMEM((B_HEADS, 2 * B_DH, tq), _BF16),
            pltpu.VMEM((B_HEADS, 1, tq), _F32),
            pltpu.VMEM((B_HEADS, V_ROWS, tq), _F32), pltpu.VMEM((D_MODEL, tq), _BF16),
        ],
        compiler_params=_params(2),
        name="fox_attn",
    )(x, mod, mod, mod, w_q.T.astype(_BF16), kaug, vt, frow, kmax2, w_out.astype(_BF16))


def kernel(x, c, ada_w, ada_b, a_w_in, a_b_i, a_b_f, a_head_gain, a_w_out, kv_gain, b_w_kv,
           b_fg_bias, b_w_q, b_w_out, mlp_w1, mlp_w2, final_gain):
    batch, seq, d = x.shape
    assert d == D_MODEL and Q_TILE == KV_TILE
    assert all(seq % t == 0
               for t in (MLP_TILE, MLSTM_CHUNK, Q_STEP * Q_TILE, KV_STEP * KV_TILE))
    mod = _adaln_table(c, ada_w, ada_b).reshape(DEPTH, 6, batch, 1, D_MODEL)
    w1_all, w2_all = mlp_w1.astype(_BF16), mlp_w2.astype(_BF16)
    shared = None
    for l in range(DEPTH):
        if l < N_A_LAYERS:
            x = _mlstm_layer(x, mod, l, a_w_in[l], a_b_i[l], a_b_f[l], a_head_gain[l], a_w_out[l])
        else:
            if shared is None:
                shared = _fox_kv(x, kv_gain, b_w_kv, b_fg_bias)
            j = l - N_A_LAYERS
            x = _fox_attn(x, mod, l, b_w_q[j], b_w_out[j], *shared)
        x = _mlp(x, mod, l, w1_all, w2_all, final_gain, final_norm=(l == DEPTH - 1))
    return x
```

```python
import functools

import jax
import jax.numpy as jnp
import numpy as np
from jax import lax
from jax.experimental import pallas as pl
from jax.experimental.pallas import tpu as pltpu

D_MODEL = 1024
DEPTH = 4
N_A_LAYERS = DEPTH // 2
A_HEADS = 4
A_DV = D_MODEL // A_HEADS
A_DQK = A_DV // 2
A_QK_W = A_HEADS * A_DQK
A_V_W = A_HEADS * A_DV
B_HEADS = 16
B_DH = D_MODEL // B_HEADS
D_FF = 4 * D_MODEL
EPS = 1e-6
LOG2E = 1.4426950408889634
LANES = 128
BF16_ROWS = 16
AV_ROWS = A_DV + BF16_ROWS
V_ROWS = B_DH + BF16_ROWS

MLSTM_CHUNK = 256
KV_TILE = 256
KV_STEP = 4
Q_TILE = 256
Q_STEP = 4
HEAD_GROUP = 8
Q_PROJ_HEADS = 4
MLP_TILE = 1024
FF_CHUNK = 512
VMEM_LIMIT = 56 * 1024 * 1024

_BF16 = jnp.bfloat16
_F32 = jnp.float32


def _dot(a, b):
    return jnp.dot(a, b, preferred_element_type=_F32)


def _dot_nt(a, b):
    return lax.dot_general(a, b, (((1,), (1,)), ((), ())), preferred_element_type=_F32)


def _dot_tn(a, b):
    return lax.dot_general(a, b, (((0,), (0,)), ((), ())), preferred_element_type=_F32)


def _rms(x):
    return x * lax.rsqrt(jnp.mean(x * x, axis=-1, keepdims=True) + EPS)


def _modulate(x, shift, scale):
    return _rms(x) * (1.0 + scale) + shift


def _log_sigmoid(z):
    return jnp.minimum(z, 0.0) - jnp.log1p(jnp.exp(-jnp.abs(z)))


def _segment_cumsum(x, axis, seg):
    pos = lax.broadcasted_iota(jnp.int32, x.shape, axis) & (seg - 1)
    k = 1
    while k < seg:
        x = x + jnp.where(pos >= k, pltpu.roll(x, k, axis), 0.0)
        k *= 2
    return x


def _params(n_grid):
    return pltpu.CompilerParams(dimension_semantics=("arbitrary",) * n_grid,
                                vmem_limit_bytes=VMEM_LIMIT)


def _const_spec(shape):
    return pl.BlockSpec(shape, lambda *_: (0,) * len(shape), pipeline_mode=pl.Buffered(1))


def _mod_spec(layer, slot):
    return pl.BlockSpec((None, None, None, 1, D_MODEL), lambda b, i: (layer, slot, b, 0, 0))


def _adaln_kernel(c_ref, w_ref, b_ref, o_ref):
    c = c_ref[...]
    cond = (c * jax.nn.sigmoid(c)).astype(_BF16)
    o_ref[...] = _dot(cond, w_ref[...].astype(_BF16)) + b_ref[...]


def _adaln_table(c, ada_w, ada_b):
    batch = c.shape[0]
    return pl.pallas_call(
        _adaln_kernel,
        grid=(DEPTH, 6),
        in_specs=[
            pl.BlockSpec((batch, D_MODEL), lambda l, j: (0, 0)),
            pl.BlockSpec((None, D_MODEL, D_MODEL), lambda l, j: (l, 0, j)),
            pl.BlockSpec((None, None, 1, D_MODEL), lambda l, j: (l, j, 0, 0)),
        ],
        out_specs=pl.BlockSpec((None, None, batch, D_MODEL), lambda l, j: (l, j, 0, 0)),
        out_shape=jax.ShapeDtypeStruct((DEPTH, 6, batch, D_MODEL), _F32),
        compiler_params=_params(2),
        name="adaln_table",
    )(c, ada_w, ada_b.reshape(DEPTH, 6, 1, D_MODEL))


def _mlstm_project_qkg(h, w, dst):
    wqt_ref, wk_ref, _, _, wg_ref, bcol_ref = w
    qt_s, k_s, _, _, gcol_s, grow_s = dst
    ct = MLSTM_CHUNK
    qt_s[...] = (_dot_nt(wqt_ref[...], h) * (A_DQK ** -0.5)).astype(_BF16)
    k_s[...] = _dot(h, wk_ref[...]).astype(_BF16)
    z = _dot(h, wg_ref[...]) + bcol_ref[...]
    bc = _segment_cumsum(_log_sigmoid(z), 0, ct)
    gcol_s[...] = (z - pltpu.roll(bc, LANES - A_HEADS, 1)) * LOG2E
    zr = z.T[:2 * A_HEADS, :]
    br = _segment_cumsum(_log_sigmoid(zr), 1, ct)
    sub = lax.broadcasted_iota(jnp.int32, zr.shape, 0)
    grow_s[...] = jnp.where(sub < A_HEADS, zr, br) * LOG2E


def _mlstm_project_v(h, w, dst):
    vt = _dot_nt(w[2][...], h).astype(_BF16)
    for hd in range(A_HEADS):
        dst[2][hd, 0:A_DV, :] = vt[hd * A_DV:(hd + 1) * A_DV, :]


def _mlstm_project_o(h, w, dst):
    dst[3][...] = _dot_nt(w[3][...], h)


def _mlstm_layer_kernel(xcur_ref, xprev_ref, sh_ref, sc_ref, g1prev_ref, g1cur_ref, wqt_ref, wk_ref,
                        wvt_ref, wot_ref, wg_ref, bcol_ref, gain_ref, wout_ref,
                        out_ref, qt_a, k_a, vt_a, ot_a, gcol_a, grow_a, qt_b, k_b, vt_b, ot_b,
                        gcol_b, grow_b, cn_scr, m_scr, z_scr, hold_scr, *, n_chunks):
    L = MLSTM_CHUNK
    heads = range(A_HEADS)
    t = pl.program_id(0)
    weights = (wqt_ref, wk_ref, wvt_ref, wot_ref, wg_ref, bcol_ref)
    set_a = (qt_a, k_a, vt_a, ot_a, gcol_a, grow_a)
    set_b = (qt_b, k_b, vt_b, ot_b, gcol_b, grow_b)

    @pl.when(t == 0)
    def _():
        for ref in set_b + (hold_scr, cn_scr, m_scr):
            ref[...] = jnp.zeros_like(ref)
        for vt_s in (vt_a, vt_b):
            vt_s[:, A_DV:AV_ROWS, :] = jnp.ones((A_HEADS, AV_ROWS - A_DV, L), _BF16)

    causal = (lax.broadcasted_iota(jnp.int32, (L, L), 0)
              <= lax.broadcasted_iota(jnp.int32, (L, L), 1))

    def step(dst, src, x_proj, x_res, g1, fresh, store):
        qt_s, k_s, vt_s, ot_s, gcol_s, grow_s = src
        gcol = gcol_s[...]
        grow = grow_s[...]
        kh = [k_s[:, h * A_DQK:(h + 1) * A_DQK] for h in heads]
        qt = [qt_s[h * A_DQK:(h + 1) * A_DQK, :] for h in heads]
        if fresh is None:
            cn_prev = [cn_scr[h] for h in heads]
            m_prev = [m_scr[h][:, 0:1] for h in heads]
        else:
            cn_prev = [jnp.where(fresh, 0.0, cn_scr[h]) for h in heads]
            m_prev = [jnp.where(fresh, 0.0, m_scr[h][:, 0:1]) for h in heads]
        st = [_dot(kh[h], qt[h]) for h in heads]
        inter_mm = [_dot(cn_prev[h].astype(_BF16), qt[h]) for h in heads]

        i_row = [grow[h:h + 1, :] for h in heads]
        b_row = [grow[A_HEADS + h:A_HEADS + h + 1, :] for h in heads]
        d = [jnp.where(causal, gcol[:, h:h + 1] + b_row[h], -jnp.inf) for h in heads]
        inter = [b_row[h] + m_prev[h] for h in heads]
        m_t = [jnp.maximum(inter[h], jnp.max(d[h], axis=0, keepdims=True)) for h in heads]
        a = [(jnp.exp2(d[h] - m_t[h]) * st[h]).astype(_BF16) for h in heads]
        w_inter = [jnp.exp2(inter[h] - m_t[h]) for h in heads]

        hp = _modulate(x_proj, sh_ref[...], sc_ref[...]).astype(_BF16)
        _mlstm_project_qkg(hp, weights, dst)
        _mlstm_project_v(hp, weights, dst)
        nd = [w_inter[h] * inter_mm[h] + _dot(vt_s[h], a[h]) for h in heads]
        _mlstm_project_o(hp, weights, dst)

        b_last = [b_row[h][:, L - 1:L] for h in heads]
        dl = [b_last[h] - b_row[h] + i_row[h] for h in heads]
        m_new = [jnp.maximum(b_last[h] + m_prev[h], jnp.max(dl[h], axis=1, keepdims=True))
                 for h in heads]
        for h in heads:
            rows = slice(h * A_DV, (h + 1) * A_DV)
            den = nd[h][A_DV:A_DV + 1, :]
            ht = nd[h][0:A_DV, :] / jnp.maximum(jnp.abs(den), jnp.exp2(-m_t[h]))
            ht = (ht * lax.rsqrt(jnp.mean(ht * ht, axis=0, keepdims=True) + EPS)
                  * gain_ref[rows, :])
            z_scr[rows, :] = (jax.nn.sigmoid(ot_s[rows, :]) * ht).astype(_BF16)
        store(x_res + g1 * _dot_tn(z_scr[...], wout_ref[...]))
        for h in heads:
            vw = (vt_s[h].astype(_F32) * jnp.exp2(dl[h] - m_new[h])).astype(_BF16)
            decay = jnp.exp2(b_last[h] + m_prev[h] - m_new[h])
            cn_scr[h] = decay * cn_prev[h] + _dot(vw, kh[h])
            m_scr[h] = jnp.broadcast_to(m_new[h], (1, LANES))

    def store_second_half(y):
        out_ref[L:2 * L, :] = y

    def store_held(y):
        hold_scr[...] = y

    out_ref[0:L, :] = hold_scr[...]
    step(set_a, set_b, xcur_ref[0:L, :], xprev_ref[L:2 * L, :], g1prev_ref[...], None,
         store_second_half)
    step(set_b, set_a, xcur_ref[L:2 * L, :], xcur_ref[0:L, :], g1cur_ref[...],
         lax.rem(2 * t, n_chunks) == 0, store_held)


def _mlstm_layer(x, mod, layer, w_all, wt_all, b_i, b_f, head_gain, w_out):
    batch, seq, _ = x.shape
    L = MLSTM_CHUNK
    nc = seq // L
    w = w_all[layer]
    wk = w[:, A_QK_W:2 * A_QK_W]
    wgate = w[:, 2 * A_QK_W + 2 * A_V_W:]
    wg = jnp.pad(wgate, ((0, 0), (0, LANES - 2 * A_HEADS)))
    wt_spec = lambda rows, blk: pl.BlockSpec(
        (None, rows, D_MODEL), lambda *_: (layer, blk, 0), pipeline_mode=pl.Buffered(1))
    bias = jnp.concatenate([b_i, b_f]).astype(_F32)
    bcol = jnp.pad(bias, (0, LANES - 2 * A_HEADS)).reshape(1, LANES)
    gain = jnp.broadcast_to(head_gain.reshape(A_V_W, 1).astype(_F32), (A_V_W, L))

    assert nc % 2 == 0
    ppb = nc // 2
    n_pairs = batch * ppb
    cur_pair = lambda t: jnp.minimum(t, n_pairs - 1)
    prev_pair = lambda t: jnp.maximum(t - 1, 0)
    tok = lambda pair: pl.BlockSpec(
        (None, 2 * L, D_MODEL), lambda t: (pair(t) // ppb, lax.rem(pair(t), ppb), 0))
    modrow = lambda slot, pair: pl.BlockSpec(
        (None, None, None, 1, D_MODEL), lambda t: (layer, slot, pair(t) // ppb, 0, 0))
    scratch_set = [
        pltpu.VMEM((A_QK_W, L), _BF16), pltpu.VMEM((L, A_QK_W), _BF16),
        pltpu.VMEM((A_HEADS, AV_ROWS, L), _BF16), pltpu.VMEM((A_V_W, L), _F32),
        pltpu.VMEM((L, LANES), _F32), pltpu.VMEM((2 * A_HEADS, L), _F32),
    ]
    return pl.pallas_call(
        functools.partial(_mlstm_layer_kernel, n_chunks=nc),
        grid=(n_pairs + 1,),
        in_specs=[
            tok(cur_pair), tok(prev_pair),
            modrow(0, cur_pair), modrow(1, cur_pair), modrow(2, prev_pair), modrow(2, cur_pair),
            wt_spec(A_QK_W, 0), _const_spec((D_MODEL, A_QK_W)),
            wt_spec(A_V_W, 1), wt_spec(A_V_W, 2),
            _const_spec((D_MODEL, LANES)), _const_spec((1, LANES)),
            _const_spec((A_V_W, L)), _const_spec((A_V_W, D_MODEL)),
        ],
        out_specs=tok(prev_pair),
        out_shape=jax.ShapeDtypeStruct(x.shape, _F32),
        scratch_shapes=scratch_set + scratch_set + [
            pltpu.VMEM((A_HEADS, AV_ROWS, A_DQK), _F32),
            pltpu.VMEM((A_HEADS, 1, LANES), _F32),
            pltpu.VMEM((A_V_W, L), _BF16),
            pltpu.VMEM((L, D_MODEL), _F32),
        ],
        compiler_params=_params(1),
        name="mlstm_layer",
    )(x, x, mod, mod, mod, mod, wt_all, wk, wt_all, wt_all, wg, bcol, gain,
      w_out.astype(_BF16))


def _mlp_kernel(x_ref, sh_ref, sc_ref, g_ref, w1_ref, w2_ref, fgain_ref, out_ref, u_scr, *,
                final_norm):
    x = x_ref[...]
    h = _modulate(x, sh_ref[...], sc_ref[...]).astype(_BF16)
    for c in range(D_FF // FF_CHUNK):
        u = jnp.maximum(_dot(h, w1_ref[:, c * FF_CHUNK:(c + 1) * FF_CHUNK]), 0.0)
        u_scr[:, c * FF_CHUNK:(c + 1) * FF_CHUNK] = (u * u).astype(_BF16)
    y = x + g_ref[...] * _dot(u_scr[...], w2_ref[...])
    if final_norm:
        y = _rms(y) * fgain_ref[...]
    out_ref[...] = y


def _layer_spec(layer, rows, cols):
    return pl.BlockSpec((None, rows, cols), lambda *_: (layer, 0, 0),
                        pipeline_mode=pl.Buffered(1))


def _mlp(x, mod, layer, w1, w2, final_gain, final_norm):
    batch, seq, _ = x.shape
    tm = MLP_TILE
    tok = pl.BlockSpec((None, tm, D_MODEL), lambda b, i: (b, i, 0))
    return pl.pallas_call(
        functools.partial(_mlp_kernel, final_norm=final_norm),
        grid=(batch, seq // tm),
        in_specs=[
            tok, _mod_spec(layer, 3), _mod_spec(layer, 4), _mod_spec(layer, 5),
            _layer_spec(layer, D_MODEL, D_FF), _layer_spec(layer, D_FF, D_MODEL),
            _const_spec((1, D_MODEL)),
        ],
        out_specs=tok,
        out_shape=jax.ShapeDtypeStruct(x.shape, _F32),
        scratch_shapes=[pltpu.VMEM((tm, D_FF), _BF16)],
        compiler_params=_params(2),
        name="mlp",
    )(x, mod, mod, mod, w1, w2, final_gain.reshape(1, D_MODEL).astype(_F32))


def _split_bf16(x):
    hi = x.astype(_BF16).astype(_F32)
    rest = x - hi
    mid = rest.astype(_BF16).astype(_F32)
    lo = (rest - mid).astype(_BF16).astype(_F32)
    return hi, mid, lo


N_BIAS = 3
MAX_LOG2_WEIGHT = 60.0
BOUND_SLACK = 1.0 + 2.0 ** -6


def _bias_placement():
    place = np.zeros((LANES, B_HEADS * LANES), np.float32)
    for hd in range(B_HEADS):
        spare = hd * LANES + (0 if hd % 2 else B_DH)
        for piece in range(N_BIAS):
            place[piece * B_HEADS + hd, spare + piece] = -1.0
            place[N_BIAS * B_HEADS, spare + N_BIAS + piece] = 1.0
    return jnp.asarray(place, _BF16)


def _fox_kv_kernel(x_ref, gain_ref, wk_ref, wvt_ref, wf_ref, bcol_ref,
                   place_ref, kaug_ref, vt_ref, frow_ref, kmax2_ref, ccol_scr, crow_scr):
    tm = KV_TILE

    @pl.when(pl.program_id(1) == 0)
    def _():
        ccol_scr[...] = jnp.zeros_like(ccol_scr)
        crow_scr[...] = jnp.zeros_like(crow_scr)
        kmax2_ref[...] = jnp.zeros_like(kmax2_ref)

    ones = jnp.ones((V_ROWS - B_DH, tm), _BF16)
    lane = lax.broadcasted_iota(jnp.int32, (tm, LANES), 1)
    pair_lane = lax.broadcasted_iota(jnp.int32, (1, LANES), 1)
    carry_col = ccol_scr[...]
    carry_row = crow_scr[:, 0:1]
    kmax2 = kmax2_ref[...]
    def project(blk):
        nonlocal carry_col, carry_row
        h = (_rms(x_ref[blk * tm:(blk + 1) * tm, :]) * gain_ref[...]).astype(_BF16)
        vt = _dot_nt(wvt_ref[...], h).astype(_BF16)
        for hd in range(B_HEADS):
            vt_ref[blk, hd, 0:B_DH, :] = vt[hd * B_DH:(hd + 1) * B_DH, :]
            vt_ref[blk, hd, B_DH:V_ROWS, :] = ones
        k = _dot(h, wk_ref[...])
        logf = _log_sigmoid(_dot(h, wf_ref[...]) + bcol_ref[...])
        fc = _segment_cumsum(logf, 0, tm) + carry_col
        carry_col = fc[tm - 1:tm, :]
        fr = _segment_cumsum(logf.T[:B_HEADS, :], 1, tm)
        fr = fr + carry_row
        carry_row = fr[:, tm - 1:tm]
        frow_ref[blk] = fr * LOG2E
        return k, fc

    def augment(blk, k, fc):
        nonlocal kmax2
        hi, mid, lo = _split_bf16(fc * LOG2E)
        pieces = jnp.where(lane < B_HEADS, hi, jnp.where(
            lane < 2 * B_HEADS, pltpu.roll(mid, B_HEADS, 1), jnp.where(
                lane < 3 * B_HEADS, pltpu.roll(lo, 2 * B_HEADS, 1), jnp.where(
                    lane == 3 * B_HEADS, 1.0, 0.0)))).astype(_BF16)
        bias = _dot(pieces, place_ref[...])
        for hd in range(B_HEADS):
            pair, odd = divmod(hd, 2)
            own = (lane >= B_DH) if odd else (lane < B_DH)
            kaug_ref[blk, hd] = jnp.where(
                own, k[:, pair * LANES:(pair + 1) * LANES],
                bias[:, hd * LANES:(hd + 1) * LANES]).astype(_BF16)
        for pair in range(B_HEADS // 2):
            kb = k[:, pair * LANES:(pair + 1) * LANES]
            n2 = jnp.max(jnp.sum(kb * kb, axis=1, keepdims=True), axis=0, keepdims=True)
            kmax2 = jnp.where(pair_lane == pair, jnp.maximum(kmax2, n2), kmax2)

    projected = project(0)
    for blk in range(KV_STEP):
        pending = projected
        if blk + 1 < KV_STEP:
            projected = project(blk + 1)
        augment(blk, *pending)
    ccol_scr[...] = carry_col
    crow_scr[...] = jnp.broadcast_to(carry_row, crow_scr.shape)
    kmax2_ref[...] = kmax2


def _fox_kv(x, kv_gain, w_kv, fg_bias):
    batch, seq, _ = x.shape
    tm = KV_TILE
    w = w_kv.astype(_BF16)
    wk, wvt, wfg = w[:, :D_MODEL], w[:, D_MODEL:2 * D_MODEL].T, w[:, 2 * D_MODEL:]
    wf = jnp.pad(wfg, ((0, 0), (0, LANES - B_HEADS)))
    bcol = jnp.pad(fg_bias.astype(_F32), (0, LANES - B_HEADS)).reshape(1, LANES)
    ks = KV_STEP
    return pl.pallas_call(
        _fox_kv_kernel,
        grid=(batch, seq // (ks * tm)),
        in_specs=[
            pl.BlockSpec((None, ks * tm, D_MODEL), lambda b, i: (b, i, 0)),
            _const_spec((1, D_MODEL)),
            _const_spec((D_MODEL, D_MODEL)), _const_spec((D_MODEL, D_MODEL)),
            _const_spec((D_MODEL, LANES)), _const_spec((1, LANES)),
            _const_spec((LANES, B_HEADS * LANES)),
        ],
        out_specs=[
            pl.BlockSpec((None, ks, B_HEADS, tm, LANES), lambda b, i: (b, i, 0, 0, 0)),
            pl.BlockSpec((None, ks, B_HEADS, V_ROWS, tm), lambda b, i: (b, i, 0, 0, 0)),
            pl.BlockSpec((None, ks, B_HEADS, tm), lambda b, i: (b, i, 0, 0)),
            pl.BlockSpec((None, 1, LANES), lambda b, i: (b, 0, 0)),
        ],
        out_shape=[
            jax.ShapeDtypeStruct((batch, seq // tm, B_HEADS, tm, LANES), _BF16),
            jax.ShapeDtypeStruct((batch, seq // tm, B_HEADS, V_ROWS, tm), _BF16),
            jax.ShapeDtypeStruct((batch, seq // tm, B_HEADS, tm), _F32),
            jax.ShapeDtypeStruct((batch, 1, LANES), _F32),
        ],
        scratch_shapes=[pltpu.VMEM((1, LANES), _F32), pltpu.VMEM((B_HEADS, LANES), _F32)],
        compiler_params=_params(2),
        name="fox_kv",
    )(x, kv_gain.reshape(1, D_MODEL).astype(_F32), wk, wvt, wf, bcol, _bias_placement())


def _fox_attn_kernel(x_ref, sh_ref, sc_ref, g1_ref, wqt_ref, kaug_ref, vt_ref, frow_ref,
                     kmax2_ref, wout_ref, out_ref, qaug_scr, m_scr, acc_scr, o_scr):
    assert Q_STEP % 2 == 0
    for s in range(Q_STEP):
        rows = pl.ds(s * Q_TILE, Q_TILE)
        _fox_attn_block(pl.program_id(1) * Q_STEP + s, s % 2, x_ref.at[rows], sh_ref, sc_ref,
                        g1_ref, wqt_ref, kaug_ref, vt_ref, frow_ref, kmax2_ref, wout_ref,
                        out_ref.at[rows], qaug_scr, m_scr, acc_scr, o_scr)


def _fox_attn_block(qi, qi_parity, x_ref, sh_ref, sc_ref, g1_ref, wqt_ref, kaug_ref, vt_ref,
                    frow_ref, kmax2_ref, wout_ref, out_ref, qaug_scr, m_scr, acc_scr, o_scr):
    tq, tk = Q_TILE, KV_TILE
    n_groups = B_HEADS // HEAD_GROUP
    h = _modulate(x_ref[...], sh_ref[...], sc_ref[...]).astype(_BF16)
    ft = frow_ref[qi]
    sub = lax.broadcasted_iota(jnp.int32, (B_DH, tq), 0)
    slab = Q_PROJ_HEADS * B_DH
    qt = [(_dot_nt(wqt_ref[r * slab:(r + 1) * slab, :], h)
           * (B_DH ** -0.5 * LOG2E)).astype(_BF16)
          for r in range(B_HEADS // Q_PROJ_HEADS)]
    kmax2 = kmax2_ref[...]
    score_bound = []
    for hd in range(B_HEADS):
        hi, mid, lo = _split_bf16(ft[hd:hd + 1, :])
        bias = jnp.where(sub < N_BIAS, 1.0, jnp.where(sub == N_BIAS, hi, jnp.where(
            sub == N_BIAS + 1, mid, jnp.where(sub == N_BIAS + 2, lo, 0.0)))).astype(_BF16)
        r, off = divmod(hd, Q_PROJ_HEADS)
        qh = qt[r][off * B_DH:(off + 1) * B_DH, :]
        lo_half, hi_half = (bias, qh) if hd % 2 else (qh, bias)
        qaug_scr[hd, 0:B_DH, :] = lo_half
        qaug_scr[hd, B_DH:2 * B_DH, :] = hi_half
        qf = qh.astype(_F32)
        qn2 = jnp.sum(qf * qf, axis=0, keepdims=True)
        score_bound.append(jnp.sqrt(qn2 * kmax2[:, hd // 2:hd // 2 + 1]) * BOUND_SLACK + 1.0)
    m_scr[...] = jnp.full(m_scr.shape, -jnp.inf, _F32)
    acc_scr[...] = jnp.zeros_like(acc_scr)

    def kv_blocks(blocks, mode):
        if mode == "diagonal":
            keep = (lax.broadcasted_iota(jnp.int32, (tk, tq), 1)
                    >= lax.broadcasted_iota(jnp.int32, (tk, tq), 0))
        def scores(unit):
            j, grp = unit
            return [_dot(kaug_ref[j, hd], qaug_scr[hd])
                    for hd in range(grp * HEAD_GROUP, (grp + 1) * HEAD_GROUP)]

        units = [(j, grp) for j in blocks for grp in range(n_groups)]
        s_next = scores(units[0])
        for n, (j, grp) in enumerate(units):
            heads = range(grp * HEAD_GROUP, (grp + 1) * HEAD_GROUP)
            s_cur = s_next
            if n + 1 < len(units):
                s_next = scores(units[n + 1])
            if mode == "fixed":
                for i, hd in enumerate(heads):
                    p_i = jnp.exp2(s_cur[i] - m_scr[hd]).astype(_BF16)
                    acc_scr[hd] = acc_scr[hd] + _dot(vt_ref[j, hd], p_i)
                continue
            if mode == "diagonal":
                s_cur = [jnp.where(keep, s, -jnp.inf) for s in s_cur]
            m_old = [m_scr[hd] for hd in heads]
            m_new = [jnp.maximum(mo, jnp.max(s, axis=0, keepdims=True))
                     for mo, s in zip(m_old, s_cur)]
            alpha = [jnp.exp2(mo - mn) for mo, mn in zip(m_old, m_new)]
            p = [jnp.exp2(s - mn) for s, mn in zip(s_cur, m_new)]
            for i, hd in enumerate(heads):
                m_scr[hd] = m_new[i]
                acc_scr[hd] = alpha[i] * acc_scr[hd] + _dot(vt_ref[j, hd], p[i].astype(_BF16))

    kv_blocks([qi], "diagonal")
    margin = m_scr[0] + MAX_LOG2_WEIGHT - score_bound[0]
    for hd in range(1, B_HEADS):
        margin = jnp.minimum(margin, m_scr[hd] + MAX_LOG2_WEIGHT - score_bound[hd])
    bounded = jnp.min(margin) >= 0.0

    @pl.when(bounded)
    def _():
        def pair(t, carry):
            kv_blocks([2 * t, 2 * t + 1], "fixed")
            return carry
        lax.fori_loop(0, qi // 2, pair, 0)
        if qi_parity:
            kv_blocks([qi - 1], "fixed")

    @pl.when(jnp.logical_not(bounded))
    def _():
        def single(j, carry):
            kv_blocks([j], "rescale")
            return carry
        lax.fori_loop(0, qi, single, 0)

    for hd in range(B_HEADS):
        o_scr[hd * B_DH:(hd + 1) * B_DH, :] = (
            acc_scr[hd, 0:B_DH, :] / acc_scr[hd, B_DH:B_DH + 1, :]).astype(_BF16)
    out_ref[...] = x_ref[...] + g1_ref[...] * _dot_tn(o_scr[...], wout_ref[...])


def _fox_attn(x, mod, layer, w_q, w_out, kaug, vt, frow, kmax2):
    batch, seq, _ = x.shape
    tq, tk = Q_TILE, KV_TILE
    nkv = seq // tk
    tok = pl.BlockSpec((None, Q_STEP * tq, D_MODEL), lambda b, i: (b, i, 0))
    return pl.pallas_call(
        _fox_attn_kernel,
        grid=(batch, seq // (Q_STEP * tq)),
        in_specs=[
            tok, _mod_spec(layer, 0), _mod_spec(layer, 1), _mod_spec(layer, 2),
            _const_spec((D_MODEL, D_MODEL)),
            pl.BlockSpec((None, nkv, B_HEADS, tk, LANES), lambda b, i: (b, 0, 0, 0, 0)),
            pl.BlockSpec((None, nkv, B_HEADS, V_ROWS, tk), lambda b, i: (b, 0, 0, 0, 0)),
            pl.BlockSpec((None, nkv, B_HEADS, tk), lambda b, i: (b, 0, 0, 0)),
            pl.BlockSpec((None, 1, LANES), lambda b, i: (b, 0, 0)),
            _const_spec((D_MODEL, D_MODEL)),
        ],
        out_specs=tok,
        out_shape=jax.ShapeDtypeStruct(x.shape, _F32),
        scratch_shapes=[
            pltpu.VMEM((B_HEADS, 2 * B_DH, tq), _BF16),
            pltpu.VMEM((B_HEADS, 1, tq), _F32),
            pltpu.VMEM((B_HEADS, V_ROWS, tq), _F32), pltpu.VMEM((D_MODEL, tq), _BF16),
        ],
        compiler_params=_params(2),
        name="fox_attn",
    )(x, mod, mod, mod, w_q.T.astype(_BF16), kaug, vt, frow, kmax2, w_out.astype(_BF16))


def kernel(x, c, ada_w, ada_b, a_w_in, a_b_i, a_b_f, a_head_gain, a_w_out, kv_gain, b_w_kv,
           b_fg_bias, b_w_q, b_w_out, mlp_w1, mlp_w2, final_gain):
    batch, seq, d = x.shape
    assert d == D_MODEL and Q_TILE == KV_TILE
    assert all(seq % t == 0
               for t in (MLP_TILE, MLSTM_CHUNK, Q_STEP * Q_TILE, KV_STEP * KV_TILE))
    mod = _adaln_table(c, ada_w, ada_b).reshape(DEPTH, 6, batch, 1, D_MODEL)
    w1_all, w2_all = mlp_w1.astype(_BF16), mlp_w2.astype(_BF16)
    a_w_all = a_w_in.astype(_BF16)
    a_wt_all = jnp.swapaxes(a_w_all, 1, 2)
    shared = None
    for l in range(DEPTH):
        if l < N_A_LAYERS:
            x = _mlstm_layer(x, mod, l, a_w_all, a_wt_all, a_b_i[l], a_b_f[l], a_head_gain[l],
                             a_w_out[l])
        else:
            if shared is None:
                shared = _fox_kv(x, kv_gain, b_w_kv, b_fg_bias)
            j = l - N_A_LAYERS
            x = _fox_attn(x, mod, l, b_w_q[j], b_w_out[j], *shared)
        x = _mlp(x, mod, l, w1_all, w2_all, final_gain, final_norm=(l == DEPTH - 1))
    return x
```

```python
import functools

import jax
import jax.numpy as jnp
import numpy as np
from jax import lax
from jax.experimental import pallas as pl
from jax.experimental.pallas import tpu as pltpu

D_MODEL = 1024
DEPTH = 4
N_A_LAYERS = DEPTH // 2
A_HEADS = 4
A_DV = D_MODEL // A_HEADS
A_DQK = A_DV // 2
A_QK_W = A_HEADS * A_DQK
A_V_W = A_HEADS * A_DV
B_HEADS = 16
B_DH = D_MODEL // B_HEADS
D_FF = 4 * D_MODEL
EPS = 1e-6
LOG2E = 1.4426950408889634
LANES = 128
BF16_ROWS = 16
AV_ROWS = A_DV + BF16_ROWS
V_ROWS = B_DH + BF16_ROWS

ADALN_SLOTS = 2
MLSTM_CHUNK = 256
KV_TILE = 256
KV_STEP = 4
Q_TILE = 256
Q_STEP = 4
HEAD_GROUP = 4
Q_PROJ_HEADS = 4
MLP_TILE = 1024
FF_CHUNK = 512
VMEM_LIMIT = 56 * 1024 * 1024

_BF16 = jnp.bfloat16
_F32 = jnp.float32


def _dot(a, b):
    return jnp.dot(a, b, preferred_element_type=_F32)


def _dot_nt(a, b):
    return lax.dot_general(a, b, (((1,), (1,)), ((), ())), preferred_element_type=_F32)


def _dot_tn(a, b):
    return lax.dot_general(a, b, (((0,), (0,)), ((), ())), preferred_element_type=_F32)


def _rms(x):
    return x * lax.rsqrt(jnp.mean(x * x, axis=-1, keepdims=True) + EPS)


def _modulate(x, shift, scale):
    return _rms(x) * (1.0 + scale) + shift


def _log_sigmoid(z):
    return jnp.minimum(z, 0.0) - jnp.log1p(jnp.exp(-jnp.abs(z)))


def _segment_cumsum(x, axis, seg):
    pos = lax.broadcasted_iota(jnp.int32, x.shape, axis) & (seg - 1)
    k = 1
    while k < seg:
        x = x + jnp.where(pos >= k, pltpu.roll(x, k, axis), 0.0)
        k *= 2
    return x


def _params(n_grid):
    return pltpu.CompilerParams(dimension_semantics=("arbitrary",) * n_grid,
                                vmem_limit_bytes=VMEM_LIMIT)


def _const_spec(shape):
    return pl.BlockSpec(shape, lambda *_: (0,) * len(shape), pipeline_mode=pl.Buffered(1))


def _mod_spec(layer, slot):
    return pl.BlockSpec((None, None, None, 1, D_MODEL), lambda b, i: (layer, slot, b, 0, 0))


def _adaln_kernel(c_ref, w_ref, b_ref, o_ref):
    c = c_ref[...]
    cond = (c * jax.nn.sigmoid(c)).astype(_BF16)
    for s in range(ADALN_SLOTS):
        cols = slice(s * D_MODEL, (s + 1) * D_MODEL)
        o_ref[s] = _dot(cond, w_ref[:, cols].astype(_BF16)) + b_ref[s]


def _adaln_table(c, ada_w, ada_b):
    batch = c.shape[0]
    ns = ADALN_SLOTS
    return pl.pallas_call(
        _adaln_kernel,
        grid=(DEPTH, 6 // ns),
        in_specs=[
            pl.BlockSpec((batch, D_MODEL), lambda l, j: (0, 0)),
            pl.BlockSpec((None, D_MODEL, ns * D_MODEL), lambda l, j: (l, 0, j)),
            pl.BlockSpec((None, ns, 1, D_MODEL), lambda l, j: (l, j, 0, 0)),
        ],
        out_specs=pl.BlockSpec((None, ns, batch, D_MODEL), lambda l, j: (l, j, 0, 0)),
        out_shape=jax.ShapeDtypeStruct((DEPTH, 6, batch, D_MODEL), _F32),
        compiler_params=_params(2),
        name="adaln_table",
    )(c, ada_w, ada_b.reshape(DEPTH, 6, 1, D_MODEL))


def _mlstm_project_qkg(h, w, dst):
    wqt_ref, wk_ref, _, _, wg_ref, bcol_ref = w
    qt_s, k_s, _, _, gcol_s, grow_s = dst
    ct = MLSTM_CHUNK
    qt_s[...] = (_dot_nt(wqt_ref[...], h) * (A_DQK ** -0.5)).astype(_BF16)
    k_s[...] = _dot(h, wk_ref[...]).astype(_BF16)
    z = _dot(h, wg_ref[...]) + bcol_ref[...]
    bc = _segment_cumsum(_log_sigmoid(z), 0, ct)
    gcol_s[...] = (z - pltpu.roll(bc, LANES - A_HEADS, 1)) * LOG2E
    zr = z.T[:2 * A_HEADS, :]
    br = _segment_cumsum(_log_sigmoid(zr), 1, ct)
    sub = lax.broadcasted_iota(jnp.int32, zr.shape, 0)
    grow_s[...] = jnp.where(sub < A_HEADS, zr, br) * LOG2E


def _mlstm_project_v(h, w, dst):
    vt = _dot_nt(w[2][...], h).astype(_BF16)
    for hd in range(A_HEADS):
        dst[2][hd, 0:A_DV, :] = vt[hd * A_DV:(hd + 1) * A_DV, :]


def _mlstm_project_o(h, w, dst):
    dst[3][...] = _dot_nt(w[3][...], h)


def _mlstm_layer_kernel(xcur_ref, xprev_ref, sh_ref, sc_ref, g1prev_ref, g1cur_ref, wqt_ref, wk_ref,
                        wvt_ref, wot_ref, wg_ref, bcol_ref, gain_ref, wout_ref,
                        out_ref, qt_a, k_a, vt_a, ot_a, gcol_a, grow_a, qt_b, k_b, vt_b, ot_b,
                        gcol_b, grow_b, cn_scr, m_scr, z_scr, hold_scr, *, n_chunks):
    L = MLSTM_CHUNK
    heads = range(A_HEADS)
    t = pl.program_id(0)
    weights = (wqt_ref, wk_ref, wvt_ref, wot_ref, wg_ref, bcol_ref)
    set_a = (qt_a, k_a, vt_a, ot_a, gcol_a, grow_a)
    set_b = (qt_b, k_b, vt_b, ot_b, gcol_b, grow_b)

    @pl.when(t == 0)
    def _():
        for ref in set_b + (hold_scr, cn_scr, m_scr):
            ref[...] = jnp.zeros_like(ref)
        for vt_s in (vt_a, vt_b):
            vt_s[:, A_DV:AV_ROWS, :] = jnp.ones((A_HEADS, AV_ROWS - A_DV, L), _BF16)

    causal = (lax.broadcasted_iota(jnp.int32, (L, L), 0)
              <= lax.broadcasted_iota(jnp.int32, (L, L), 1))

    def step(dst, src, x_proj, x_res, g1, fresh, store):
        qt_s, k_s, vt_s, ot_s, gcol_s, grow_s = src
        gcol = gcol_s[...]
        grow = grow_s[...]
        kh = [k_s[:, h * A_DQK:(h + 1) * A_DQK] for h in heads]
        qt = [qt_s[h * A_DQK:(h + 1) * A_DQK, :] for h in heads]
        if fresh is None:
            cn_prev = [cn_scr[h] for h in heads]
            m_prev = [m_scr[h][:, 0:1] for h in heads]
        else:
            cn_prev = [jnp.where(fresh, 0.0, cn_scr[h]) for h in heads]
            m_prev = [jnp.where(fresh, 0.0, m_scr[h][:, 0:1]) for h in heads]
        st = [_dot(kh[h], qt[h]) for h in heads]
        inter_mm = [_dot(cn_prev[h].astype(_BF16), qt[h]) for h in heads]

        i_row = [grow[h:h + 1, :] for h in heads]
        b_row = [grow[A_HEADS + h:A_HEADS + h + 1, :] for h in heads]
        d = [jnp.where(causal, gcol[:, h:h + 1] + b_row[h], -jnp.inf) for h in heads]
        inter = [b_row[h] + m_prev[h] for h in heads]
        m_t = [jnp.maximum(inter[h], jnp.max(d[h], axis=0, keepdims=True)) for h in heads]
        a = [(jnp.exp2(d[h] - m_t[h]) * st[h]).astype(_BF16) for h in heads]
        w_inter = [jnp.exp2(inter[h] - m_t[h]) for h in heads]

        hp = _modulate(x_proj, sh_ref[...], sc_ref[...]).astype(_BF16)
        _mlstm_project_qkg(hp, weights, dst)
        _mlstm_project_v(hp, weights, dst)
        nd = [w_inter[h] * inter_mm[h] + _dot(vt_s[h], a[h]) for h in heads]
        _mlstm_project_o(hp, weights, dst)

        b_last = [b_row[h][:, L - 1:L] for h in heads]
        dl = [b_last[h] - b_row[h] + i_row[h] for h in heads]
        m_new = [jnp.maximum(b_last[h] + m_prev[h], jnp.max(dl[h], axis=1, keepdims=True))
                 for h in heads]
        for h in heads:
            rows = slice(h * A_DV, (h + 1) * A_DV)
            den = nd[h][A_DV:A_DV + 1, :]
            ht = nd[h][0:A_DV, :] / jnp.maximum(jnp.abs(den), jnp.exp2(-m_t[h]))
            ht = (ht * lax.rsqrt(jnp.mean(ht * ht, axis=0, keepdims=True) + EPS)
                  * gain_ref[rows, :])
            z_scr[rows, :] = (jax.nn.sigmoid(ot_s[rows, :]) * ht).astype(_BF16)
        store(x_res + g1 * _dot_tn(z_scr[...], wout_ref[...]))
        for h in heads:
            vw = (vt_s[h].astype(_F32) * jnp.exp2(dl[h] - m_new[h])).astype(_BF16)
            decay = jnp.exp2(b_last[h] + m_prev[h] - m_new[h])
            cn_scr[h] = decay * cn_prev[h] + _dot(vw, kh[h])
            m_scr[h] = jnp.broadcast_to(m_new[h], (1, LANES))

    def store_second_half(y):
        out_ref[L:2 * L, :] = y

    def store_held(y):
        hold_scr[...] = y

    out_ref[0:L, :] = hold_scr[...]
    step(set_a, set_b, xcur_ref[0:L, :], xprev_ref[L:2 * L, :], g1prev_ref[...], None,
         store_second_half)
    step(set_b, set_a, xcur_ref[L:2 * L, :], xcur_ref[0:L, :], g1cur_ref[...],
         lax.rem(2 * t, n_chunks) == 0, store_held)


def _mlstm_layer(x, mod, layer, w_all, wt_all, b_i, b_f, head_gain, w_out):
    batch, seq, _ = x.shape
    L = MLSTM_CHUNK
    nc = seq // L
    w = w_all[layer]
    wk = w[:, A_QK_W:2 * A_QK_W]
    wgate = w[:, 2 * A_QK_W + 2 * A_V_W:]
    wg = jnp.pad(wgate, ((0, 0), (0, LANES - 2 * A_HEADS)))
    wt_spec = lambda rows, blk: pl.BlockSpec(
        (None, rows, D_MODEL), lambda *_: (layer, blk, 0), pipeline_mode=pl.Buffered(1))
    bias = jnp.concatenate([b_i, b_f]).astype(_F32)
    bcol = jnp.pad(bias, (0, LANES - 2 * A_HEADS)).reshape(1, LANES)
    gain = jnp.broadcast_to(head_gain.reshape(A_V_W, 1).astype(_F32), (A_V_W, L))

    assert nc % 2 == 0
    ppb = nc // 2
    n_pairs = batch * ppb
    cur_pair = lambda t: jnp.minimum(t, n_pairs - 1)
    prev_pair = lambda t: jnp.maximum(t - 1, 0)
    tok = lambda pair: pl.BlockSpec(
        (None, 2 * L, D_MODEL), lambda t: (pair(t) // ppb, lax.rem(pair(t), ppb), 0))
    modrow = lambda slot, pair: pl.BlockSpec(
        (None, None, None, 1, D_MODEL), lambda t: (layer, slot, pair(t) // ppb, 0, 0))
    scratch_set = [
        pltpu.VMEM((A_QK_W, L), _BF16), pltpu.VMEM((L, A_QK_W), _BF16),
        pltpu.VMEM((A_HEADS, AV_ROWS, L), _BF16), pltpu.VMEM((A_V_W, L), _F32),
        pltpu.VMEM((L, LANES), _F32), pltpu.VMEM((2 * A_HEADS, L), _F32),
    ]
    return pl.pallas_call(
        functools.partial(_mlstm_layer_kernel, n_chunks=nc),
        grid=(n_pairs + 1,),
        in_specs=[
            tok(cur_pair), tok(prev_pair),
            modrow(0, cur_pair), modrow(1, cur_pair), modrow(2, prev_pair), modrow(2, cur_pair),
            wt_spec(A_QK_W, 0), _const_spec((D_MODEL, A_QK_W)),
            wt_spec(A_V_W, 1), wt_spec(A_V_W, 2),
            _const_spec((D_MODEL, LANES)), _const_spec((1, LANES)),
            _const_spec((A_V_W, L)), _const_spec((A_V_W, D_MODEL)),
        ],
        out_specs=tok(prev_pair),
        out_shape=jax.ShapeDtypeStruct(x.shape, _F32),
        scratch_shapes=scratch_set + scratch_set + [
            pltpu.VMEM((A_HEADS, AV_ROWS, A_DQK), _F32),
            pltpu.VMEM((A_HEADS, 1, LANES), _F32),
            pltpu.VMEM((A_V_W, L), _BF16),
            pltpu.VMEM((L, D_MODEL), _F32),
        ],
        compiler_params=_params(1),
        name="mlstm_layer",
    )(x, x, mod, mod, mod, mod, wt_all, wk, wt_all, wt_all, wg, bcol, gain,
      w_out.astype(_BF16))


def _mlp_kernel(x_ref, sh_ref, sc_ref, g_ref, w1_ref, w2_ref, fgain_ref, out_ref, u_scr, *,
                final_norm):
    x = x_ref[...]
    h = _modulate(x, sh_ref[...], sc_ref[...]).astype(_BF16)
    for c in range(D_FF // FF_CHUNK):
        u = jnp.maximum(_dot(h, w1_ref[:, c * FF_CHUNK:(c + 1) * FF_CHUNK]), 0.0)
        u_scr[:, c * FF_CHUNK:(c + 1) * FF_CHUNK] = (u * u).astype(_BF16)
    y = x + g_ref[...] * _dot(u_scr[...], w2_ref[...])
    if final_norm:
        y = _rms(y) * fgain_ref[...]
    out_ref[...] = y


def _layer_spec(layer, rows, cols):
    return pl.BlockSpec((None, rows, cols), lambda *_: (layer, 0, 0),
                        pipeline_mode=pl.Buffered(1))


def _mlp(x, mod, layer, w1, w2, final_gain, final_norm):
    batch, seq, _ = x.shape
    tm = MLP_TILE
    tok = pl.BlockSpec((None, tm, D_MODEL), lambda b, i: (b, i, 0))
    return pl.pallas_call(
        functools.partial(_mlp_kernel, final_norm=final_norm),
        grid=(batch, seq // tm),
        in_specs=[
            tok, _mod_spec(layer, 3), _mod_spec(layer, 4), _mod_spec(layer, 5),
            _layer_spec(layer, D_MODEL, D_FF), _layer_spec(layer, D_FF, D_MODEL),
            _const_spec((1, D_MODEL)),
        ],
        out_specs=tok,
        out_shape=jax.ShapeDtypeStruct(x.shape, _F32),
        scratch_shapes=[pltpu.VMEM((tm, D_FF), _BF16)],
        compiler_params=_params(2),
        name="mlp",
    )(x, mod, mod, mod, w1, w2, final_gain.reshape(1, D_MODEL).astype(_F32))


def _split_bf16(x):
    hi = x.astype(_BF16).astype(_F32)
    rest = x - hi
    mid = rest.astype(_BF16).astype(_F32)
    lo = (rest - mid).astype(_BF16).astype(_F32)
    return hi, mid, lo


N_BIAS = 3
MAX_LOG2_WEIGHT = 60.0
BOUND_SLACK = 1.0 + 2.0 ** -6


def _bias_placement():
    place = np.zeros((LANES, B_HEADS * LANES), np.float32)
    for hd in range(B_HEADS):
        spare = hd * LANES + (0 if hd % 2 else B_DH)
        for piece in range(N_BIAS):
            place[piece * B_HEADS + hd, spare + piece] = -1.0
            place[N_BIAS * B_HEADS, spare + N_BIAS + piece] = 1.0
    return jnp.asarray(place, _BF16)


def _fox_kv_kernel(x_ref, gain_ref, wk_ref, wvt_ref, wf_ref, bcol_ref,
                   place_ref, kaug_ref, vt_ref, frow_ref, kmax2_ref, ccol_scr, crow_scr):
    tm = KV_TILE

    @pl.when(pl.program_id(1) == 0)
    def _():
        ccol_scr[...] = jnp.zeros_like(ccol_scr)
        crow_scr[...] = jnp.zeros_like(crow_scr)
        kmax2_ref[...] = jnp.zeros_like(kmax2_ref)

    ones = jnp.ones((V_ROWS - B_DH, tm), _BF16)
    lane = lax.broadcasted_iota(jnp.int32, (tm, LANES), 1)
    pair_lane = lax.broadcasted_iota(jnp.int32, (1, LANES), 1)
    carry_col = ccol_scr[...]
    carry_row = crow_scr[:, 0:1]
    kmax2 = kmax2_ref[...]
    def project(blk):
        nonlocal carry_col, carry_row
        h = (_rms(x_ref[blk * tm:(blk + 1) * tm, :]) * gain_ref[...]).astype(_BF16)
        vt = _dot_nt(wvt_ref[...], h).astype(_BF16)
        for hd in range(B_HEADS):
            vt_ref[blk, hd, 0:B_DH, :] = vt[hd * B_DH:(hd + 1) * B_DH, :]
            vt_ref[blk, hd, B_DH:V_ROWS, :] = ones
        k = _dot(h, wk_ref[...])
        logf = _log_sigmoid(_dot(h, wf_ref[...]) + bcol_ref[...])
        fc = _segment_cumsum(logf, 0, tm) + carry_col
        carry_col = fc[tm - 1:tm, :]
        fr = _segment_cumsum(logf.T[:B_HEADS, :], 1, tm)
        fr = fr + carry_row
        carry_row = fr[:, tm - 1:tm]
        frow_ref[blk] = fr * LOG2E
        return k, fc

    def augment(blk, k, fc):
        nonlocal kmax2
        hi, mid, lo = _split_bf16(fc * LOG2E)
        pieces = jnp.where(lane < B_HEADS, hi, jnp.where(
            lane < 2 * B_HEADS, pltpu.roll(mid, B_HEADS, 1), jnp.where(
                lane < 3 * B_HEADS, pltpu.roll(lo, 2 * B_HEADS, 1), jnp.where(
                    lane == 3 * B_HEADS, 1.0, 0.0)))).astype(_BF16)
        bias = _dot(pieces, place_ref[...])
        for hd in range(B_HEADS):
            pair, odd = divmod(hd, 2)
            own = (lane >= B_DH) if odd else (lane < B_DH)
            kaug_ref[blk, hd] = jnp.where(
                own, k[:, pair * LANES:(pair + 1) * LANES],
                bias[:, hd * LANES:(hd + 1) * LANES]).astype(_BF16)
        for pair in range(B_HEADS // 2):
            kb = k[:, pair * LANES:(pair + 1) * LANES]
            n2 = jnp.max(jnp.sum(kb * kb, axis=1, keepdims=True), axis=0, keepdims=True)
            kmax2 = jnp.where(pair_lane == pair, jnp.maximum(kmax2, n2), kmax2)

    projected = project(0)
    for blk in range(KV_STEP):
        pending = projected
        if blk + 1 < KV_STEP:
            projected = project(blk + 1)
        augment(blk, *pending)
    ccol_scr[...] = carry_col
    crow_scr[...] = jnp.broadcast_to(carry_row, crow_scr.shape)
    kmax2_ref[...] = kmax2


def _fox_kv(x, kv_gain, w_kv, fg_bias):
    batch, seq, _ = x.shape
    tm = KV_TILE
    w = w_kv.astype(_BF16)
    wk, wvt, wfg = w[:, :D_MODEL], w[:, D_MODEL:2 * D_MODEL].T, w[:, 2 * D_MODEL:]
    wf = jnp.pad(wfg, ((0, 0), (0, LANES - B_HEADS)))
    bcol = jnp.pad(fg_bias.astype(_F32), (0, LANES - B_HEADS)).reshape(1, LANES)
    ks = KV_STEP
    return pl.pallas_call(
        _fox_kv_kernel,
        grid=(batch, seq // (ks * tm)),
        in_specs=[
            pl.BlockSpec((None, ks * tm, D_MODEL), lambda b, i: (b, i, 0)),
            _const_spec((1, D_MODEL)),
            _const_spec((D_MODEL, D_MODEL)), _const_spec((D_MODEL, D_MODEL)),
            _const_spec((D_MODEL, LANES)), _const_spec((1, LANES)),
            _const_spec((LANES, B_HEADS * LANES)),
        ],
        out_specs=[
            pl.BlockSpec((None, ks, B_HEADS, tm, LANES), lambda b, i: (b, i, 0, 0, 0)),
            pl.BlockSpec((None, ks, B_HEADS, V_ROWS, tm), lambda b, i: (b, i, 0, 0, 0)),
            pl.BlockSpec((None, ks, B_HEADS, tm), lambda b, i: (b, i, 0, 0)),
            pl.BlockSpec((None, 1, LANES), lambda b, i: (b, 0, 0)),
        ],
        out_shape=[
            jax.ShapeDtypeStruct((batch, seq // tm, B_HEADS, tm, LANES), _BF16),
            jax.ShapeDtypeStruct((batch, seq // tm, B_HEADS, V_ROWS, tm), _BF16),
            jax.ShapeDtypeStruct((batch, seq // tm, B_HEADS, tm), _F32),
            jax.ShapeDtypeStruct((batch, 1, LANES), _F32),
        ],
        scratch_shapes=[pltpu.VMEM((1, LANES), _F32), pltpu.VMEM((B_HEADS, LANES), _F32)],
        compiler_params=_params(2),
        name="fox_kv",
    )(x, kv_gain.reshape(1, D_MODEL).astype(_F32), wk, wvt, wf, bcol, _bias_placement())


def _fox_attn_kernel(x_ref, sh_ref, sc_ref, g1_ref, wqt_ref, kaug_ref, vt_ref, frow_ref,
                     kmax2_ref, wout_ref, out_ref, qaug_scr, m_scr, acc_scr, o_scr):
    assert Q_STEP % 2 == 0
    for s in range(Q_STEP):
        rows = pl.ds(s * Q_TILE, Q_TILE)
        _fox_attn_block(pl.program_id(1) * Q_STEP + s, s % 2, x_ref.at[rows], sh_ref, sc_ref,
                        g1_ref, wqt_ref, kaug_ref, vt_ref, frow_ref, kmax2_ref, wout_ref,
                        out_ref.at[rows], qaug_scr, m_scr, acc_scr, o_scr)


def _fox_attn_block(qi, qi_parity, x_ref, sh_ref, sc_ref, g1_ref, wqt_ref, kaug_ref, vt_ref,
                    frow_ref, kmax2_ref, wout_ref, out_ref, qaug_scr, m_scr, acc_scr, o_scr):
    tq, tk = Q_TILE, KV_TILE
    n_groups = B_HEADS // HEAD_GROUP
    h = _modulate(x_ref[...], sh_ref[...], sc_ref[...]).astype(_BF16)
    ft = frow_ref[qi]
    sub = lax.broadcasted_iota(jnp.int32, (B_DH, tq), 0)
    slab = Q_PROJ_HEADS * B_DH
    qt = [(_dot_nt(wqt_ref[r * slab:(r + 1) * slab, :], h)
           * (B_DH ** -0.5 * LOG2E)).astype(_BF16)
          for r in range(B_HEADS // Q_PROJ_HEADS)]
    kmax2 = kmax2_ref[...]
    score_bound = []
    for hd in range(B_HEADS):
        hi, mid, lo = _split_bf16(ft[hd:hd + 1, :])
        bias = jnp.where(sub < N_BIAS, 1.0, jnp.where(sub == N_BIAS, hi, jnp.where(
            sub == N_BIAS + 1, mid, jnp.where(sub == N_BIAS + 2, lo, 0.0)))).astype(_BF16)
        r, off = divmod(hd, Q_PROJ_HEADS)
        qh = qt[r][off * B_DH:(off + 1) * B_DH, :]
        lo_half, hi_half = (bias, qh) if hd % 2 else (qh, bias)
        qaug_scr[hd, 0:B_DH, :] = lo_half
        qaug_scr[hd, B_DH:2 * B_DH, :] = hi_half
        qf = qh.astype(_F32)
        qn2 = jnp.sum(qf * qf, axis=0, keepdims=True)
        score_bound.append(jnp.sqrt(qn2 * kmax2[:, hd // 2:hd // 2 + 1]) * BOUND_SLACK + 1.0)
    m_scr[...] = jnp.full(m_scr.shape, -jnp.inf, _F32)
    acc_scr[...] = jnp.zeros_like(acc_scr)

    def kv_blocks(blocks, mode):
        if mode == "diagonal":
            keep = (lax.broadcasted_iota(jnp.int32, (tk, tq), 1)
                    >= lax.broadcasted_iota(jnp.int32, (tk, tq), 0))
        def scores(unit):
            j, grp = unit
            return [_dot(kaug_ref[j, hd], qaug_scr[hd])
                    for hd in range(grp * HEAD_GROUP, (grp + 1) * HEAD_GROUP)]

        units = [(j, grp) for j in blocks for grp in range(n_groups)]
        s_next = scores(units[0])
        for n, (j, grp) in enumerate(units):
            heads = range(grp * HEAD_GROUP, (grp + 1) * HEAD_GROUP)
            s_cur = s_next
            if n + 1 < len(units):
                s_next = scores(units[n + 1])
            if mode == "fixed":
                for i, hd in enumerate(heads):
                    p_i = jnp.exp2(s_cur[i] - m_scr[hd]).astype(_BF16)
                    acc_scr[hd] = acc_scr[hd] + _dot(vt_ref[j, hd], p_i)
                continue
            if mode == "diagonal":
                s_cur = [jnp.where(keep, s, -jnp.inf) for s in s_cur]
            m_old = [m_scr[hd] for hd in heads]
            m_new = [jnp.maximum(mo, jnp.max(s, axis=0, keepdims=True))
                     for mo, s in zip(m_old, s_cur)]
            alpha = [jnp.exp2(mo - mn) for mo, mn in zip(m_old, m_new)]
            p = [jnp.exp2(s - mn) for s, mn in zip(s_cur, m_new)]
            for i, hd in enumerate(heads):
                m_scr[hd] = m_new[i]
                acc_scr[hd] = alpha[i] * acc_scr[hd] + _dot(vt_ref[j, hd], p[i].astype(_BF16))

    kv_blocks([qi], "diagonal")
    margin = m_scr[0] + MAX_LOG2_WEIGHT - score_bound[0]
    for hd in range(1, B_HEADS):
        margin = jnp.minimum(margin, m_scr[hd] + MAX_LOG2_WEIGHT - score_bound[hd])
    bounded = jnp.min(margin) >= 0.0

    @pl.when(bounded)
    def _():
        def pair(t, carry):
            kv_blocks([2 * t, 2 * t + 1], "fixed")
            return carry
        lax.fori_loop(0, qi // 2, pair, 0)
        if qi_parity:
            kv_blocks([qi - 1], "fixed")

    @pl.when(jnp.logical_not(bounded))
    def _():
        def single(j, carry):
            kv_blocks([j], "rescale")
            return carry
        lax.fori_loop(0, qi, single, 0)

    for hd in range(B_HEADS):
        o_scr[hd * B_DH:(hd + 1) * B_DH, :] = (
            acc_scr[hd, 0:B_DH, :] / acc_scr[hd, B_DH:B_DH + 1, :]).astype(_BF16)
    out_ref[...] = x_ref[...] + g1_ref[...] * _dot_tn(o_scr[...], wout_ref[...])


def _fox_attn(x, mod, layer, w_q, w_out, kaug, vt, frow, kmax2):
    batch, seq, _ = x.shape
    tq, tk = Q_TILE, KV_TILE
    nkv = seq // tk
    tok = pl.BlockSpec((None, Q_STEP * tq, D_MODEL), lambda b, i: (b, i, 0))
    return pl.pallas_call(
        _fox_attn_kernel,
        grid=(batch, seq // (Q_STEP * tq)),
        in_specs=[
            tok, _mod_spec(layer, 0), _mod_spec(layer, 1), _mod_spec(layer, 2),
            _const_spec((D_MODEL, D_MODEL)),
            pl.BlockSpec((None, nkv, B_HEADS, tk, LANES), lambda b, i: (b, 0, 0, 0, 0)),
            pl.BlockSpec((None, nkv, B_HEADS, V_ROWS, tk), lambda b, i: (b, 0, 0, 0, 0)),
            pl.BlockSpec((None, nkv, B_HEADS, tk), lambda b, i: (b, 0, 0, 0)),
            pl.BlockSpec((None, 1, LANES), lambda b, i: (b, 0, 0)),
            _const_spec((D_MODEL, D_MODEL)),
        ],
        out_specs=tok,
        out_shape=jax.ShapeDtypeStruct(x.shape, _F32),
        scratch_shapes=[
            pltpu.VMEM((B_HEADS, 2 * B_DH, tq), _BF16),
            pltpu.VMEM((B_HEADS, 1, tq), _F32),
            pltpu.VMEM((B_HEADS, V_ROWS, tq), _F32), pltpu.VMEM((D_MODEL, tq), _BF16),
        ],
        compiler_params=_params(2),
        name="fox_attn",
    )(x, mod, mod, mod, w_q.T.astype(_BF16), kaug, vt, frow, kmax2, w_out.astype(_BF16))


def kernel(x, c, ada_w, ada_b, a_w_in, a_b_i, a_b_f, a_head_gain, a_w_out, kv_gain, b_w_kv,
           b_fg_bias, b_w_q, b_w_out, mlp_w1, mlp_w2, final_gain):
    batch, seq, d = x.shape
    assert d == D_MODEL and Q_TILE == KV_TILE
    assert all(seq % t == 0
               for t in (MLP_TILE, MLSTM_CHUNK, Q_STEP * Q_TILE, KV_STEP * KV_TILE))
    mod = _adaln_table(c, ada_w, ada_b).reshape(DEPTH, 6, batch, 1, D_MODEL)
    w1_all, w2_all = mlp_w1.astype(_BF16), mlp_w2.astype(_BF16)
    a_w_all = a_w_in.astype(_BF16)
    a_wt_all = jnp.swapaxes(a_w_all, 1, 2)
    shared = None
    for l in range(DEPTH):
        if l < N_A_LAYERS:
            x = _mlstm_layer(x, mod, l, a_w_all, a_wt_all, a_b_i[l], a_b_f[l], a_head_gain[l],
                             a_w_out[l])
        else:
            if shared is None:
                shared = _fox_kv(x, kv_gain, b_w_kv, b_fg_bias)
            j = l - N_A_LAYERS
            x = _fox_attn(x, mod, l, b_w_q[j], b_w_out[j], *shared)
        x = _mlp(x, mod, l, w1_all, w2_all, final_gain, final_norm=(l == DEPTH - 1))
    return x
```

```python
import functools

import jax
import jax.numpy as jnp
import numpy as np
from jax import lax
from jax.experimental import pallas as pl
from jax.experimental.pallas import tpu as pltpu

D_MODEL = 1024
DEPTH = 4
N_A_LAYERS = DEPTH // 2
A_HEADS = 4
A_DV = D_MODEL // A_HEADS
A_DQK = A_DV // 2
A_QK_W = A_HEADS * A_DQK
A_V_W = A_HEADS * A_DV
B_HEADS = 16
B_DH = D_MODEL // B_HEADS
D_FF = 4 * D_MODEL
EPS = 1e-6
LOG2E = 1.4426950408889634
LANES = 128
BF16_ROWS = 16
AV_ROWS = A_DV + BF16_ROWS
V_ROWS = B_DH + BF16_ROWS

ADALN_SLOTS = 2
MLSTM_CHUNK = 256
KV_TILE = 256
KV_STEP = 4
Q_TILE = 256
Q_STEP = 4
HEAD_GROUP = 4
Q_PROJ_HEADS = 4
MLP_TILE = 1024
FF_CHUNK = 512
VMEM_LIMIT = 56 * 1024 * 1024

_BF16 = jnp.bfloat16
_F32 = jnp.float32


def _dot(a, b):
    return jnp.dot(a, b, preferred_element_type=_F32)


def _dot_nt(a, b):
    return lax.dot_general(a, b, (((1,), (1,)), ((), ())), preferred_element_type=_F32)


def _dot_tn(a, b):
    return lax.dot_general(a, b, (((0,), (0,)), ((), ())), preferred_element_type=_F32)


def _rms(x):
    return x * lax.rsqrt(jnp.mean(x * x, axis=-1, keepdims=True) + EPS)


def _modulate(x, shift, scale):
    return _rms(x) * (1.0 + scale) + shift


def _log_sigmoid(z):
    return jnp.minimum(z, 0.0) - jnp.log1p(jnp.exp(-jnp.abs(z)))


def _segment_cumsum(x, axis, seg):
    pos = lax.broadcasted_iota(jnp.int32, x.shape, axis) & (seg - 1)
    k = 1
    while k < seg:
        x = x + jnp.where(pos >= k, pltpu.roll(x, k, axis), 0.0)
        k *= 2
    return x


def _params(n_grid):
    return pltpu.CompilerParams(dimension_semantics=("arbitrary",) * n_grid,
                                vmem_limit_bytes=VMEM_LIMIT)


def _const_spec(shape):
    return pl.BlockSpec(shape, lambda *_: (0,) * len(shape), pipeline_mode=pl.Buffered(1))


def _mod_spec(layer, slot):
    return pl.BlockSpec((None, None, None, 1, D_MODEL), lambda b, i: (layer, slot, b, 0, 0))


def _adaln_kernel(c_ref, w_ref, b_ref, o_ref):
    c = c_ref[...]
    cond = (c * jax.nn.sigmoid(c)).astype(_BF16)
    for s in range(ADALN_SLOTS):
        cols = slice(s * D_MODEL, (s + 1) * D_MODEL)
        o_ref[s] = _dot(cond, w_ref[:, cols].astype(_BF16)) + b_ref[s]


def _adaln_table(c, ada_w, ada_b):
    batch = c.shape[0]
    ns = ADALN_SLOTS
    return pl.pallas_call(
        _adaln_kernel,
        grid=(DEPTH, 6 // ns),
        in_specs=[
            pl.BlockSpec((batch, D_MODEL), lambda l, j: (0, 0)),
            pl.BlockSpec((None, D_MODEL, ns * D_MODEL), lambda l, j: (l, 0, j)),
            pl.BlockSpec((None, ns, 1, D_MODEL), lambda l, j: (l, j, 0, 0)),
        ],
        out_specs=pl.BlockSpec((None, ns, batch, D_MODEL), lambda l, j: (l, j, 0, 0)),
        out_shape=jax.ShapeDtypeStruct((DEPTH, 6, batch, D_MODEL), _F32),
        compiler_params=_params(2),
        name="adaln_table",
    )(c, ada_w, ada_b.reshape(DEPTH, 6, 1, D_MODEL))


def _mlstm_project_qkg(h, w, dst):
    wqt_ref, wk_ref, _, _, wg_ref, bcol_ref = w
    qt_s, k_s, _, _, gcol_s, grow_s = dst
    ct = MLSTM_CHUNK
    qt_s[...] = (_dot_nt(wqt_ref[...], h) * (A_DQK ** -0.5)).astype(_BF16)
    k_s[...] = _dot(h, wk_ref[...]).astype(_BF16)
    z = _dot(h, wg_ref[...]) + bcol_ref[...]
    bc = _segment_cumsum(_log_sigmoid(z), 0, ct)
    gcol_s[...] = (z - pltpu.roll(bc, LANES - A_HEADS, 1)) * LOG2E
    zr = z.T[:2 * A_HEADS, :]
    br = _segment_cumsum(_log_sigmoid(zr), 1, ct)
    sub = lax.broadcasted_iota(jnp.int32, zr.shape, 0)
    grow_s[...] = jnp.where(sub < A_HEADS, zr, br) * LOG2E


def _mlstm_project_v(h, w, dst):
    vt = _dot_nt(w[2][...], h).astype(_BF16)
    for hd in range(A_HEADS):
        dst[2][hd, 0:A_DV, :] = vt[hd * A_DV:(hd + 1) * A_DV, :]


def _mlstm_project_o(h, w, dst):
    dst[3][...] = _dot_nt(w[3][...], h)


def _mlstm_layer_kernel(xcur_ref, xprev_ref, sh_ref, sc_ref, g1prev_ref, g1cur_ref, wqt_ref, wk_ref,
                        wvt_ref, wot_ref, wg_ref, bcol_ref, gain_ref, wout_ref,
                        out_ref, qt_a, k_a, vt_a, ot_a, gcol_a, grow_a, qt_b, k_b, vt_b, ot_b,
                        gcol_b, grow_b, cn_scr, m_scr, z_scr, hold_scr, *, n_chunks):
    L = MLSTM_CHUNK
    heads = range(A_HEADS)
    t = pl.program_id(0)
    weights = (wqt_ref, wk_ref, wvt_ref, wot_ref, wg_ref, bcol_ref)
    set_a = (qt_a, k_a, vt_a, ot_a, gcol_a, grow_a)
    set_b = (qt_b, k_b, vt_b, ot_b, gcol_b, grow_b)

    @pl.when(t == 0)
    def _():
        for ref in set_b + (hold_scr, cn_scr, m_scr):
            ref[...] = jnp.zeros_like(ref)
        for vt_s in (vt_a, vt_b):
            vt_s[:, A_DV:AV_ROWS, :] = jnp.ones((A_HEADS, AV_ROWS - A_DV, L), _BF16)

    causal = (lax.broadcasted_iota(jnp.int32, (L, L), 0)
              <= lax.broadcasted_iota(jnp.int32, (L, L), 1))

    def step(dst, src, x_proj, x_res, g1, fresh, store):
        qt_s, k_s, vt_s, ot_s, gcol_s, grow_s = src
        gcol = gcol_s[...]
        grow = grow_s[...]
        kh = [k_s[:, h * A_DQK:(h + 1) * A_DQK] for h in heads]
        qt = [qt_s[h * A_DQK:(h + 1) * A_DQK, :] for h in heads]
        if fresh is None:
            cn_prev = [cn_scr[h] for h in heads]
            m_prev = [m_scr[h][:, 0:1] for h in heads]
        else:
            cn_prev = [jnp.where(fresh, 0.0, cn_scr[h]) for h in heads]
            m_prev = [jnp.where(fresh, 0.0, m_scr[h][:, 0:1]) for h in heads]
        st = [_dot(kh[h], qt[h]) for h in heads]
        inter_mm = [_dot(cn_prev[h].astype(_BF16), qt[h]) for h in heads]

        i_row = [grow[h:h + 1, :] for h in heads]
        b_row = [grow[A_HEADS + h:A_HEADS + h + 1, :] for h in heads]
        d = [jnp.where(causal, gcol[:, h:h + 1] + b_row[h], -jnp.inf) for h in heads]
        inter = [b_row[h] + m_prev[h] for h in heads]
        m_t = [jnp.maximum(inter[h], jnp.max(d[h], axis=0, keepdims=True)) for h in heads]
        a = [(jnp.exp2(d[h] - m_t[h]) * st[h]).astype(_BF16) for h in heads]
        w_inter = [jnp.exp2(inter[h] - m_t[h]) for h in heads]

        hp = _modulate(x_proj, sh_ref[...], sc_ref[...]).astype(_BF16)
        _mlstm_project_qkg(hp, weights, dst)
        _mlstm_project_v(hp, weights, dst)
        nd = [w_inter[h] * inter_mm[h] + _dot(vt_s[h], a[h]) for h in heads]
        _mlstm_project_o(hp, weights, dst)

        b_last = [b_row[h][:, L - 1:L] for h in heads]
        dl = [b_last[h] - b_row[h] + i_row[h] for h in heads]
        m_new = [jnp.maximum(b_last[h] + m_prev[h], jnp.max(dl[h], axis=1, keepdims=True))
                 for h in heads]
        for h in heads:
            rows = slice(h * A_DV, (h + 1) * A_DV)
            den = nd[h][A_DV:A_DV + 1, :]
            ht = nd[h][0:A_DV, :] / jnp.maximum(jnp.abs(den), jnp.exp2(-m_t[h]))
            ht = (ht * lax.rsqrt(jnp.mean(ht * ht, axis=0, keepdims=True) + EPS)
                  * gain_ref[rows, :])
            z_scr[rows, :] = (jax.nn.sigmoid(ot_s[rows, :]) * ht).astype(_BF16)
        store(x_res + g1 * _dot_tn(z_scr[...], wout_ref[...]))
        for h in heads:
            vw = (vt_s[h].astype(_F32) * jnp.exp2(dl[h] - m_new[h])).astype(_BF16)
            decay = jnp.exp2(b_last[h] + m_prev[h] - m_new[h])
            cn_scr[h] = decay * cn_prev[h] + _dot(vw, kh[h])
            m_scr[h] = jnp.broadcast_to(m_new[h], (1, LANES))

    def store_second_half(y):
        out_ref[L:2 * L, :] = y

    def store_held(y):
        hold_scr[...] = y

    out_ref[0:L, :] = hold_scr[...]
    step(set_a, set_b, xcur_ref[0:L, :], xprev_ref[L:2 * L, :], g1prev_ref[...], None,
         store_second_half)
    step(set_b, set_a, xcur_ref[L:2 * L, :], xcur_ref[0:L, :], g1cur_ref[...],
         lax.rem(2 * t, n_chunks) == 0, store_held)


def _mlstm_layer(x, mod, layer, w_all, wt_all, b_i, b_f, head_gain, w_out):
    batch, seq, _ = x.shape
    L = MLSTM_CHUNK
    nc = seq // L
    w = w_all[layer]
    wk = w[:, A_QK_W:2 * A_QK_W]
    wgate = w[:, 2 * A_QK_W + 2 * A_V_W:]
    wg = jnp.pad(wgate, ((0, 0), (0, LANES - 2 * A_HEADS)))
    wt_spec = lambda rows, blk: pl.BlockSpec(
        (None, rows, D_MODEL), lambda *_: (layer, blk, 0), pipeline_mode=pl.Buffered(1))
    bias = jnp.concatenate([b_i, b_f]).astype(_F32)
    bcol = jnp.pad(bias, (0, LANES - 2 * A_HEADS)).reshape(1, LANES)
    gain = jnp.broadcast_to(head_gain.reshape(A_V_W, 1).astype(_F32), (A_V_W, L))

    assert nc % 2 == 0
    ppb = nc // 2
    n_pairs = batch * ppb
    cur_pair = lambda t: jnp.minimum(t, n_pairs - 1)
    prev_pair = lambda t: jnp.maximum(t - 1, 0)
    tok = lambda pair: pl.BlockSpec(
        (None, 2 * L, D_MODEL), lambda t: (pair(t) // ppb, lax.rem(pair(t), ppb), 0))
    modrow = lambda slot, pair: pl.BlockSpec(
        (None, None, None, 1, D_MODEL), lambda t: (layer, slot, pair(t) // ppb, 0, 0))
    scratch_set = [
        pltpu.VMEM((A_QK_W, L), _BF16), pltpu.VMEM((L, A_QK_W), _BF16),
        pltpu.VMEM((A_HEADS, AV_ROWS, L), _BF16), pltpu.VMEM((A_V_W, L), _F32),
        pltpu.VMEM((L, LANES), _F32), pltpu.VMEM((2 * A_HEADS, L), _F32),
    ]
    return pl.pallas_call(
        functools.partial(_mlstm_layer_kernel, n_chunks=nc),
        grid=(n_pairs + 1,),
        in_specs=[
            tok(cur_pair), tok(prev_pair),
            modrow(0, cur_pair), modrow(1, cur_pair), modrow(2, prev_pair), modrow(2, cur_pair),
            wt_spec(A_QK_W, 0), _const_spec((D_MODEL, A_QK_W)),
            wt_spec(A_V_W, 1), wt_spec(A_V_W, 2),
            _const_spec((D_MODEL, LANES)), _const_spec((1, LANES)),
            _const_spec((A_V_W, L)), _const_spec((A_V_W, D_MODEL)),
        ],
        out_specs=tok(prev_pair),
        out_shape=jax.ShapeDtypeStruct(x.shape, _F32),
        scratch_shapes=scratch_set + scratch_set + [
            pltpu.VMEM((A_HEADS, AV_ROWS, A_DQK), _F32),
            pltpu.VMEM((A_HEADS, 1, LANES), _F32),
            pltpu.VMEM((A_V_W, L), _BF16),
            pltpu.VMEM((L, D_MODEL), _F32),
        ],
        compiler_params=_params(1),
        name="mlstm_layer",
    )(x, x, mod, mod, mod, mod, wt_all, wk, wt_all, wt_all, wg, bcol, gain,
      w_out.astype(_BF16))


def _mlp_kernel(x_ref, sh_ref, sc_ref, g_ref, w1_ref, w2_ref, fgain_ref, out_ref, u_scr, *,
                final_norm):
    x = x_ref[...]
    h = _modulate(x, sh_ref[...], sc_ref[...]).astype(_BF16)
    for c in range(D_FF // FF_CHUNK):
        u = jnp.maximum(_dot(h, w1_ref[:, c * FF_CHUNK:(c + 1) * FF_CHUNK]), 0.0)
        u_scr[:, c * FF_CHUNK:(c + 1) * FF_CHUNK] = (u * u).astype(_BF16)
    y = x + g_ref[...] * _dot(u_scr[...], w2_ref[...])
    if final_norm:
        y = _rms(y) * fgain_ref[...]
    out_ref[...] = y


def _layer_spec(layer, rows, cols):
    return pl.BlockSpec((None, rows, cols), lambda *_: (layer, 0, 0),
                        pipeline_mode=pl.Buffered(1))


def _mlp(x, mod, layer, w1, w2, final_gain, final_norm):
    batch, seq, _ = x.shape
    tm = MLP_TILE
    tok = pl.BlockSpec((None, tm, D_MODEL), lambda b, i: (b, i, 0))
    return pl.pallas_call(
        functools.partial(_mlp_kernel, final_norm=final_norm),
        grid=(batch, seq // tm),
        in_specs=[
            tok, _mod_spec(layer, 3), _mod_spec(layer, 4), _mod_spec(layer, 5),
            _layer_spec(layer, D_MODEL, D_FF), _layer_spec(layer, D_FF, D_MODEL),
            _const_spec((1, D_MODEL)),
        ],
        out_specs=tok,
        out_shape=jax.ShapeDtypeStruct(x.shape, _F32),
        scratch_shapes=[pltpu.VMEM((tm, D_FF), _BF16)],
        compiler_params=_params(2),
        name="mlp",
    )(x, mod, mod, mod, w1, w2, final_gain.reshape(1, D_MODEL).astype(_F32))


def _split_bf16(x):
    hi = x.astype(_BF16).astype(_F32)
    rest = x - hi
    mid = rest.astype(_BF16).astype(_F32)
    lo = (rest - mid).astype(_BF16).astype(_F32)
    return hi, mid, lo


N_BIAS = 3
MAX_LOG2_WEIGHT = 60.0
BOUND_SLACK = 1.0 + 2.0 ** -6


def _bias_placement():
    place = np.zeros((LANES, B_HEADS * LANES), np.float32)
    for hd in range(B_HEADS):
        spare = hd * LANES + (0 if hd % 2 else B_DH)
        for piece in range(N_BIAS):
            place[piece * B_HEADS + hd, spare + piece] = -1.0
            place[N_BIAS * B_HEADS, spare + N_BIAS + piece] = 1.0
    return jnp.asarray(place, _BF16)


def _fox_kv_kernel(x_ref, gain_ref, wk_ref, wvt_ref, wf_ref, bcol_ref,
                   place_ref, kaug_ref, vt_ref, frow_ref, kmax2_ref, ccol_scr, crow_scr):
    tm = KV_TILE

    @pl.when(pl.program_id(1) == 0)
    def _():
        ccol_scr[...] = jnp.zeros_like(ccol_scr)
        crow_scr[...] = jnp.zeros_like(crow_scr)
        kmax2_ref[...] = jnp.zeros_like(kmax2_ref)

    ones = jnp.ones((V_ROWS - B_DH, tm), _BF16)
    lane = lax.broadcasted_iota(jnp.int32, (tm, LANES), 1)
    pair_lane = lax.broadcasted_iota(jnp.int32, (1, LANES), 1)
    carry_col = ccol_scr[...]
    carry_row = crow_scr[:, 0:1]
    kmax2 = kmax2_ref[...]
    def project(blk):
        nonlocal carry_col, carry_row
        h = (_rms(x_ref[blk * tm:(blk + 1) * tm, :]) * gain_ref[...]).astype(_BF16)
        vt = _dot_nt(wvt_ref[...], h).astype(_BF16)
        for hd in range(B_HEADS):
            vt_ref[blk, hd, 0:B_DH, :] = vt[hd * B_DH:(hd + 1) * B_DH, :]
            vt_ref[blk, hd, B_DH:V_ROWS, :] = ones
        k = _dot(h, wk_ref[...])
        logf = _log_sigmoid(_dot(h, wf_ref[...]) + bcol_ref[...])
        fc = _segment_cumsum(logf, 0, tm) + carry_col
        carry_col = fc[tm - 1:tm, :]
        fr = _segment_cumsum(logf.T[:B_HEADS, :], 1, tm)
        fr = fr + carry_row
        carry_row = fr[:, tm - 1:tm]
        frow_ref[blk] = fr * LOG2E
        return k, fc

    def augment(blk, k, fc):
        nonlocal kmax2
        hi, mid, lo = _split_bf16(fc * LOG2E)
        pieces = jnp.where(lane < B_HEADS, hi, jnp.where(
            lane < 2 * B_HEADS, pltpu.roll(mid, B_HEADS, 1), jnp.where(
                lane < 3 * B_HEADS, pltpu.roll(lo, 2 * B_HEADS, 1), jnp.where(
                    lane == 3 * B_HEADS, 1.0, 0.0)))).astype(_BF16)
        bias = _dot(pieces, place_ref[...])
        for hd in range(B_HEADS):
            pair, odd = divmod(hd, 2)
            own = (lane >= B_DH) if odd else (lane < B_DH)
            kaug_ref[blk, hd] = jnp.where(
                own, k[:, pair * LANES:(pair + 1) * LANES],
                bias[:, hd * LANES:(hd + 1) * LANES]).astype(_BF16)
        for pair in range(B_HEADS // 2):
            kb = k[:, pair * LANES:(pair + 1) * LANES]
            n2 = jnp.max(jnp.sum(kb * kb, axis=1, keepdims=True), axis=0, keepdims=True)
            kmax2 = jnp.where(pair_lane == pair, jnp.maximum(kmax2, n2), kmax2)

    projected = project(0)
    for blk in range(KV_STEP):
        pending = projected
        if blk + 1 < KV_STEP:
            projected = project(blk + 1)
        augment(blk, *pending)
    ccol_scr[...] = carry_col
    crow_scr[...] = jnp.broadcast_to(carry_row, crow_scr.shape)
    kmax2_ref[...] = kmax2


def _fox_kv(x, kv_gain, w_kv, fg_bias):
    batch, seq, _ = x.shape
    tm = KV_TILE
    w = w_kv.astype(_BF16)
    wk, wvt, wfg = w[:, :D_MODEL], w[:, D_MODEL:2 * D_MODEL].T, w[:, 2 * D_MODEL:]
    wf = jnp.pad(wfg, ((0, 0), (0, LANES - B_HEADS)))
    bcol = jnp.pad(fg_bias.astype(_F32), (0, LANES - B_HEADS)).reshape(1, LANES)
    ks = KV_STEP
    return pl.pallas_call(
        _fox_kv_kernel,
        grid=(batch, seq // (ks * tm)),
        in_specs=[
            pl.BlockSpec((None, ks * tm, D_MODEL), lambda b, i: (b, i, 0)),
            _const_spec((1, D_MODEL)),
            _const_spec((D_MODEL, D_MODEL)), _const_spec((D_MODEL, D_MODEL)),
            _const_spec((D_MODEL, LANES)), _const_spec((1, LANES)),
            _const_spec((LANES, B_HEADS * LANES)),
        ],
        out_specs=[
            pl.BlockSpec((None, ks, B_HEADS, tm, LANES), lambda b, i: (b, i, 0, 0, 0)),
            pl.BlockSpec((None, ks, B_HEADS, V_ROWS, tm), lambda b, i: (b, i, 0, 0, 0)),
            pl.BlockSpec((None, ks, B_HEADS, tm), lambda b, i: (b, i, 0, 0)),
            pl.BlockSpec((None, 1, LANES), lambda b, i: (b, 0, 0)),
        ],
        out_shape=[
            jax.ShapeDtypeStruct((batch, seq // tm, B_HEADS, tm, LANES), _BF16),
            jax.ShapeDtypeStruct((batch, seq // tm, B_HEADS, V_ROWS, tm), _BF16),
            jax.ShapeDtypeStruct((batch, seq // tm, B_HEADS, tm), _F32),
            jax.ShapeDtypeStruct((batch, 1, LANES), _F32),
        ],
        scratch_shapes=[pltpu.VMEM((1, LANES), _F32), pltpu.VMEM((B_HEADS, LANES), _F32)],
        compiler_params=_params(2),
        name="fox_kv",
    )(x, kv_gain.reshape(1, D_MODEL).astype(_F32), wk, wvt, wf, bcol, _bias_placement())


def _fox_attn_kernel(x_ref, sh_ref, sc_ref, g1_ref, wqt_ref, kaug_ref, vt_ref, frow_ref,
                     kmax2_ref, wout_ref, out_ref, qaug_scr, m_scr, acc_scr, o_scr):
    assert Q_STEP % 2 == 0
    for s in range(Q_STEP):
        rows = pl.ds(s * Q_TILE, Q_TILE)
        _fox_attn_block(pl.program_id(1) * Q_STEP + s, s % 2, x_ref.at[rows], sh_ref, sc_ref,
                        g1_ref, wqt_ref, kaug_ref, vt_ref, frow_ref, kmax2_ref, wout_ref,
                        out_ref.at[rows], qaug_scr, m_scr, acc_scr, o_scr)


def _fox_attn_block(qi, qi_parity, x_ref, sh_ref, sc_ref, g1_ref, wqt_ref, kaug_ref, vt_ref,
                    frow_ref, kmax2_ref, wout_ref, out_ref, qaug_scr, m_scr, acc_scr, o_scr):
    tq, tk = Q_TILE, KV_TILE
    half = tk // 2
    n_groups = B_HEADS // HEAD_GROUP
    h = _modulate(x_ref[...], sh_ref[...], sc_ref[...]).astype(_BF16)
    ft = frow_ref[qi]
    sub = lax.broadcasted_iota(jnp.int32, (B_DH, tq), 0)
    slab = Q_PROJ_HEADS * B_DH
    qt = [(_dot_nt(wqt_ref[r * slab:(r + 1) * slab, :], h)
           * (B_DH ** -0.5 * LOG2E)).astype(_BF16)
          for r in range(B_HEADS // Q_PROJ_HEADS)]
    kmax2 = kmax2_ref[...]
    score_bound = []
    for hd in range(B_HEADS):
        hi, mid, lo = _split_bf16(ft[hd:hd + 1, :])
        bias = jnp.where(sub < N_BIAS, 1.0, jnp.where(sub == N_BIAS, hi, jnp.where(
            sub == N_BIAS + 1, mid, jnp.where(sub == N_BIAS + 2, lo, 0.0)))).astype(_BF16)
        r, off = divmod(hd, Q_PROJ_HEADS)
        qh = qt[r][off * B_DH:(off + 1) * B_DH, :]
        lo_half, hi_half = (bias, qh) if hd % 2 else (qh, bias)
        qaug_scr[hd, 0:B_DH, :] = lo_half
        qaug_scr[hd, B_DH:2 * B_DH, :] = hi_half
        qf = qh.astype(_F32)
        qn2 = jnp.sum(qf * qf, axis=0, keepdims=True)
        score_bound.append(jnp.sqrt(qn2 * kmax2[:, hd // 2:hd // 2 + 1]) * BOUND_SLACK + 1.0)
    m_scr[...] = jnp.full(m_scr.shape, -jnp.inf, _F32)
    acc_scr[...] = jnp.zeros_like(acc_scr)

    def kv_blocks(blocks, mode):
        if mode == "diagonal":
            keep = (lax.broadcasted_iota(jnp.int32, (tk, tq), 1)
                    >= lax.broadcasted_iota(jnp.int32, (tk, tq), 0))
        def scores(unit):
            j, grp = unit
            return [_dot(kaug_ref[j, hd], qaug_scr[hd])
                    for hd in range(grp * HEAD_GROUP, (grp + 1) * HEAD_GROUP)]

        units = [(j, grp) for j in blocks for grp in range(n_groups)]
        s_next = scores(units[0])
        for n, (j, grp) in enumerate(units):
            heads = range(grp * HEAD_GROUP, (grp + 1) * HEAD_GROUP)
            s_cur = s_next
            if n + 1 < len(units):
                s_next = scores(units[n + 1])
            if mode == "fixed":
                for i, hd in enumerate(heads):
                    p_i = jnp.exp2(s_cur[i] - m_scr[hd]).astype(_BF16)
                    acc_scr[hd] = acc_scr[hd] + _dot(vt_ref[j, hd], p_i)
                continue
            if mode == "diagonal":
                tops = [jnp.where(keep[0:half, :], s[0:half, :], -jnp.inf) for s in s_cur]
                bots = [jnp.where(keep[0:half, 0:half], s[half:tk, half:tq], -jnp.inf)
                        for s in s_cur]
                m_old = [m_scr[hd] for hd in heads]
                m_new = []
                for mo, top, bot in zip(m_old, tops, bots):
                    mt = jnp.max(top, axis=0, keepdims=True)
                    mb = jnp.max(bot, axis=0, keepdims=True)
                    m_blk = jnp.concatenate(
                        [mt[:, 0:half], jnp.maximum(mt[:, half:tq], mb)], axis=1)
                    m_new.append(jnp.maximum(mo, m_blk))
                alpha = [jnp.exp2(mo - mn) for mo, mn in zip(m_old, m_new)]
                for i, hd in enumerate(heads):
                    p_top = jnp.exp2(tops[i] - m_new[i]).astype(_BF16)
                    p_bot = jnp.exp2(bots[i] - m_new[i][:, half:tq]).astype(_BF16)
                    pv_bot = _dot(vt_ref[j, hd, :, half:tk], p_bot)
                    m_scr[hd] = m_new[i]
                    acc_scr[hd] = (alpha[i] * acc_scr[hd] + _dot(vt_ref[j, hd, :, 0:half], p_top)
                                   + jnp.concatenate([jnp.zeros_like(pv_bot), pv_bot], axis=1))
                continue
            m_old = [m_scr[hd] for hd in heads]
            m_new = [jnp.maximum(mo, jnp.max(s, axis=0, keepdims=True))
                     for mo, s in zip(m_old, s_cur)]
            alpha = [jnp.exp2(mo - mn) for mo, mn in zip(m_old, m_new)]
            p = [jnp.exp2(s - mn) for s, mn in zip(s_cur, m_new)]
            for i, hd in enumerate(heads):
                m_scr[hd] = m_new[i]
                acc_scr[hd] = alpha[i] * acc_scr[hd] + _dot(vt_ref[j, hd], p[i].astype(_BF16))

    kv_blocks([qi], "diagonal")
    margin = m_scr[0] + MAX_LOG2_WEIGHT - score_bound[0]
    for hd in range(1, B_HEADS):
        margin = jnp.minimum(margin, m_scr[hd] + MAX_LOG2_WEIGHT - score_bound[hd])
    bounded = jnp.min(margin) >= 0.0

    @pl.when(bounded)
    def _():
        def pair(t, carry):
            kv_blocks([2 * t, 2 * t + 1], "fixed")
            return carry
        lax.fori_loop(0, qi // 2, pair, 0)
        if qi_parity:
            kv_blocks([qi - 1], "fixed")

    @pl.when(jnp.logical_not(bounded))
    def _():
        def single(j, carry):
            kv_blocks([j], "rescale")
            return carry
        lax.fori_loop(0, qi, single, 0)

    for hd in range(B_HEADS):
        o_scr[hd * B_DH:(hd + 1) * B_DH, :] = (
            acc_scr[hd, 0:B_DH, :] / acc_scr[hd, B_DH:B_DH + 1, :]).astype(_BF16)
    out_ref[...] = x_ref[...] + g1_ref[...] * _dot_tn(o_scr[...], wout_ref[...])


def _fox_attn(x, mod, layer, w_q, w_out, kaug, vt, frow, kmax2):
    batch, seq, _ = x.shape
    tq, tk = Q_TILE, KV_TILE
    nkv = seq // tk
    tok = pl.BlockSpec((None, Q_STEP * tq, D_MODEL), lambda b, i: (b, i, 0))
    return pl.pallas_call(
        _fox_attn_kernel,
        grid=(batch, seq // (Q_STEP * tq)),
        in_specs=[
            tok, _mod_spec(layer, 0), _mod_spec(layer, 1), _mod_spec(layer, 2),
            _const_spec((D_MODEL, D_MODEL)),
            pl.BlockSpec((None, nkv, B_HEADS, tk, LANES), lambda b, i: (b, 0, 0, 0, 0)),
            pl.BlockSpec((None, nkv, B_HEADS, V_ROWS, tk), lambda b, i: (b, 0, 0, 0, 0)),
            pl.BlockSpec((None, nkv, B_HEADS, tk), lambda b, i: (b, 0, 0, 0)),
            pl.BlockSpec((None, 1, LANES), lambda b, i: (b, 0, 0)),
            _const_spec((D_MODEL, D_MODEL)),
        ],
        out_specs=tok,
        out_shape=jax.ShapeDtypeStruct(x.shape, _F32),
        scratch_shapes=[
            pltpu.VMEM((B_HEADS, 2 * B_DH, tq), _BF16),
            pltpu.VMEM((B_HEADS, 1, tq), _F32),
            pltpu.VMEM((B_HEADS, V_ROWS, tq), _F32), pltpu.VMEM((D_MODEL, tq), _BF16),
        ],
        compiler_params=_params(2),
        name="fox_attn",
    )(x, mod, mod, mod, w_q.T.astype(_BF16), kaug, vt, frow, kmax2, w_out.astype(_BF16))


def kernel(x, c, ada_w, ada_b, a_w_in, a_b_i, a_b_f, a_head_gain, a_w_out, kv_gain, b_w_kv,
           b_fg_bias, b_w_q, b_w_out, mlp_w1, mlp_w2, final_gain):
    batch, seq, d = x.shape
    assert d == D_MODEL and Q_TILE == KV_TILE
    assert all(seq % t == 0
               for t in (MLP_TILE, MLSTM_CHUNK, Q_STEP * Q_TILE, KV_STEP * KV_TILE))
    mod = _adaln_table(c, ada_w, ada_b).reshape(DEPTH, 6, batch, 1, D_MODEL)
    w1_all, w2_all = mlp_w1.astype(_BF16), mlp_w2.astype(_BF16)
    a_w_all = a_w_in.astype(_BF16)
    a_wt_all = jnp.swapaxes(a_w_all, 1, 2)
    shared = None
    for l in range(DEPTH):
        if l < N_A_LAYERS:
            x = _mlstm_layer(x, mod, l, a_w_all, a_wt_all, a_b_i[l], a_b_f[l], a_head_gain[l],
                             a_w_out[l])
        else:
            if shared is None:
                shared = _fox_kv(x, kv_gain, b_w_kv, b_fg_bias)
            j = l - N_A_LAYERS
            x = _fox_attn(x, mod, l, b_w_q[j], b_w_out[j], *shared)
        x = _mlp(x, mod, l, w1_all, w2_all, final_gain, final_norm=(l == DEPTH - 1))
    return x
```
